```python
import jax, jax.numpy as jnp
from jax import lax
import numpy as np

D_MODEL = 1024
BATCH = 2
SEQ = 8192
DEPTH = 2

GRID_W = 64
CTX_LEN = 256
N_EVEN = (DEPTH + 1) // 2
N_ODD = DEPTH // 2
BLOCK = 128
NORM_EPS = 1e-6
ROPE_THETA = 10000.0
NEG_INF = -1e30
F32 = jnp.float32

HQ_A = 8
HKV_A = 2
DH_A = 64
WINDOW = 128
H_B = 4
DK_B = 128
DV_B = 128
CONV_K = 5
CHUNK = 64
HQ_C = 8
HKV_C = 2
DH_C = 128
N_EXPERTS = 16
CAP_FACTOR = 2
EXPERT_FF = 2048

A_Q = HQ_A * DH_A
A_KV = HKV_A * DH_A
A_COLS = A_Q + 2 * A_KV
B_QK = H_B * DK_B
B_V = H_B * DV_B
B_CONV = 2 * B_QK + B_V
B_COLS = B_CONV + B_V + 2 * H_B + 2 * H_B
IN_AB = A_COLS + B_COLS
OUT_AB = A_Q + B_V
C_Q = HQ_C * DH_C
C_KV = HKV_C * DH_C
IN_C = C_Q + 2 * C_KV
OUT_C = C_Q

kernel_name = "hybrid_diffusion_swa_gdn_gqa_ecmoe"


def rms_norm(x, gain):
    xf = x.astype(F32)
    y = xf * lax.rsqrt(jnp.mean(xf * xf, axis=-1, keepdims=True) + NORM_EPS)
    return (y * gain.astype(F32)).astype(x.dtype)


def l2_norm(x):
    xf = x.astype(F32)
    return xf * lax.rsqrt(jnp.sum(xf * xf, axis=-1, keepdims=True) + NORM_EPS)


def modulate(h, shift, scale):
    return h * (1 + scale) + shift


def axial_rope_tables(rows, head_dim):
    n = rows * GRID_W
    t = jnp.arange(n)
    row = (t // GRID_W).astype(F32)
    col = (t % GRID_W).astype(F32)
    quarter = head_dim // 4
    inv = ROPE_THETA ** (-jnp.arange(quarter, dtype=F32) / quarter)
    ang = jnp.stack([row[:, None] * inv, col[:, None] * inv], axis=1)
    return jnp.cos(ang), jnp.sin(ang)


def apply_axial_rope(x, cos, sin):
    B, n, H, hd = x.shape
    xf = x.astype(F32).reshape(B, n, H, 2, 2, hd // 4)
    x0, x1 = xf[..., 0, :], xf[..., 1, :]
    cs, sn = cos[:, None], sin[:, None]
    out = jnp.stack([x0 * cs - x1 * sn, x1 * cs + x0 * sn], axis=-2)
    return out.reshape(B, n, H, hd).astype(x.dtype)


def context_attention(q, k, v, sink=None):
    B, L, Hq, dh = q.shape
    Hkv = k.shape[2]
    G = Hq // Hkv
    qg = q.reshape(B, L, Hkv, G, dh)
    s = jnp.einsum('blhgd,bchd->bhglc', qg, k, preferred_element_type=F32) * dh ** -0.5
    if sink is not None:
        sk = jnp.broadcast_to(sink.astype(F32).reshape(1, Hkv, G, 1, 1), s.shape[:-1] + (1,))
        p = jax.nn.softmax(jnp.concatenate([sk, s], axis=-1), axis=-1)[..., 1:]
    else:
        p = jax.nn.softmax(s, axis=-1)
    o = jnp.einsum('bhglc,bchd->blhgd', p.astype(v.dtype), v)
    return o.reshape(B, L, Hq * dh)


def window_sink_attention(q, k, v, k_ctx, v_ctx, sink):
    B, S, Hq, dh = q.shape
    G = Hq // HKV_A
    nb = S // BLOCK
    L = k_ctx.shape[1]
    scale = dh ** -0.5
    qb = q.reshape(B, nb, BLOCK, HKV_A, G, dh)

    def band(t):
        tp = jnp.pad(t, ((0, 0), (BLOCK, BLOCK), (0, 0), (0, 0))).reshape(B, nb + 2, BLOCK, HKV_A, dh)
        return jnp.concatenate([tp[:, :-2], tp[:, 1:-1], tp[:, 2:]], axis=2)

    kb, vb = band(k), band(v)
    s_band = jnp.einsum('bnqhgd,bnkhd->bnhgqk', qb, kb, preferred_element_type=F32) * scale
    qi = jnp.arange(BLOCK)[:, None]
    kj = jnp.arange(3 * BLOCK)[None, :]
    kpos = (jnp.arange(nb) * BLOCK - BLOCK)[:, None, None] + kj
    valid = (jnp.abs(kj - BLOCK - qi) <= WINDOW) & (kpos >= 0) & (kpos < S)
    s_band = jnp.where(valid[None, :, None, None], s_band, NEG_INF)
    s_ctx = jnp.einsum('bnqhgd,bchd->bnhgqc', qb, k_ctx, preferred_element_type=F32) * scale
    sk = jnp.broadcast_to(sink.astype(F32).reshape(1, 1, HKV_A, G, 1, 1), s_ctx.shape[:-1] + (1,))
    p = jax.nn.softmax(jnp.concatenate([sk, s_ctx, s_band], axis=-1), axis=-1)
    p_ctx = p[..., 1:1 + L].astype(v.dtype)
    p_band = p[..., 1 + L:].astype(v.dtype)
    o = (jnp.einsum('bnhgqc,bchd->bnqhgd', p_ctx, v_ctx)
         + jnp.einsum('bnhgqk,bnkhd->bnqhgd', p_band, vb))
    return o.reshape(B, S, Hq * dh)


def global_block_attention(q, k, v, k_ctx, v_ctx):
    B, S, Hq, dh = q.shape
    G = Hq // HKV_C
    nb = S // BLOCK
    k_all = jnp.concatenate([k_ctx, k], axis=1)
    v_all = jnp.concatenate([v_ctx, v], axis=1)
    qb = jnp.moveaxis(q.reshape(B, nb, BLOCK, HKV_C, G, dh), 1, 0)

    def one_block(qblk):
        s = jnp.einsum('bqhgd,bkhd->bhgqk', qblk, k_all, preferred_element_type=F32) * dh ** -0.5
        p = jax.nn.softmax(s, axis=-1).astype(v_all.dtype)
        return jnp.einsum('bhgqk,bkhd->bqhgd', p, v_all)

    o = lax.map(one_block, qb)
    return jnp.moveaxis(o, 0, 1).reshape(B, S, Hq * dh)


def centred_depthwise_conv(x, w):
    ch = x.shape[-1]
    return lax.conv_general_dilated(
        x, w[:, None, :].astype(x.dtype), window_strides=(1,),
        padding=[(CONV_K // 2, CONV_K // 2)],
        dimension_numbers=('NWC', 'WIO', 'NWC'), feature_group_count=ch)


def chunk_gated_delta(q, k, v, g, beta, state0):
    B, T, H, dk = q.shape
    dv = v.shape[-1]
    n = T // CHUNK

    def chunks(t):
        return jnp.moveaxis(t.reshape((B, n, CHUNK, H) + t.shape[3:]), 3, 1)

    q, k, v, g, beta = (chunks(t) for t in (q, k, v, g, beta))
    gc = jnp.cumsum(g, axis=-1)
    lower = jnp.tril(jnp.ones((CHUNK, CHUNK), bool))
    strict = jnp.tril(jnp.ones((CHUNK, CHUNK), bool), -1)
    decay = jnp.where(lower, jnp.exp(jnp.where(lower, gc[..., :, None] - gc[..., None, :], 0.0)), 0.0)
    kb = k * beta[..., None]
    vb = v * beta[..., None]
    a_kk = jnp.where(strict, jnp.einsum('bhnid,bhnjd->bhnij', kb, k) * decay, 0.0)
    eye = jnp.eye(CHUNK, dtype=F32)
    tinv = lax.linalg.triangular_solve(a_kk + eye, jnp.broadcast_to(eye, a_kk.shape),
                                       left_side=True, lower=True, unit_diagonal=True)
    u = jnp.einsum('bhnij,bhnjd->bhnid', tinv, vb)
    w = jnp.einsum('bhnij,bhnjd->bhnid', tinv, kb * jnp.exp(gc)[..., None])
    qg = q * jnp.exp(gc)[..., None]
    a_qk = jnp.einsum('bhnid,bhnjd->bhnij', q, k) * decay
    kd = k * jnp.exp(gc[..., -1:] - gc)[..., None]
    g_end = jnp.exp(gc[..., -1])
    xs = tuple(jnp.moveaxis(t, 2, 0) for t in (qg, w, u, a_qk, kd, g_end))

    def step(s, inp):
        qg_i, w_i, u_i, a_i, kd_i, ge_i = inp
        v_new = u_i - jnp.einsum('bhcd,bhde->bhce', w_i, s)
        o = jnp.einsum('bhcd,bhde->bhce', qg_i, s) + jnp.einsum('bhij,bhje->bhie', a_i, v_new)
        s = s * ge_i[..., None, None] + jnp.einsum('bhcd,bhce->bhde', kd_i, v_new)
        return s, o

    s_final, o = lax.scan(step, state0, xs)
    o = jnp.transpose(o, (1, 0, 3, 2, 4)).reshape(B, T, H, dv)
    return o, s_final


def gated_deltanet(p_lat, p_ctx, conv_w, a_log, dt_bias, onorm):
    def prep(p):
        B, T, _ = p.shape
        qkv = jax.nn.silu(centred_depthwise_conv(p[..., :B_CONV], conv_w))
        q = l2_norm(qkv[..., :B_QK].reshape(B, T, H_B, DK_B)) * DK_B ** -0.5
        k = l2_norm(qkv[..., B_QK:2 * B_QK].reshape(B, T, H_B, DK_B))
        v = qkv[..., 2 * B_QK:].reshape(B, T, H_B, DV_B).astype(F32)
        z = p[..., B_CONV:B_CONV + B_V]
        a = p[..., B_CONV + B_V:B_CONV + B_V + 2 * H_B].astype(F32).reshape(B, T, 2, H_B)
        b = p[..., B_CONV + B_V + 2 * H_B:].astype(F32).reshape(B, T, 2, H_B)
        g = -jnp.exp(a_log.astype(F32)) * jax.nn.softplus(a + dt_bias.astype(F32))
        beta = jax.nn.sigmoid(b)
        return q, k, v, z, g, beta

    ql, kl, vl, zl, gl, bl = prep(p_lat)
    qc, kc, vc, zc, gcx, bc = prep(p_ctx)
    B = p_lat.shape[0]
    zero = jnp.zeros((B, H_B, DK_B, DV_B), F32)

    def flip(t):
        return jnp.flip(t, axis=1)

    oc_f, s_f = chunk_gated_delta(qc, kc, vc, gcx[:, :, 0], bc[:, :, 0], zero)
    ol_f, _ = chunk_gated_delta(ql, kl, vl, gl[:, :, 0], bl[:, :, 0], s_f)
    oc_b, s_b = chunk_gated_delta(flip(qc), flip(kc), flip(vc), flip(gcx[:, :, 1]), flip(bc[:, :, 1]), zero)
    ol_b, _ = chunk_gated_delta(flip(ql), flip(kl), flip(vl), flip(gl[:, :, 1]), flip(bl[:, :, 1]), s_b)

    def out_gate(o, z):
        B_, T, _ = z.shape
        y = rms_norm(o, onorm) * jax.nn.silu(z.astype(F32)).reshape(B_, T, H_B, DV_B)
        return y.reshape(B_, T, B_V).astype(z.dtype)

    return out_gate(ol_f + flip(ol_b), zl), out_gate(oc_f + flip(oc_b), zc)


def mixer_even(h_lat, h_ctx, w_in, w_out, qn, kn, sink, conv_w, a_log, dt_bias, onorm, cos, sin, with_ctx):
    B, S, _ = h_lat.shape
    L = h_ctx.shape[1]
    p_lat = h_lat @ w_in
    p_ctx = h_ctx @ w_in

    def qkv_a(p, n):
        q = rms_norm(p[..., :A_Q].reshape(B, n, HQ_A, DH_A), qn)
        k = rms_norm(p[..., A_Q:A_Q + A_KV].reshape(B, n, HKV_A, DH_A), kn)
        v = p[..., A_Q + A_KV:A_COLS].reshape(B, n, HKV_A, DH_A)
        return q, k, v

    ql, kl, vl = qkv_a(p_lat, S)
    qc, kc, vc = qkv_a(p_ctx, L)
    ql = apply_axial_rope(ql, cos, sin)
    kl = apply_axial_rope(kl, cos, sin)
    oa_lat = window_sink_attention(ql, kl, vl, kc, vc, sink)
    ob_lat, ob_ctx = gated_deltanet(p_lat[..., A_COLS:], p_ctx[..., A_COLS:], conv_w, a_log, dt_bias, onorm)
    y_lat = jnp.concatenate([oa_lat, ob_lat], axis=-1) @ w_out
    y_ctx = None
    if with_ctx:
        oa_ctx = context_attention(qc, kc, vc, sink)
        y_ctx = jnp.concatenate([oa_ctx, ob_ctx], axis=-1) @ w_out
    return y_lat, y_ctx


def mixer_odd(h_lat, h_ctx, w_in, w_out, qn, kn, cos, sin, with_ctx):
    B, S, _ = h_lat.shape
    L = h_ctx.shape[1]
    p_lat = h_lat @ w_in
    ql = apply_axial_rope(rms_norm(p_lat[..., :C_Q].reshape(B, S, HQ_C, DH_C), qn), cos, sin)
    kl = apply_axial_rope(rms_norm(p_lat[..., C_Q:C_Q + C_KV].reshape(B, S, HKV_C, DH_C), kn), cos, sin)
    vl = p_lat[..., C_Q + C_KV:].reshape(B, S, HKV_C, DH_C)
    kv_ctx = h_ctx @ w_in[:, C_Q:]
    kc = rms_norm(kv_ctx[..., :C_KV].reshape(B, L, HKV_C, DH_C), kn)
    vc = kv_ctx[..., C_KV:].reshape(B, L, HKV_C, DH_C)
    y_lat = global_block_attention(ql, kl, vl, kc, vc) @ w_out
    y_ctx = None
    if with_ctx:
        qc = rms_norm((h_ctx @ w_in[:, :C_Q]).reshape(B, L, HQ_C, DH_C), qn)
        y_ctx = context_attention(qc, kc, vc) @ w_out
    return y_lat, y_ctx


def expert_choice_moe(h, w_router, w_gate, w_up, w_down):
    B, n, D = h.shape
    cap = max(1, CAP_FACTOR * n // N_EXPERTS)
    logits = jnp.einsum('bnd,de->bne', h, w_router, preferred_element_type=F32)
    aff = jax.nn.softmax(logits, axis=-1)
    gate, idx = lax.top_k(jnp.swapaxes(aff, 1, 2), cap)
    xs = jax.vmap(lambda hb, ib: hb[ib])(h, idx)
    hid = jax.nn.silu(jnp.einsum('becd,edf->becf', xs, w_gate)) * jnp.einsum('becd,edf->becf', xs, w_up)
    ye = jnp.einsum('becf,efd->becd', hid, w_down) * gate[..., None].astype(h.dtype)
    return jax.vmap(lambda ib, yb: jnp.zeros((n, D), h.dtype).at[ib.reshape(-1)].add(yb.reshape(-1, D)))(idx, ye)


def setup_inputs(seed: int = 0) -> dict:
    key = jax.random.key(seed)
    ks = jax.random.split(key, 26)
    D = D_MODEL

    def nrm(k, shape, s):
        return jax.random.normal(k, shape, F32) * s

    def gain(k, shape):
        return 1.0 + 0.02 * jax.random.normal(k, shape, F32)

    dt = jnp.exp(jax.random.uniform(ks[15], (N_EVEN, 2, H_B), F32,
                                    float(np.log(1e-3)), float(np.log(1e-1))))
    return {
        "x": nrm(ks[0], (BATCH, SEQ, D), 1.0),
        "c": nrm(ks[1], (BATCH, D), 1.0),
        "ctx": nrm(ks[2], (BATCH, CTX_LEN, D), 1.0),
        "c_ctx": nrm(ks[3], (D,), 1.0),
        "w_mod": nrm(ks[4], (DEPTH, D, 6 * D), 0.02),
        "b_mod": nrm(ks[5], (DEPTH, 6 * D), 0.02),
        "norm_mix": gain(ks[6], (DEPTH, D)),
        "norm_ffn": gain(ks[7], (DEPTH, D)),
        "w_in_ab": nrm(ks[8], (N_EVEN, D, IN_AB), D ** -0.5),
        "w_out_ab": nrm(ks[9], (N_EVEN, OUT_AB, D), OUT_AB ** -0.5),
        "qnorm_a": gain(ks[10], (N_EVEN, DH_A)),
        "knorm_a": gain(ks[11], (N_EVEN, DH_A)),
        "sink_a": nrm(ks[12], (N_EVEN, HQ_A), 0.5),
        "conv_b": nrm(ks[13], (N_EVEN, CONV_K, B_CONV), CONV_K ** -0.5),
        "a_log_b": jnp.log(jax.random.uniform(ks[14], (N_EVEN, 2, H_B), F32, 1.0, 16.0)),
        "dt_bias_b": dt + jnp.log(-jnp.expm1(-dt)),
        "onorm_b": gain(ks[16], (N_EVEN, DV_B)),
        "w_in_c": nrm(ks[17], (N_ODD, D, IN_C), D ** -0.5),
        "w_out_c": nrm(ks[18], (N_ODD, OUT_C, D), OUT_C ** -0.5),
        "qnorm_c": gain(ks[19], (N_ODD, DH_C)),
        "knorm_c": gain(ks[20], (N_ODD, DH_C)),
        "w_router": nrm(ks[21], (DEPTH, D, N_EXPERTS), D ** -0.5),
        "w_gate": nrm(ks[22], (DEPTH, N_EXPERTS, D, EXPERT_FF), D ** -0.5),
        "w_up": nrm(ks[23], (DEPTH, N_EXPERTS, D, EXPERT_FF), D ** -0.5),
        "w_down": nrm(ks[24], (DEPTH, N_EXPERTS, EXPERT_FF, D), EXPERT_FF ** -0.5),
    }


def reference(x, c, ctx, c_ctx, w_mod, b_mod, norm_mix, norm_ffn, w_in_ab, w_out_ab, qnorm_a, knorm_a,
              sink_a, conv_b, a_log_b, dt_bias_b, onorm_b, w_in_c, w_out_c, qnorm_c, knorm_c,
              w_router, w_gate, w_up, w_down):
    rows = x.shape[1] // GRID_W
    cos_a, sin_a = axial_rope_tables(rows, DH_A)
    cos_c, sin_c = axial_rope_tables(rows, DH_C)
    xl, xc = x, ctx
    for i in range(DEPTH):
        last = i == DEPTH - 1
        j = i // 2
        m_lat = (jax.nn.silu(c) @ w_mod[i] + b_mod[i])[:, None, :]
        m_ctx = jax.nn.silu(c_ctx) @ w_mod[i] + b_mod[i]
        sh1, sc1, gt1, sh2, sc2, gt2 = jnp.split(m_lat, 6, axis=-1)
        sh1c, sc1c, gt1c, sh2c, sc2c, gt2c = jnp.split(m_ctx, 6, axis=-1)
        hl = modulate(rms_norm(xl, norm_mix[i]), sh1, sc1)
        hc = modulate(rms_norm(xc, norm_mix[i]), sh1c, sc1c)
        if i % 2 == 0:
            yl, yc = mixer_even(hl, hc, w_in_ab[j], w_out_ab[j], qnorm_a[j], knorm_a[j], sink_a[j],
                                conv_b[j], a_log_b[j], dt_bias_b[j], onorm_b[j], cos_a, sin_a, not last)
        else:
            yl, yc = mixer_odd(hl, hc, w_in_c[j], w_out_c[j], qnorm_c[j], knorm_c[j], cos_c, sin_c, not last)
        xl = xl + gt1 * yl
        hl2 = modulate(rms_norm(xl, norm_ffn[i]), sh2, sc2)
        xl = xl + gt2 * expert_choice_moe(hl2, w_router[i], w_gate[i], w_up[i], w_down[i])
        if not last:
            xc = xc + gt1c * yc
            hc2 = modulate(rms_norm(xc, norm_ffn[i]), sh2c, sc2c)
            xc = xc + gt2c * expert_choice_moe(hc2, w_router[i], w_gate[i], w_up[i], w_down[i])
    return xl
```

```python
import functools

import jax
import jax.numpy as jnp
from jax import lax
from jax.experimental import pallas as pl
from jax.experimental.pallas import tpu as pltpu

F32 = jnp.float32
BF16 = jnp.bfloat16
HIGHEST = lax.Precision.HIGHEST

GRID_W = 64
NORM_EPS = 1e-6
ROPE_THETA = 10000.0
NEG_INF = -1e30
HQ_A, HKV_A, DH_A, WINDOW = 8, 2, 64, 128
H_B, DK_B, CONV_K, CHUNK = 4, 128, 5, 64
HQ_C, HKV_C, DH_C = 8, 2, 128
N_EXPERTS, CAP_FACTOR = 16, 2

LANES = 128
SUBLANES = 8
VMEM_LIMIT = 56 * 2 ** 20

ROW_TILE = 256
Q_TILE = 128
KV_TILE = 768
FF_TILE = 512


def _params(*sem):
    return pltpu.CompilerParams(dimension_semantics=sem, vmem_limit_bytes=VMEM_LIMIT)


def _silu(x):
    return x * (1.0 / (1.0 + jnp.exp(-x)))


def _sigmoid(x):
    return 1.0 / (1.0 + jnp.exp(-x))


def _norm_mod(x, gain, shift, scale):
    ms = jnp.mean(x * x, axis=-1, keepdims=True)
    y = x * lax.rsqrt(ms + NORM_EPS) * gain
    return y * (1.0 + scale) + shift


def _segment_mean_square(p, ones_bd, seg):
    sq = p * p
    hi = sq.astype(BF16)
    lo = (sq - hi.astype(F32)).astype(BF16)
    s = (jnp.dot(hi, ones_bd, preferred_element_type=F32)
         + jnp.dot(lo, ones_bd, preferred_element_type=F32))
    return s * (1.0 / seg)


def _rope(x, cos, sin_signed, dist):
    n = x.shape[-1]
    lane = lax.broadcasted_iota(jnp.int32, x.shape, 1)
    up = pltpu.roll(x, n - dist, 1)
    dn = pltpu.roll(x, dist, 1)
    partner = jnp.where((lane & dist) == 0, up, dn)
    return x * cos + partner * sin_signed


def _mod_kernel(c_ref, w_ref, b_ref, o_ref):
    s = _silu(c_ref[...])
    o_ref[0] = jnp.dot(s, w_ref[0], precision=HIGHEST, preferred_element_type=F32) + b_ref[0]


def _modulation(cvec, w_mod, b_mod):
    depth, d, n6 = w_mod.shape
    tn = 1536
    return pl.pallas_call(
        _mod_kernel,
        grid=(depth, n6 // tn),
        in_specs=[pl.BlockSpec((SUBLANES, d), lambda l, j: (0, 0)),
                  pl.BlockSpec((1, d, tn), lambda l, j: (l, 0, j)),
                  pl.BlockSpec((1, 1, tn), lambda l, j: (l, 0, j))],
        out_specs=pl.BlockSpec((1, SUBLANES, tn), lambda l, j: (l, 0, j)),
        out_shape=jax.ShapeDtypeStruct((depth, SUBLANES, n6), F32),
        compiler_params=_params("parallel", "parallel"),
        name="modulation",
    )(cvec, w_mod, b_mod.reshape(depth, 1, n6))


def _mod_spec(n_ctx_tiles, tile0=0):
    return lambda b, i: (jnp.where(i + tile0 < n_ctx_tiles, 0, 1 + b), 0, 0)


def _inproj_even_kernel(x_ref, mod_ref, gain_ref, w_ref, cos_ref, sin_ref, qn_ref, kn_ref, ones_ref,
                        q_ref, k_ref, v_ref, pc_ref, z_ref, ab_ref):
    h = _norm_mod(x_ref[0], gain_ref[...], mod_ref[0, 0:1, :], mod_ref[0, 1:2, :]).astype(BF16)

    def proj(lo, hi):
        return jnp.dot(h, w_ref[:, lo:hi], preferred_element_type=F32)

    c = cos_ref[...]
    s = sin_ref[...]
    nq, nk = HQ_A * DH_A, 2 * HKV_A * DH_A
    q = proj(0, nq)
    q = q * lax.rsqrt(_segment_mean_square(q, ones_ref[...], DH_A) + NORM_EPS) * qn_ref[...]
    q = _rope(q, jnp.concatenate([c] * (nq // LANES), axis=1),
              jnp.concatenate([s] * (nq // LANES), axis=1), DH_A // 4)
    q_ref[0] = (q * DH_A ** -0.5).astype(BF16)
    k = proj(nq, nq + nk)
    k = k * lax.rsqrt(_segment_mean_square(k, ones_ref[0:nk, 0:nk], DH_A) + NORM_EPS) * kn_ref[...]
    k = _rope(k, jnp.concatenate([c] * (nk // LANES), axis=1),
              jnp.concatenate([s] * (nk // LANES), axis=1), DH_A // 4)
    k_ref[0] = k.astype(BF16)
    o = nq + nk
    v_ref[0] = proj(o, o + nk).astype(BF16)
    o += nk
    nconv = pc_ref.shape[2]
    pc_ref[0] = proj(o, o + nconv)
    o += nconv
    nz = z_ref.shape[2]
    z_ref[0] = proj(o, o + nz)
    o += nz
    ab_ref[0] = proj(o, o + LANES)


def _inproj_even(xs, mod, gain, w, cos, sin, qn, kn, ones_bd, n_ctx):
    b, t, d = xs.shape
    tm = ROW_TILE
    nq, nk = HQ_A * DH_A, 2 * HKV_A * DH_A
    nconv, nz = 3 * H_B * DK_B, H_B * DK_B
    row = lambda w_: pl.BlockSpec((1, tm, w_), lambda bb, i: (bb, i, 0))
    const = lambda a: pl.BlockSpec(a.shape, lambda bb, i: (0,) * a.ndim)
    return pl.pallas_call(
        _inproj_even_kernel,
        grid=(b, t // tm),
        in_specs=[row(d), pl.BlockSpec((1, 6, d), _mod_spec(n_ctx // tm)), const(gain), const(w),
                  pl.BlockSpec((tm, LANES), lambda bb, i: (i, 0)),
                  pl.BlockSpec((tm, LANES), lambda bb, i: (i, 0)),
                  const(qn), const(kn), const(ones_bd)],
        out_specs=[row(nq), row(nk), row(nk), row(nconv), row(nz), row(LANES)],
        out_shape=[jax.ShapeDtypeStruct((b, t, nq), BF16), jax.ShapeDtypeStruct((b, t, nk), BF16),
                   jax.ShapeDtypeStruct((b, t, nk), BF16), jax.ShapeDtypeStruct((b, t, nconv), F32),
                   jax.ShapeDtypeStruct((b, t, nz), F32), jax.ShapeDtypeStruct((b, t, LANES), F32)],
        compiler_params=_params("parallel", "parallel"),
        name="inproj_even",
    )(xs, mod, gain, w, cos, sin, qn, kn, ones_bd)


def _attn_a_kernel(sink_ref, q_ref, k_ref, v_ref, o_ref, *, n_ctx, t_all):
    i = pl.program_id(1)
    g_heads = HQ_A // HKV_A
    band = 3 * Q_TILE
    n = i - n_ctx // Q_TILE
    start = jnp.clip(n_ctx + (n - 1) * Q_TILE, 0, t_all - band)
    start = pl.multiple_of(start, Q_TILE)
    q = q_ref[0]
    rows = g_heads * Q_TILE
    lane = lax.broadcasted_iota(jnp.int32, (Q_TILE, LANES), 1)
    qpos = n * Q_TILE + (lax.broadcasted_iota(jnp.int32, (rows, band), 0) & (Q_TILE - 1))
    kpos = (start - n_ctx) + lax.broadcasted_iota(jnp.int32, (rows, band), 1)
    valid = (n >= 0) & (jnp.abs(kpos - qpos) <= WINDOW) & (kpos >= 0)
    dims = (((1,), (1,)), ((), ()))
    for h in range(HKV_A):
        cols = slice(LANES * h, LANES * (h + 1))
        parts, sinks = [], []
        for g in range(g_heads):
            j = g_heads * h + g
            tile = q[:, LANES * (j // 2):LANES * (j // 2 + 1)]
            keep = (lane >= DH_A * (j % 2)) & (lane < DH_A * (j % 2 + 1))
            parts.append(jnp.where(keep, tile, jnp.zeros_like(tile)))
            sinks.append(jnp.full((Q_TILE, 1), sink_ref[j], F32))
        qs = jnp.concatenate(parts, axis=0)
        sk = jnp.concatenate(sinks, axis=0)
        kc = k_ref[0, 0:n_ctx, cols]
        vc = v_ref[0, 0:n_ctx, cols]
        kb = k_ref[0, pl.ds(start, band), cols]
        vb = v_ref[0, pl.ds(start, band), cols]
        s_c = lax.dot_general(qs, kc, dims, preferred_element_type=F32)
        s_b = lax.dot_general(qs, kb, dims, preferred_element_type=F32)
        s_b = jnp.where(valid, s_b, NEG_INF)
        m = jnp.maximum(jnp.maximum(s_c.max(axis=1, keepdims=True), s_b.max(axis=1, keepdims=True)), sk)
        p_c = jnp.exp(s_c - m)
        p_b = jnp.exp(s_b - m)
        den = p_c.sum(axis=1, keepdims=True) + p_b.sum(axis=1, keepdims=True) + jnp.exp(sk - m)
        o = (jnp.dot(p_c.astype(BF16), vc, preferred_element_type=F32)
             + jnp.dot(p_b.astype(BF16), vb, preferred_element_type=F32)) / den
        for pair in range(g_heads // 2):
            lo = o[(2 * pair) * Q_TILE:(2 * pair + 1) * Q_TILE]
            hi = o[(2 * pair + 1) * Q_TILE:(2 * pair + 2) * Q_TILE]
            c0 = LANES * (g_heads // 2 * h + pair)
            o_ref[0, :, c0:c0 + LANES] = jnp.where(lane < DH_A, lo, hi).astype(BF16)


def _attn_a(sink, q, k, v, n_ctx):
    b, t, nq = q.shape
    nk = k.shape[2]
    return pl.pallas_call(
        functools.partial(_attn_a_kernel, n_ctx=n_ctx, t_all=t),
        grid=(b, t // Q_TILE),
        in_specs=[pl.BlockSpec(memory_space=pltpu.SMEM),
                  pl.BlockSpec((1, Q_TILE, nq), lambda bb, i: (bb, i, 0)),
                  pl.BlockSpec((1, t, nk), lambda bb, i: (bb, 0, 0)),
                  pl.BlockSpec((1, t, nk), lambda bb, i: (bb, 0, 0))],
        out_specs=pl.BlockSpec((1, Q_TILE, nq), lambda bb, i: (bb, i, 0)),
        out_shape=jax.ShapeDtypeStruct((b, t, nq), BF16),
        compiler_params=_params("parallel", "arbitrary"),
        name="attn_window",
    )(sink, q, k, v)


def _gdn_prep_kernel(pc_ref, prev_ref, next_ref, cw_ref, ab_ref, alog_ref, dtb_ref,
                     q_ref, k_ref, v_ref, gb_ref, ext_sc, *, n_ctx, t_all):
    tm = pc_ref.shape[1]
    r0 = pl.program_id(1) * tm
    halo = SUBLANES
    prev_on = jnp.where((r0 == 0) | (r0 == n_ctx), 0.0, 1.0)
    next_on = jnp.where((r0 + tm == n_ctx) | (r0 + tm == t_all), 0.0, 1.0)
    ext_sc[0:halo, :] = prev_ref[0] * prev_on
    ext_sc[halo:halo + tm, :] = pc_ref[0]
    ext_sc[halo + tm:2 * halo + tm, :] = next_ref[0] * next_on
    nh = H_B * DK_B
    for grp, out_ref in enumerate((q_ref, k_ref, v_ref)):
        c0 = nh * grp
        acc = None
        for tap in range(CONV_K):
            off = halo - CONV_K // 2 + tap
            term = cw_ref[tap:tap + 1, c0:c0 + nh] * ext_sc[off:off + tm, c0:c0 + nh]
            acc = term if acc is None else acc + term
        y = _silu(acc)
        if grp == 2:
            out_ref[0] = y
            continue
        scale = DK_B ** -0.5 if grp == 0 else 1.0
        for h in range(H_B):
            yh = y[:, DK_B * h:DK_B * (h + 1)]
            inv = lax.rsqrt(jnp.sum(yh * yh, axis=-1, keepdims=True) + NORM_EPS)
            out_ref[0, :, DK_B * h:DK_B * (h + 1)] = yh * (inv * scale)
    ab = ab_ref[0]
    lane = lax.broadcasted_iota(jnp.int32, ab.shape, 1)
    xg = ab + dtb_ref[...]
    softplus = jnp.maximum(xg, 0.0) + jnp.log(1.0 + jnp.exp(-jnp.abs(xg)))
    g = -jnp.exp(alog_ref[...]) * softplus
    gb_ref[0] = jnp.where(lane < 2 * H_B, g, jnp.where(lane < 4 * H_B, _sigmoid(ab), 0.0))


def _gdn_prep(pc, conv_w, ab, alog, dtb, n_ctx):
    b, t, nconv = pc.shape
    tm = ROW_TILE
    nh = H_B * DK_B
    hb = tm // SUBLANES
    nblk = t // SUBLANES
    row = lambda w_: pl.BlockSpec((1, tm, w_), lambda bb, i: (bb, i, 0))
    const = lambda a: pl.BlockSpec(a.shape, lambda bb, i: (0,) * a.ndim)
    return pl.pallas_call(
        functools.partial(_gdn_prep_kernel, n_ctx=n_ctx, t_all=t),
        grid=(b, t // tm),
        in_specs=[row(nconv),
                  pl.BlockSpec((1, SUBLANES, nconv), lambda bb, i: (bb, jnp.maximum(i * hb - 1, 0), 0)),
                  pl.BlockSpec((1, SUBLANES, nconv),
                               lambda bb, i: (bb, jnp.minimum((i + 1) * hb, nblk - 1), 0)),
                  const(conv_w), row(LANES), const(alog), const(dtb)],
        out_specs=[row(nh), row(nh), row(nh), row(LANES)],
        out_shape=[jax.ShapeDtypeStruct((b, t, nh), F32)] * 3 + [jax.ShapeDtypeStruct((b, t, LANES), F32)],
        scratch_shapes=[pltpu.VMEM((tm + 2 * SUBLANES, nconv), F32)],
        compiler_params=_params("parallel", "parallel"),
        name="gdn_prep",
    )(pc, pc, pc, conv_w, ab, alog, dtb)


def _gdn_chunk_kernel(q_ref, k_ref, v_ref, gb_ref, u_ref, w_ref, qg_ref, a_ref, kdt_ref, ge_ref):
    c = CHUNK
    nchunks = q_ref.shape[1] // c
    r_i = lax.broadcasted_iota(jnp.int32, (c, c), 0)
    c_i = lax.broadcasted_iota(jnp.int32, (c, c), 1)
    tri_l = (r_i >= c_i).astype(F32)
    tri_u = (r_i <= c_i).astype(F32)
    row2 = lax.broadcasted_iota(jnp.int32, (c, LANES), 0)
    col2 = lax.broadcasted_iota(jnp.int32, (c, LANES), 1)
    colm = col2 & (c - 1)
    lane8 = lax.broadcasted_iota(jnp.int32, (1, LANES), 1)
    half_of = [(col2 // c) == (h % 2) for h in range(H_B)]
    eye_stack = jnp.concatenate(
        [jnp.where(half_of[h] & (row2 == colm), 1.0, 0.0) for h in range(H_B)], axis=0)

    def bmm(ls, rs):
        rcat = jnp.concatenate([rs[0:2 * c], rs[2 * c:4 * c]], axis=1).astype(BF16)
        full = jnp.dot(ls.astype(BF16), rcat, preferred_element_type=F32)
        return jnp.concatenate([full[0:2 * c, 0:LANES], full[2 * c:4 * c, LANES:2 * LANES]], axis=0)

    def wide(t_stack, mats):
        rv = jnp.concatenate([jnp.concatenate(mats[0:2], axis=0),
                              jnp.concatenate(mats[2:4], axis=0)], axis=1).astype(BF16)
        full = jnp.dot(t_stack.astype(BF16), rv, preferred_element_type=F32)
        return [full[c * h:c * (h + 1), LANES * (h // 2):LANES * (h // 2 + 1)] for h in range(H_B)]

    dims = (((1,), (1,)), ((), ()))
    for ci in range(nchunks):
        rows = slice(c * ci, c * (ci + 1))
        gb = gb_ref[0, rows, :]
        gc = jnp.where(lane8 < H_B,
                       jnp.dot(tri_l, gb, precision=HIGHEST, preferred_element_type=F32),
                       jnp.dot(tri_u, gb, precision=HIGHEST, preferred_element_type=F32))
        gc_t = gc.T
        eg = jnp.exp(gc)
        g_last = jnp.where(lane8 < H_B, gc[c - 1:c, :], gc[0:1, :])
        ge_ref[0, ci] = jnp.broadcast_to(jnp.exp(g_last), (SUBLANES, LANES))
        ek = jnp.exp(g_last - gc)
        qs = [q_ref[0, rows, DK_B * h:DK_B * (h + 1)] for h in range(H_B)]
        ks = [k_ref[0, rows, DK_B * h:DK_B * (h + 1)] for h in range(H_B)]
        vs = [v_ref[0, rows, DK_B * h:DK_B * (h + 1)] for h in range(H_B)]
        raw = []
        for h in range(H_B):
            kq = jnp.concatenate([ks[h], qs[h]], axis=0).astype(BF16)
            kk = jnp.concatenate([ks[h], ks[h]], axis=0).astype(BF16)
            raw.append(lax.dot_general(kq, kk, dims, preferred_element_type=F32))
        for d in range(2):
            keep = (row2 >= colm) if d == 0 else (row2 <= colm)
            strict = (row2 > colm) if d == 0 else (row2 < colm)
            a_blocks, aqk, betas, egs = [], [], [], []
            for h in range(H_B):
                idx = H_B * d + h
                g_col = gc[:, idx:idx + 1]
                g_row = jnp.concatenate([gc_t[idx:idx + 1, :]] * 2, axis=1)
                decay = jnp.where(keep, jnp.exp(jnp.where(keep, g_col - g_row, 0.0)), 0.0)
                beta = gb[:, 2 * H_B + idx:2 * H_B + idx + 1]
                betas.append(beta)
                egs.append(eg[:, idx:idx + 1])
                a_blocks.append(jnp.where(strict & half_of[h], beta * raw[h][0:c] * decay, 0.0))
                aqk.append(raw[h][c:2 * c] * decay)
            x = -jnp.concatenate(a_blocks, axis=0)
            t_inv = eye_stack + x
            p = x
            for _ in range(5):
                p = bmm(p, p)
                t_inv = t_inv + bmm(t_inv, p)
            us = wide(t_inv, [vs[h] * betas[h] for h in range(H_B)])
            ws = wide(t_inv, [ks[h] * (betas[h] * egs[h]) for h in range(H_B)])
            for h in range(H_B):
                idx = H_B * d + h
                cols = slice(DK_B * h, DK_B * (h + 1))
                u_ref[d, 0, rows, cols] = us[h]
                w_ref[d, 0, rows, cols] = ws[h].astype(BF16)
                qg_ref[d, 0, rows, cols] = (qs[h] * egs[h]).astype(BF16)
            for pair in range(H_B // 2):
                h0, h1 = 2 * pair, 2 * pair + 1
                a_ref[d, 0, rows, LANES * pair:LANES * (pair + 1)] = jnp.where(
                    col2 < c, aqk[h0], aqk[h1]).astype(BF16)
                kd0 = (ks[h0] * ek[:, H_B * d + h0:H_B * d + h0 + 1]).T
                kd1 = (ks[h1] * ek[:, H_B * d + h1:H_B * d + h1 + 1]).T
                kdt_ref[d, 0, ci, :, LANES * pair:LANES * (pair + 1)] = jnp.concatenate(
                    [kd0, kd1], axis=1).astype(BF16)


def _gdn_chunk(qb, kb, vb, gb):
    b, t, nh = qb.shape
    tm = ROW_TILE
    cps = tm // CHUNK
    nck = t // CHUNK
    row = lambda w_: pl.BlockSpec((1, tm, w_), lambda bb, i: (bb, i, 0))
    drow = lambda w_: pl.BlockSpec((2, 1, tm, w_), lambda bb, i: (0, bb, i, 0))
    return pl.pallas_call(
        _gdn_chunk_kernel,
        grid=(b, t // tm),
        in_specs=[row(nh), row(nh), row(nh), row(LANES)],
        out_specs=[drow(nh), drow(nh), drow(nh), drow(nh // 2),
                   pl.BlockSpec((2, 1, cps, DK_B, nh // 2), lambda bb, i: (0, bb, i, 0, 0)),
                   pl.BlockSpec((1, cps, SUBLANES, LANES), lambda bb, i: (bb, i, 0, 0))],
        out_shape=[jax.ShapeDtypeStruct((2, b, t, nh), F32), jax.ShapeDtypeStruct((2, b, t, nh), BF16),
                   jax.ShapeDtypeStruct((2, b, t, nh), BF16),
                   jax.ShapeDtypeStruct((2, b, t, nh // 2), BF16),
                   jax.ShapeDtypeStruct((2, b, nck, DK_B, nh // 2), BF16),
                   jax.ShapeDtypeStruct((b, nck, SUBLANES, LANES), F32)],
        compiler_params=_params("parallel", "parallel"),
        name="gdn_chunk",
    )(qb, kb, vb, gb)


def _gdn_scan_kernel(u_ref, w_ref, qg_ref, a_ref, kdt_ref, ge_ref, o_ref, s_sc):
    d = pl.program_id(1)

    @pl.when(pl.program_id(2) == 0)
    def _():
        s_sc[...] = jnp.zeros_like(s_sc)

    ge = ge_ref[0, 0]
    zero = jnp.zeros((CHUNK, DK_B), BF16)
    for h in range(H_B):
        cols = slice(DK_B * h, DK_B * (h + 1))
        pair = slice(LANES * (h // 2), LANES * (h // 2 + 1))
        s = s_sc[h]
        wq = jnp.concatenate([w_ref[0, 0, :, cols], qg_ref[0, 0, :, cols]], axis=0)
        r = jnp.dot(wq, s.astype(BF16), preferred_element_type=F32)
        v_new = (u_ref[0, 0, :, cols] - r[0:CHUNK]).astype(BF16)
        v_pad = jnp.concatenate([v_new, zero] if h % 2 == 0 else [zero, v_new], axis=0)
        o_ref[0, 0, :, cols] = r[CHUNK:2 * CHUNK] + jnp.dot(
            a_ref[0, 0, :, pair], v_pad, preferred_element_type=F32)
        g_end = jnp.where(d == 0, ge[0:1, h:h + 1], ge[0:1, H_B + h:H_B + h + 1])
        s_sc[h] = s * g_end + jnp.dot(kdt_ref[0, 0, 0, :, pair], v_pad, preferred_element_type=F32)


def _gdn_scan(u, w, qg, a, kdt, ge, n_ctx):
    _, b, t, nh = u.shape
    nck = t // CHUNK
    ncc = n_ctx // CHUNK

    def chunk(d, s):
        back = jnp.where(s < ncc, ncc - 1 - s, nck - 1 - (s - ncc))
        return jnp.where(d == 0, s, back)

    drow = lambda w_: pl.BlockSpec((1, 1, CHUNK, w_), lambda bb, d, s: (d, bb, chunk(d, s), 0))
    return pl.pallas_call(
        _gdn_scan_kernel,
        grid=(b, 2, nck),
        in_specs=[drow(nh), drow(nh), drow(nh), drow(nh // 2),
                  pl.BlockSpec((1, 1, 1, DK_B, nh // 2), lambda bb, d, s: (d, bb, chunk(d, s), 0, 0)),
                  pl.BlockSpec((1, 1, SUBLANES, LANES), lambda bb, d, s: (bb, chunk(d, s), 0, 0))],
        out_specs=drow(nh),
        out_shape=jax.ShapeDtypeStruct((2, b, t, nh), F32),
        scratch_shapes=[pltpu.VMEM((H_B, DK_B, DK_B), F32)],
        compiler_params=_params("parallel", "parallel", "arbitrary"),
        name="gdn_scan",
    )(u, w, qg, a, kdt, ge)


def _residual_router(y, x_ref, mod_ref, g2_ref, wr_ref, xn_ref, h2_ref, aff_ref):
    xn = x_ref[0] + mod_ref[0, 2:3, :] * y
    xn_ref[0] = xn
    h2 = _norm_mod(xn, g2_ref[...], mod_ref[0, 3:4, :], mod_ref[0, 4:5, :])
    h2_ref[0] = h2
    logits = jnp.dot(h2, wr_ref[...], precision=HIGHEST, preferred_element_type=F32)
    lane = lax.broadcasted_iota(jnp.int32, logits.shape, 1)
    logits = jnp.where(lane < N_EXPERTS, logits, NEG_INF)
    e = jnp.exp(logits - logits.max(axis=-1, keepdims=True))
    aff_ref[0] = e / e.sum(axis=-1, keepdims=True)


def _outproj_even_kernel(oa_ref, of_ref, ob_ref, z_ref, x_ref, mod_ref, w_ref, onorm_ref, g2_ref, wr_ref,
                         xn_ref, h2_ref, aff_ref):
    na = oa_ref.shape[2]
    y = jnp.dot(oa_ref[0], w_ref[0:na, :], preferred_element_type=F32)
    o = of_ref[0, 0] + ob_ref[0, 0]
    z = z_ref[0]
    for h in range(H_B):
        cols = slice(DK_B * h, DK_B * (h + 1))
        oh = o[:, cols]
        ms = jnp.mean(oh * oh, axis=-1, keepdims=True)
        yh = oh * lax.rsqrt(ms + NORM_EPS) * onorm_ref[...] * _silu(z[:, cols])
        y = y + jnp.dot(yh.astype(BF16), w_ref[na + DK_B * h:na + DK_B * (h + 1), :],
                        preferred_element_type=F32)
    _residual_router(y, x_ref, mod_ref, g2_ref, wr_ref, xn_ref, h2_ref, aff_ref)


def _outproj_odd_kernel(o_ref, x_ref, mod_ref, w_ref, g2_ref, wr_ref, xn_ref, h2_ref, aff_ref):
    y = jnp.dot(o_ref[0], w_ref[...], preferred_element_type=F32)
    _residual_router(y, x_ref, mod_ref, g2_ref, wr_ref, xn_ref, h2_ref, aff_ref)


def _outproj_call(kernel, name, acts, xs, mod, consts, n_ctx, tile0):
    b, t, d = xs.shape
    tm = ROW_TILE
    nt = t // tm - tile0
    specs = []
    for a, lead, off in acts:
        if lead is None:
            specs.append(pl.BlockSpec((1, tm, a.shape[-1]), lambda bb, i, off=off: (bb, i + off, 0)))
        else:
            specs.append(pl.BlockSpec((1, 1, tm, a.shape[-1]),
                                      lambda bb, i, lead=lead, off=off: (lead, bb, i + off, 0)))
    row = pl.BlockSpec((1, tm, d), lambda bb, i: (bb, i + tile0, 0))
    const = lambda a: pl.BlockSpec(a.shape, lambda bb, i: (0,) * a.ndim)
    mod_spec = pl.BlockSpec((1, 6, d), _mod_spec(n_ctx // tm, tile0))
    wout, rest = consts[0], consts[1:]
    return pl.pallas_call(
        kernel,
        grid=(b, nt),
        in_specs=specs + [row, mod_spec, const(wout)] + [const(a) for a in rest],
        out_specs=[pl.BlockSpec((1, tm, d), lambda bb, i: (bb, i, 0)),
                   pl.BlockSpec((1, tm, d), lambda bb, i: (bb, i, 0)),
                   pl.BlockSpec((1, tm, LANES), lambda bb, i: (bb, i, 0))],
        out_shape=[jax.ShapeDtypeStruct((b, nt * tm, d), F32), jax.ShapeDtypeStruct((b, nt * tm, d), F32),
                   jax.ShapeDtypeStruct((b, nt * tm, LANES), F32)],
        compiler_params=_params("parallel", "parallel"),
        name=name,
    )(*[a for a, _, _ in acts], xs, mod, wout, *rest)


def _inproj_odd_kernel(x_ref, mod_ref, gain_ref, w_ref, cos_ref, sin_ref, qn_ref, kn_ref,
                       q_ref, k_ref, v_ref):
    h = _norm_mod(x_ref[0], gain_ref[...], mod_ref[0, 0:1, :], mod_ref[0, 1:2, :]).astype(BF16)
    c = cos_ref[...]
    s = sin_ref[...]
    nq, nk = HQ_C * DH_C, HKV_C * DH_C

    def normed_heads(lo, nheads, gain_ref_, out_ref, scale):
        p = jnp.dot(h, w_ref[:, lo:lo + nheads * DH_C], preferred_element_type=F32)
        for hh in range(nheads):
            ph = p[:, DH_C * hh:DH_C * (hh + 1)]
            ms = jnp.mean(ph * ph, axis=-1, keepdims=True)
            ph = _rope(ph * lax.rsqrt(ms + NORM_EPS) * gain_ref_[...], c, s, DH_C // 4)
            out_ref[0, :, DH_C * hh:DH_C * (hh + 1)] = (ph * scale).astype(BF16)

    normed_heads(0, HQ_C, qn_ref, q_ref, DH_C ** -0.5)
    normed_heads(nq, HKV_C, kn_ref, k_ref, 1.0)
    v_ref[0] = jnp.dot(h, w_ref[:, nq + nk:nq + 2 * nk], preferred_element_type=F32).astype(BF16)


def _inproj_odd(xs, mod, gain, w, cos, sin, qn, kn, n_ctx):
    b, t, d = xs.shape
    tm = ROW_TILE
    nq, nk = HQ_C * DH_C, HKV_C * DH_C
    row = lambda w_: pl.BlockSpec((1, tm, w_), lambda bb, i: (bb, i, 0))
    const = lambda a: pl.BlockSpec(a.shape, lambda bb, i: (0,) * a.ndim)
    return pl.pallas_call(
        _inproj_odd_kernel,
        grid=(b, t // tm),
        in_specs=[row(d), pl.BlockSpec((1, 6, d), _mod_spec(n_ctx // tm)), const(gain), const(w),
                  pl.BlockSpec((tm, LANES), lambda bb, i: (i, 0)),
                  pl.BlockSpec((tm, LANES), lambda bb, i: (i, 0)),
                  const(qn), const(kn)],
        out_specs=[row(nq), row(nk), row(nk)],
        out_shape=[jax.ShapeDtypeStruct((b, t, nq), BF16), jax.ShapeDtypeStruct((b, t, nk), BF16),
                   jax.ShapeDtypeStruct((b, t, nk), BF16)],
        compiler_params=_params("parallel", "parallel"),
        name="inproj_odd",
    )(xs, mod, gain, w, cos, sin, qn, kn)


def _attn_c_kernel(q_ref, k_ref, v_ref, o_ref):
    t_all = k_ref.shape[1]
    g_heads = HQ_C // HKV_C
    rows = g_heads * Q_TILE
    q = q_ref[0]
    dims = (((1,), (1,)), ((), ()))
    for h in range(HKV_C):
        cols = slice(DH_C * h, DH_C * (h + 1))
        qs = jnp.concatenate([q[:, DH_C * (g_heads * h + g):DH_C * (g_heads * h + g + 1)]
                              for g in range(g_heads)], axis=0)

        def body(ci, carry):
            m, l, acc = carry
            r0 = pl.multiple_of(ci * KV_TILE, KV_TILE)
            kk = k_ref[0, pl.ds(r0, KV_TILE), cols]
            vv = v_ref[0, pl.ds(r0, KV_TILE), cols]
            s = lax.dot_general(qs, kk, dims, preferred_element_type=F32)
            m_new = jnp.maximum(m, s.max(axis=1, keepdims=True))
            p = jnp.exp(s - m_new)
            alpha = jnp.exp(m - m_new)
            l = alpha * l + p.sum(axis=1, keepdims=True)
            acc = alpha * acc + jnp.dot(p.astype(BF16), vv, preferred_element_type=F32)
            return m_new, l, acc

        init = (jnp.full((rows, 1), NEG_INF, F32), jnp.zeros((rows, 1), F32),
                jnp.zeros((rows, DH_C), F32))
        _, l, acc = lax.fori_loop(0, t_all // KV_TILE, body, init)
        o = acc / l
        for g in range(g_heads):
            j = g_heads * h + g
            o_ref[0, :, DH_C * j:DH_C * (j + 1)] = o[Q_TILE * g:Q_TILE * (g + 1)].astype(BF16)


def _attn_c(q, k, v, n_ctx):
    b, t, nq = q.shape
    nk = k.shape[2]
    t0 = n_ctx // Q_TILE
    return pl.pallas_call(
        _attn_c_kernel,
        grid=(b, t // Q_TILE - t0),
        in_specs=[pl.BlockSpec((1, Q_TILE, nq), lambda bb, i: (bb, i + t0, 0)),
                  pl.BlockSpec((1, t, nk), lambda bb, i: (bb, 0, 0)),
                  pl.BlockSpec((1, t, nk), lambda bb, i: (bb, 0, 0))],
        out_specs=pl.BlockSpec((1, Q_TILE, nq), lambda bb, i: (bb, i, 0)),
        out_shape=jax.ShapeDtypeStruct((b, t - n_ctx, nq), BF16),
        compiler_params=_params("parallel", "arbitrary"),
        name="attn_global",
    )(q, k, v)


def _expert_ffn_kernel(x_ref, wg_ref, wu_ref, wd_ref, y_ref):
    x = x_ref[0].astype(BF16)
    ff = wg_ref.shape[2]
    acc = None
    for f0 in range(0, ff, FF_TILE):
        g = jnp.dot(x, wg_ref[0, :, f0:f0 + FF_TILE], preferred_element_type=F32)
        u = jnp.dot(x, wu_ref[0, :, f0:f0 + FF_TILE], preferred_element_type=F32)
        hid = (_silu(g) * u).astype(BF16)
        part = jnp.dot(hid, wd_ref[0, f0:f0 + FF_TILE, :], preferred_element_type=F32)
        acc = part if acc is None else acc + part
    y_ref[0] = acc


def _expert_ffn(xs, wg, wu, wd):
    e, r, d = xs.shape
    ff = wg.shape[2]
    nr = 2 if r % (2 * SUBLANES) == 0 else 1
    tr = r // nr
    return pl.pallas_call(
        _expert_ffn_kernel,
        grid=(e, nr),
        in_specs=[pl.BlockSpec((1, tr, d), lambda ee, i: (ee, i, 0)),
                  pl.BlockSpec((1, d, ff), lambda ee, i: (ee, 0, 0)),
                  pl.BlockSpec((1, d, ff), lambda ee, i: (ee, 0, 0)),
                  pl.BlockSpec((1, ff, d), lambda ee, i: (ee, 0, 0))],
        out_specs=pl.BlockSpec((1, tr, d), lambda ee, i: (ee, i, 0)),
        out_shape=jax.ShapeDtypeStruct((e, r, d), F32),
        compiler_params=_params("parallel", "arbitrary"),
        name="expert_ffn",
    )(xs, wg, wu, wd)


def _route(aff, cap):
    gate, idx = lax.top_k(jnp.swapaxes(aff, 1, 2), cap)
    return gate, idx


def _moe(groups, wg, wu, wd):
    e = wg.shape[0]
    gathered, meta = [], []
    for h2, aff in groups:
        b, n, d = h2.shape
        cap = max(1, CAP_FACTOR * n // e)
        gate, idx = _route(aff, cap)
        xs = jax.vmap(lambda hb, ib: hb[ib])(h2, idx)
        gathered.append(jnp.swapaxes(xs, 0, 1).reshape(e, b * cap, d))
        meta.append((b, n, d, cap, gate, idx))
    ys = _expert_ffn(jnp.concatenate(gathered, axis=1), wg, wu, wd)
    outs, r0 = [], 0
    for b, n, d, cap, gate, idx in meta:
        ye = jnp.swapaxes(ys[:, r0:r0 + b * cap].reshape(e, b, cap, d), 0, 1) * gate[..., None]
        r0 += b * cap
        outs.append(jax.vmap(lambda ib, yb: jnp.zeros((n, d), F32).at[ib.reshape(-1)].add(
            yb.reshape(-1, d)))(idx, ye))
    return outs


def _rope_tables(s_len, n_ctx, head_dim):
    quarter = head_dim // 4
    t = jnp.arange(s_len)
    row = (t // GRID_W).astype(F32)
    col = (t % GRID_W).astype(F32)
    inv = ROPE_THETA ** (-jnp.arange(quarter, dtype=F32) / quarter)
    ar, ac = row[:, None] * inv, col[:, None] * inv
    cos = jnp.concatenate([jnp.cos(ar), jnp.cos(ar), jnp.cos(ac), jnp.cos(ac)], axis=1)
    sin = jnp.concatenate([-jnp.sin(ar), jnp.sin(ar), -jnp.sin(ac), jnp.sin(ac)], axis=1)
    reps = LANES // head_dim
    cos, sin = jnp.tile(cos, (1, reps)), jnp.tile(sin, (1, reps))
    cos = jnp.concatenate([jnp.ones((n_ctx, LANES), F32), cos], axis=0)
    sin = jnp.concatenate([jnp.zeros((n_ctx, LANES), F32), sin], axis=0)
    return cos, sin


def _even_weight(w):
    nq, nkv = HQ_A * DH_A, HKV_A * DH_A
    q = w[:, :nq]
    dup = lambda m: jnp.concatenate(
        [m[:, DH_A * (h // 2):DH_A * (h // 2 + 1)] for h in range(2 * HKV_A)], axis=1)
    k = dup(w[:, nq:nq + nkv])
    v = dup(w[:, nq + nkv:nq + 2 * nkv])
    o = nq + 2 * nkv
    nconv, nz = 3 * H_B * DK_B, H_B * DK_B
    conv = w[:, o:o + nconv]
    z = w[:, o + nconv:o + nconv + nz]
    ab = w[:, o + nconv + nz:]
    ab = jnp.pad(ab, ((0, 0), (0, LANES - ab.shape[1])))
    return jnp.concatenate([q, k, v, conv, z, ab], axis=1).astype(BF16)


def _lane_vec(v):
    v = v.reshape(1, -1)
    return jnp.pad(v, ((0, 0), (0, LANES - v.shape[1])))


def kernel(x, c, ctx, c_ctx, w_mod, b_mod, norm_mix, norm_ffn, w_in_ab, w_out_ab, qnorm_a, knorm_a,
           sink_a, conv_b, a_log_b, dt_bias_b, onorm_b, w_in_c, w_out_c, qnorm_c, knorm_c,
           w_router, w_gate, w_up, w_down):
    b, s_len, d = x.shape
    n_ctx = ctx.shape[1]
    depth = w_mod.shape[0]
    assert b + 1 <= SUBLANES and n_ctx % ROW_TILE == 0 and s_len % ROW_TILE == 0
    t_all = n_ctx + s_len

    cvec = jnp.concatenate([c_ctx[None], c, jnp.zeros((SUBLANES - 1 - b, d), F32)], axis=0)
    mod = _modulation(cvec, w_mod, b_mod).reshape(depth, SUBLANES, 6, d)
    cos_a, sin_a = _rope_tables(s_len, n_ctx, DH_A)
    cos_c, sin_c = _rope_tables(s_len, n_ctx, DH_C)
    nq_a = HQ_A * DH_A
    seg = jnp.arange(nq_a) // DH_A
    ones_bd = (seg[:, None] == seg[None, :]).astype(BF16)

    xs = jnp.concatenate([ctx, x], axis=1)
    for i in range(depth):
        last = i == depth - 1
        j = i // 2
        gain1 = norm_mix[i].reshape(1, d)
        gain2 = norm_ffn[i].reshape(1, d)
        w_r = jnp.pad(w_router[i], ((0, 0), (0, LANES - N_EXPERTS)))
        if i % 2 == 0:
            q, k, v, pc, z, ab = _inproj_even(
                xs, mod[i], gain1, _even_weight(w_in_ab[j]), cos_a, sin_a,
                jnp.tile(qnorm_a[j], HQ_A).reshape(1, -1), jnp.tile(knorm_a[j], 2 * HKV_A).reshape(1, -1),
                ones_bd, n_ctx)
            oa = _attn_a(sink_a[j], q, k, v, n_ctx)
            qb, kb, vb, gb = _gdn_prep(pc, conv_b[j], ab, _lane_vec(a_log_b[j]), _lane_vec(dt_bias_b[j]),
                                       n_ctx)
            u, w, qg, a, kdt, ge = _gdn_chunk(qb, kb, vb, gb)
            o_dir = _gdn_scan(u, w, qg, a, kdt, ge, n_ctx)
            tile0 = n_ctx // ROW_TILE if last else 0
            xn, h2, aff = _outproj_call(
                _outproj_even_kernel, "outproj_even",
                [(oa, None, tile0), (o_dir, 0, tile0), (o_dir, 1, tile0), (z, None, tile0)], xs, mod[i],
                [w_out_ab[j].astype(BF16), onorm_b[j].reshape(1, -1), gain2, w_r], n_ctx, tile0)
        else:
            q, k, v = _inproj_odd(xs, mod[i], gain1, w_in_c[j].astype(BF16), cos_c, sin_c,
                                  qnorm_c[j].reshape(1, -1), knorm_c[j].reshape(1, -1), n_ctx)
            if last:
                o = _attn_c(q, k, v, n_ctx)
                tile0 = n_ctx // ROW_TILE
            else:
                raise NotImplementedError("context queries of a non-final odd layer")
            xn, h2, aff = _outproj_call(
                _outproj_odd_kernel, "outproj_odd", [(o, None, 0)], xs, mod[i],
                [w_out_c[j].astype(BF16), gain2, w_r], n_ctx, tile0)
        wg, wu, wd = w_gate[i].astype(BF16), w_up[i].astype(BF16), w_down[i].astype(BF16)
        aff = aff[..., :N_EXPERTS]
        if last:
            (moe_l,) = _moe([(h2, aff)], wg, wu, wd)
            gate2 = mod[i][1:1 + b, 5][:, None, :]
            return xn + gate2 * moe_l
        moe_c, moe_l = _moe([(h2[:, :n_ctx], aff[:, :n_ctx]), (h2[:, n_ctx:], aff[:, n_ctx:])], wg, wu, wd)
        gate2_l = mod[i][1:1 + b, 5][:, None, :]
        gate2_c = mod[i][0, 5][None, None, :]
        xs = xn + jnp.concatenate([gate2_c * moe_c, gate2_l * moe_l], axis=1)
```

```python
import functools

import jax
import jax.numpy as jnp
from jax import lax
from jax.experimental import pallas as pl
from jax.experimental.pallas import tpu as pltpu

F32 = jnp.float32
BF16 = jnp.bfloat16
HIGHEST = lax.Precision.HIGHEST

GRID_W = 64
NORM_EPS = 1e-6
ROPE_THETA = 10000.0
NEG_INF = -1e30
HQ_A, HKV_A, DH_A, WINDOW = 8, 2, 64, 128
H_B, DK_B, CONV_K, CHUNK = 4, 128, 5, 64
HQ_C, HKV_C, DH_C = 8, 2, 128
N_EXPERTS, CAP_FACTOR = 16, 2

LANES = 128
SUBLANES = 8
VMEM_LIMIT = 56 * 2 ** 20

ROW_TILE = 256
Q_TILE = 128
KV_TILE = 768
SOFTMAX_ROWS = 64
LOG2_E = 1.4426950408889634
FF_TILE = 512


def _params(*sem):
    return pltpu.CompilerParams(dimension_semantics=sem, vmem_limit_bytes=VMEM_LIMIT)


def _silu(x):
    return x * (1.0 / (1.0 + jnp.exp(-x)))


def _sigmoid(x):
    return 1.0 / (1.0 + jnp.exp(-x))


def _norm_mod(x, gain, shift, scale):
    ms = jnp.mean(x * x, axis=-1, keepdims=True)
    y = x * lax.rsqrt(ms + NORM_EPS) * gain
    return y * (1.0 + scale) + shift


def _segment_mean_square(p, ones_bd, seg):
    sq = p * p
    hi = sq.astype(BF16)
    lo = (sq - hi.astype(F32)).astype(BF16)
    s = (jnp.dot(hi, ones_bd, preferred_element_type=F32)
         + jnp.dot(lo, ones_bd, preferred_element_type=F32))
    return s * (1.0 / seg)


def _rope(x, cos, sin_signed, dist):
    n = x.shape[-1]
    lane = lax.broadcasted_iota(jnp.int32, x.shape, 1)
    up = pltpu.roll(x, n - dist, 1)
    dn = pltpu.roll(x, dist, 1)
    partner = jnp.where((lane & dist) == 0, up, dn)
    return x * cos + partner * sin_signed


def _mod_kernel(c_ref, w_ref, b_ref, o_ref):
    s = _silu(c_ref[...])
    o_ref[0] = jnp.dot(s, w_ref[0], precision=HIGHEST, preferred_element_type=F32) + b_ref[0]


def _modulation(cvec, w_mod, b_mod):
    depth, d, n6 = w_mod.shape
    tn = 1536
    return pl.pallas_call(
        _mod_kernel,
        grid=(depth, n6 // tn),
        in_specs=[pl.BlockSpec((SUBLANES, d), lambda l, j: (0, 0)),
                  pl.BlockSpec((1, d, tn), lambda l, j: (l, 0, j)),
                  pl.BlockSpec((1, 1, tn), lambda l, j: (l, 0, j))],
        out_specs=pl.BlockSpec((1, SUBLANES, tn), lambda l, j: (l, 0, j)),
        out_shape=jax.ShapeDtypeStruct((depth, SUBLANES, n6), F32),
        compiler_params=_params("parallel", "parallel"),
        name="modulation",
    )(cvec, w_mod, b_mod.reshape(depth, 1, n6))


def _mod_spec(n_ctx_tiles, tile0=0):
    return lambda b, i: (jnp.where(i + tile0 < n_ctx_tiles, 0, 1 + b), 0, 0)


def _inproj_even_kernel(x_ref, mod_ref, gain_ref, w_ref, cos_ref, sin_ref, qn_ref, kn_ref, ones_ref,
                        q_ref, k_ref, v_ref, pc_ref, z_ref, ab_ref):
    h = _norm_mod(x_ref[0], gain_ref[...], mod_ref[0, 0:1, :], mod_ref[0, 1:2, :]).astype(BF16)

    def proj(lo, hi):
        return jnp.dot(h, w_ref[:, lo:hi], preferred_element_type=F32)

    c = cos_ref[...]
    s = sin_ref[...]
    nq, nk = HQ_A * DH_A, 2 * HKV_A * DH_A
    q = proj(0, nq)
    q = q * lax.rsqrt(_segment_mean_square(q, ones_ref[...], DH_A) + NORM_EPS) * qn_ref[...]
    q = _rope(q, jnp.concatenate([c] * (nq // LANES), axis=1),
              jnp.concatenate([s] * (nq // LANES), axis=1), DH_A // 4)
    q_ref[0] = (q * DH_A ** -0.5).astype(BF16)
    k = proj(nq, nq + nk)
    k = k * lax.rsqrt(_segment_mean_square(k, ones_ref[0:nk, 0:nk], DH_A) + NORM_EPS) * kn_ref[...]
    k = _rope(k, jnp.concatenate([c] * (nk // LANES), axis=1),
              jnp.concatenate([s] * (nk // LANES), axis=1), DH_A // 4)
    k_ref[0] = k.astype(BF16)
    o = nq + nk
    v_ref[0] = proj(o, o + nk).astype(BF16)
    o += nk
    nconv = pc_ref.shape[2]
    pc_ref[0] = proj(o, o + nconv)
    o += nconv
    nz = z_ref.shape[2]
    z_ref[0] = proj(o, o + nz)
    o += nz
    ab_ref[0] = proj(o, o + LANES)


def _inproj_even(xs, mod, gain, w, cos, sin, qn, kn, ones_bd, n_ctx):
    b, t, d = xs.shape
    tm = ROW_TILE
    nq, nk = HQ_A * DH_A, 2 * HKV_A * DH_A
    nconv, nz = 3 * H_B * DK_B, H_B * DK_B
    row = lambda w_: pl.BlockSpec((1, tm, w_), lambda bb, i: (bb, i, 0))
    const = lambda a: pl.BlockSpec(a.shape, lambda bb, i: (0,) * a.ndim)
    return pl.pallas_call(
        _inproj_even_kernel,
        grid=(b, t // tm),
        in_specs=[row(d), pl.BlockSpec((1, 6, d), _mod_spec(n_ctx // tm)), const(gain), const(w),
                  pl.BlockSpec((tm, LANES), lambda bb, i: (i, 0)),
                  pl.BlockSpec((tm, LANES), lambda bb, i: (i, 0)),
                  const(qn), const(kn), const(ones_bd)],
        out_specs=[row(nq), row(nk), row(nk), row(nconv), row(nz), row(LANES)],
        out_shape=[jax.ShapeDtypeStruct((b, t, nq), BF16), jax.ShapeDtypeStruct((b, t, nk), BF16),
                   jax.ShapeDtypeStruct((b, t, nk), BF16), jax.ShapeDtypeStruct((b, t, nconv), F32),
                   jax.ShapeDtypeStruct((b, t, nz), F32), jax.ShapeDtypeStruct((b, t, LANES), F32)],
        compiler_params=_params("parallel", "parallel"),
        name="inproj_even",
    )(xs, mod, gain, w, cos, sin, qn, kn, ones_bd)


def _attn_a_kernel(sink_ref, q_ref, k_ref, v_ref, o_ref, *, n_ctx, t_all):
    i = pl.program_id(1)
    g_heads = HQ_A // HKV_A
    band = 3 * Q_TILE
    n = i - n_ctx // Q_TILE
    start = jnp.clip(n_ctx + (n - 1) * Q_TILE, 0, t_all - band)
    start = pl.multiple_of(start, Q_TILE)
    q = q_ref[0]
    rows = g_heads * Q_TILE
    lane = lax.broadcasted_iota(jnp.int32, (Q_TILE, LANES), 1)
    qpos = n * Q_TILE + (lax.broadcasted_iota(jnp.int32, (rows, band), 0) & (Q_TILE - 1))
    kpos = (start - n_ctx) + lax.broadcasted_iota(jnp.int32, (rows, band), 1)
    valid = (n >= 0) & (jnp.abs(kpos - qpos) <= WINDOW) & (kpos >= 0)
    dims = (((1,), (1,)), ((), ()))
    for h in range(HKV_A):
        cols = slice(LANES * h, LANES * (h + 1))
        parts, sinks = [], []
        for g in range(g_heads):
            j = g_heads * h + g
            tile = q[:, LANES * (j // 2):LANES * (j // 2 + 1)]
            keep = (lane >= DH_A * (j % 2)) & (lane < DH_A * (j % 2 + 1))
            parts.append(jnp.where(keep, tile, jnp.zeros_like(tile)))
            sinks.append(jnp.full((Q_TILE, 1), sink_ref[j], F32))
        qs = jnp.concatenate(parts, axis=0)
        sk = jnp.concatenate(sinks, axis=0)
        kc = k_ref[0, 0:n_ctx, cols]
        vc = v_ref[0, 0:n_ctx, cols]
        kb = k_ref[0, pl.ds(start, band), cols]
        vb = v_ref[0, pl.ds(start, band), cols]
        s_c = lax.dot_general(qs, kc, dims, preferred_element_type=F32)
        s_b = lax.dot_general(qs, kb, dims, preferred_element_type=F32)
        s_b = jnp.where(valid, s_b, NEG_INF)
        m = jnp.maximum(jnp.maximum(s_c.max(axis=1, keepdims=True), s_b.max(axis=1, keepdims=True)), sk)
        p_c = jnp.exp(s_c - m)
        p_b = jnp.exp(s_b - m)
        den = p_c.sum(axis=1, keepdims=True) + p_b.sum(axis=1, keepdims=True) + jnp.exp(sk - m)
        o = (jnp.dot(p_c.astype(BF16), vc, preferred_element_type=F32)
             + jnp.dot(p_b.astype(BF16), vb, preferred_element_type=F32)) / den
        for pair in range(g_heads // 2):
            lo = o[(2 * pair) * Q_TILE:(2 * pair + 1) * Q_TILE]
            hi = o[(2 * pair + 1) * Q_TILE:(2 * pair + 2) * Q_TILE]
            c0 = LANES * (g_heads // 2 * h + pair)
            o_ref[0, :, c0:c0 + LANES] = jnp.where(lane < DH_A, lo, hi).astype(BF16)


def _attn_a(sink, q, k, v, n_ctx):
    b, t, nq = q.shape
    nk = k.shape[2]
    return pl.pallas_call(
        functools.partial(_attn_a_kernel, n_ctx=n_ctx, t_all=t),
        grid=(b, t // Q_TILE),
        in_specs=[pl.BlockSpec(memory_space=pltpu.SMEM),
                  pl.BlockSpec((1, Q_TILE, nq), lambda bb, i: (bb, i, 0)),
                  pl.BlockSpec((1, t, nk), lambda bb, i: (bb, 0, 0)),
                  pl.BlockSpec((1, t, nk), lambda bb, i: (bb, 0, 0))],
        out_specs=pl.BlockSpec((1, Q_TILE, nq), lambda bb, i: (bb, i, 0)),
        out_shape=jax.ShapeDtypeStruct((b, t, nq), BF16),
        compiler_params=_params("parallel", "arbitrary"),
        name="attn_window",
    )(sink, q, k, v)


def _gdn_prep_kernel(pc_ref, prev_ref, next_ref, cw_ref, ab_ref, alog_ref, dtb_ref,
                     q_ref, k_ref, v_ref, gb_ref, ext_sc, *, n_ctx, t_all):
    tm = pc_ref.shape[1]
    r0 = pl.program_id(1) * tm
    halo = SUBLANES
    prev_on = jnp.where((r0 == 0) | (r0 == n_ctx), 0.0, 1.0)
    next_on = jnp.where((r0 + tm == n_ctx) | (r0 + tm == t_all), 0.0, 1.0)
    ext_sc[0:halo, :] = prev_ref[0] * prev_on
    ext_sc[halo:halo + tm, :] = pc_ref[0]
    ext_sc[halo + tm:2 * halo + tm, :] = next_ref[0] * next_on
    nh = H_B * DK_B
    for grp, out_ref in enumerate((q_ref, k_ref, v_ref)):
        c0 = nh * grp
        acc = None
        for tap in range(CONV_K):
            off = halo - CONV_K // 2 + tap
            term = cw_ref[tap:tap + 1, c0:c0 + nh] * ext_sc[off:off + tm, c0:c0 + nh]
            acc = term if acc is None else acc + term
        y = _silu(acc)
        if grp == 2:
            out_ref[0] = y
            continue
        scale = DK_B ** -0.5 if grp == 0 else 1.0
        for h in range(H_B):
            yh = y[:, DK_B * h:DK_B * (h + 1)]
            inv = lax.rsqrt(jnp.sum(yh * yh, axis=-1, keepdims=True) + NORM_EPS)
            out_ref[0, :, DK_B * h:DK_B * (h + 1)] = yh * (inv * scale)
    ab = ab_ref[0]
    lane = lax.broadcasted_iota(jnp.int32, ab.shape, 1)
    xg = ab + dtb_ref[...]
    softplus = jnp.maximum(xg, 0.0) + jnp.log(1.0 + jnp.exp(-jnp.abs(xg)))
    g = -jnp.exp(alog_ref[...]) * softplus
    gb_ref[0] = jnp.where(lane < 2 * H_B, g, jnp.where(lane < 4 * H_B, _sigmoid(ab), 0.0))


def _gdn_prep(pc, conv_w, ab, alog, dtb, n_ctx):
    b, t, nconv = pc.shape
    tm = ROW_TILE
    nh = H_B * DK_B
    hb = tm // SUBLANES
    nblk = t // SUBLANES
    row = lambda w_: pl.BlockSpec((1, tm, w_), lambda bb, i: (bb, i, 0))
    const = lambda a: pl.BlockSpec(a.shape, lambda bb, i: (0,) * a.ndim)
    return pl.pallas_call(
        functools.partial(_gdn_prep_kernel, n_ctx=n_ctx, t_all=t),
        grid=(b, t // tm),
        in_specs=[row(nconv),
                  pl.BlockSpec((1, SUBLANES, nconv), lambda bb, i: (bb, jnp.maximum(i * hb - 1, 0), 0)),
                  pl.BlockSpec((1, SUBLANES, nconv),
                               lambda bb, i: (bb, jnp.minimum((i + 1) * hb, nblk - 1), 0)),
                  const(conv_w), row(LANES), const(alog), const(dtb)],
        out_specs=[row(nh), row(nh), row(nh), row(LANES)],
        out_shape=[jax.ShapeDtypeStruct((b, t, nh), F32)] * 3 + [jax.ShapeDtypeStruct((b, t, LANES), F32)],
        scratch_shapes=[pltpu.VMEM((tm + 2 * SUBLANES, nconv), F32)],
        compiler_params=_params("parallel", "parallel"),
        name="gdn_prep",
    )(pc, pc, pc, conv_w, ab, alog, dtb)


def _gdn_chunk_kernel(q_ref, k_ref, v_ref, gb_ref, u_ref, w_ref, qg_ref, a_ref, kdt_ref, ge_ref):
    c = CHUNK
    nchunks = q_ref.shape[1] // c
    r_i = lax.broadcasted_iota(jnp.int32, (c, c), 0)
    c_i = lax.broadcasted_iota(jnp.int32, (c, c), 1)
    tri_l = (r_i >= c_i).astype(F32)
    tri_u = (r_i <= c_i).astype(F32)
    row2 = lax.broadcasted_iota(jnp.int32, (c, LANES), 0)
    col2 = lax.broadcasted_iota(jnp.int32, (c, LANES), 1)
    colm = col2 & (c - 1)
    lane8 = lax.broadcasted_iota(jnp.int32, (1, LANES), 1)
    half_of = [(col2 // c) == (h % 2) for h in range(H_B)]
    eye_stack = jnp.concatenate(
        [jnp.where(half_of[h] & (row2 == colm), 1.0, 0.0) for h in range(H_B)], axis=0)

    def bmm(ls, rs):
        rcat = jnp.concatenate([rs[0:2 * c], rs[2 * c:4 * c]], axis=1).astype(BF16)
        full = jnp.dot(ls.astype(BF16), rcat, preferred_element_type=F32)
        return jnp.concatenate([full[0:2 * c, 0:LANES], full[2 * c:4 * c, LANES:2 * LANES]], axis=0)

    def wide(t_stack, mats):
        rv = jnp.concatenate([jnp.concatenate(mats[0:2], axis=0),
                              jnp.concatenate(mats[2:4], axis=0)], axis=1).astype(BF16)
        full = jnp.dot(t_stack.astype(BF16), rv, preferred_element_type=F32)
        return [full[c * h:c * (h + 1), LANES * (h // 2):LANES * (h // 2 + 1)] for h in range(H_B)]

    dims = (((1,), (1,)), ((), ()))
    for ci in range(nchunks):
        rows = slice(c * ci, c * (ci + 1))
        gb = gb_ref[0, rows, :]
        gc = jnp.where(lane8 < H_B,
                       jnp.dot(tri_l, gb, precision=HIGHEST, preferred_element_type=F32),
                       jnp.dot(tri_u, gb, precision=HIGHEST, preferred_element_type=F32))
        gc_t = gc.T
        eg = jnp.exp(gc)
        g_last = jnp.where(lane8 < H_B, gc[c - 1:c, :], gc[0:1, :])
        ge_ref[0, ci] = jnp.broadcast_to(jnp.exp(g_last), (SUBLANES, LANES))
        ek = jnp.exp(g_last - gc)
        qs = [q_ref[0, rows, DK_B * h:DK_B * (h + 1)] for h in range(H_B)]
        ks = [k_ref[0, rows, DK_B * h:DK_B * (h + 1)] for h in range(H_B)]
        vs = [v_ref[0, rows, DK_B * h:DK_B * (h + 1)] for h in range(H_B)]
        raw = []
        for h in range(H_B):
            kq = jnp.concatenate([ks[h], qs[h]], axis=0).astype(BF16)
            kk = jnp.concatenate([ks[h], ks[h]], axis=0).astype(BF16)
            raw.append(lax.dot_general(kq, kk, dims, preferred_element_type=F32))
        for d in range(2):
            keep = (row2 >= colm) if d == 0 else (row2 <= colm)
            strict = (row2 > colm) if d == 0 else (row2 < colm)
            a_blocks, aqk, betas, egs = [], [], [], []
            for h in range(H_B):
                idx = H_B * d + h
                g_col = gc[:, idx:idx + 1]
                g_row = jnp.concatenate([gc_t[idx:idx + 1, :]] * 2, axis=1)
                decay = jnp.where(keep, jnp.exp(jnp.where(keep, g_col - g_row, 0.0)), 0.0)
                beta = gb[:, 2 * H_B + idx:2 * H_B + idx + 1]
                betas.append(beta)
                egs.append(eg[:, idx:idx + 1])
                a_blocks.append(jnp.where(strict & half_of[h], beta * raw[h][0:c] * decay, 0.0))
                aqk.append(raw[h][c:2 * c] * decay)
            x = -jnp.concatenate(a_blocks, axis=0)
            t_inv = eye_stack + x
            p = x
            for _ in range(5):
                p = bmm(p, p)
                t_inv = t_inv + bmm(t_inv, p)
            us = wide(t_inv, [vs[h] * betas[h] for h in range(H_B)])
            ws = wide(t_inv, [ks[h] * (betas[h] * egs[h]) for h in range(H_B)])
            for h in range(H_B):
                idx = H_B * d + h
                cols = slice(DK_B * h, DK_B * (h + 1))
                u_ref[d, 0, rows, cols] = us[h]
                w_ref[d, 0, rows, cols] = ws[h].astype(BF16)
                qg_ref[d, 0, rows, cols] = (qs[h] * egs[h]).astype(BF16)
            for pair in range(H_B // 2):
                h0, h1 = 2 * pair, 2 * pair + 1
                a_ref[d, 0, rows, LANES * pair:LANES * (pair + 1)] = jnp.where(
                    col2 < c, aqk[h0], aqk[h1]).astype(BF16)
                kd0 = (ks[h0] * ek[:, H_B * d + h0:H_B * d + h0 + 1]).T
                kd1 = (ks[h1] * ek[:, H_B * d + h1:H_B * d + h1 + 1]).T
                kdt_ref[d, 0, ci, :, LANES * pair:LANES * (pair + 1)] = jnp.concatenate(
                    [kd0, kd1], axis=1).astype(BF16)


def _gdn_chunk(qb, kb, vb, gb):
    b, t, nh = qb.shape
    tm = ROW_TILE
    cps = tm // CHUNK
    nck = t // CHUNK
    row = lambda w_: pl.BlockSpec((1, tm, w_), lambda bb, i: (bb, i, 0))
    drow = lambda w_: pl.BlockSpec((2, 1, tm, w_), lambda bb, i: (0, bb, i, 0))
    return pl.pallas_call(
        _gdn_chunk_kernel,
        grid=(b, t // tm),
        in_specs=[row(nh), row(nh), row(nh), row(LANES)],
        out_specs=[drow(nh), drow(nh), drow(nh), drow(nh // 2),
                   pl.BlockSpec((2, 1, cps, DK_B, nh // 2), lambda bb, i: (0, bb, i, 0, 0)),
                   pl.BlockSpec((1, cps, SUBLANES, LANES), lambda bb, i: (bb, i, 0, 0))],
        out_shape=[jax.ShapeDtypeStruct((2, b, t, nh), F32), jax.ShapeDtypeStruct((2, b, t, nh), BF16),
                   jax.ShapeDtypeStruct((2, b, t, nh), BF16),
                   jax.ShapeDtypeStruct((2, b, t, nh // 2), BF16),
                   jax.ShapeDtypeStruct((2, b, nck, DK_B, nh // 2), BF16),
                   jax.ShapeDtypeStruct((b, nck, SUBLANES, LANES), F32)],
        compiler_params=_params("parallel", "parallel"),
        name="gdn_chunk",
    )(qb, kb, vb, gb)


def _gdn_scan_kernel(u_ref, w_ref, qg_ref, a_ref, kdt_ref, ge_ref, o_ref, s_sc):
    d = pl.program_id(1)

    @pl.when(pl.program_id(2) == 0)
    def _():
        s_sc[...] = jnp.zeros_like(s_sc)

    ge = ge_ref[0, 0]
    zero = jnp.zeros((CHUNK, DK_B), BF16)
    for h in range(H_B):
        cols = slice(DK_B * h, DK_B * (h + 1))
        pair = slice(LANES * (h // 2), LANES * (h // 2 + 1))
        s = s_sc[h]
        wq = jnp.concatenate([w_ref[0, 0, :, cols], qg_ref[0, 0, :, cols]], axis=0)
        r = jnp.dot(wq, s.astype(BF16), preferred_element_type=F32)
        v_new = (u_ref[0, 0, :, cols] - r[0:CHUNK]).astype(BF16)
        v_pad = jnp.concatenate([v_new, zero] if h % 2 == 0 else [zero, v_new], axis=0)
        o_ref[0, 0, :, cols] = r[CHUNK:2 * CHUNK] + jnp.dot(
            a_ref[0, 0, :, pair], v_pad, preferred_element_type=F32)
        g_end = jnp.where(d == 0, ge[0:1, h:h + 1], ge[0:1, H_B + h:H_B + h + 1])
        s_sc[h] = s * g_end + jnp.dot(kdt_ref[0, 0, 0, :, pair], v_pad, preferred_element_type=F32)


def _gdn_scan(u, w, qg, a, kdt, ge, n_ctx):
    _, b, t, nh = u.shape
    nck = t // CHUNK
    ncc = n_ctx // CHUNK

    def chunk(d, s):
        back = jnp.where(s < ncc, ncc - 1 - s, nck - 1 - (s - ncc))
        return jnp.where(d == 0, s, back)

    drow = lambda w_: pl.BlockSpec((1, 1, CHUNK, w_), lambda bb, d, s: (d, bb, chunk(d, s), 0))
    return pl.pallas_call(
        _gdn_scan_kernel,
        grid=(b, 2, nck),
        in_specs=[drow(nh), drow(nh), drow(nh), drow(nh // 2),
                  pl.BlockSpec((1, 1, 1, DK_B, nh // 2), lambda bb, d, s: (d, bb, chunk(d, s), 0, 0)),
                  pl.BlockSpec((1, 1, SUBLANES, LANES), lambda bb, d, s: (bb, chunk(d, s), 0, 0))],
        out_specs=drow(nh),
        out_shape=jax.ShapeDtypeStruct((2, b, t, nh), F32),
        scratch_shapes=[pltpu.VMEM((H_B, DK_B, DK_B), F32)],
        compiler_params=_params("parallel", "parallel", "arbitrary"),
        name="gdn_scan",
    )(u, w, qg, a, kdt, ge)


def _residual_router(y, x_ref, mod_ref, g2_ref, wr_ref, xn_ref, h2_ref, aff_ref):
    xn = x_ref[0] + mod_ref[0, 2:3, :] * y
    xn_ref[0] = xn
    h2 = _norm_mod(xn, g2_ref[...], mod_ref[0, 3:4, :], mod_ref[0, 4:5, :])
    h2_ref[0] = h2
    logits = jnp.dot(h2, wr_ref[...], precision=HIGHEST, preferred_element_type=F32)
    lane = lax.broadcasted_iota(jnp.int32, logits.shape, 1)
    logits = jnp.where(lane < N_EXPERTS, logits, NEG_INF)
    e = jnp.exp(logits - logits.max(axis=-1, keepdims=True))
    aff_ref[0] = e / e.sum(axis=-1, keepdims=True)


def _outproj_even_kernel(oa_ref, of_ref, ob_ref, z_ref, x_ref, mod_ref, w_ref, onorm_ref, g2_ref, wr_ref,
                         xn_ref, h2_ref, aff_ref):
    na = oa_ref.shape[2]
    y = jnp.dot(oa_ref[0], w_ref[0:na, :], preferred_element_type=F32)
    o = of_ref[0, 0] + ob_ref[0, 0]
    z = z_ref[0]
    for h in range(H_B):
        cols = slice(DK_B * h, DK_B * (h + 1))
        oh = o[:, cols]
        ms = jnp.mean(oh * oh, axis=-1, keepdims=True)
        yh = oh * lax.rsqrt(ms + NORM_EPS) * onorm_ref[...] * _silu(z[:, cols])
        y = y + jnp.dot(yh.astype(BF16), w_ref[na + DK_B * h:na + DK_B * (h + 1), :],
                        preferred_element_type=F32)
    _residual_router(y, x_ref, mod_ref, g2_ref, wr_ref, xn_ref, h2_ref, aff_ref)


def _outproj_odd_kernel(o_ref, x_ref, mod_ref, w_ref, g2_ref, wr_ref, xn_ref, h2_ref, aff_ref):
    y = jnp.dot(o_ref[0], w_ref[...], preferred_element_type=F32)
    _residual_router(y, x_ref, mod_ref, g2_ref, wr_ref, xn_ref, h2_ref, aff_ref)


def _outproj_call(kernel, name, acts, xs, mod, consts, n_ctx, tile0):
    b, t, d = xs.shape
    tm = ROW_TILE
    nt = t // tm - tile0
    specs = []
    for a, lead, off in acts:
        if lead is None:
            specs.append(pl.BlockSpec((1, tm, a.shape[-1]), lambda bb, i, off=off: (bb, i + off, 0)))
        else:
            specs.append(pl.BlockSpec((1, 1, tm, a.shape[-1]),
                                      lambda bb, i, lead=lead, off=off: (lead, bb, i + off, 0)))
    row = pl.BlockSpec((1, tm, d), lambda bb, i: (bb, i + tile0, 0))
    const = lambda a: pl.BlockSpec(a.shape, lambda bb, i: (0,) * a.ndim)
    mod_spec = pl.BlockSpec((1, 6, d), _mod_spec(n_ctx // tm, tile0))
    wout, rest = consts[0], consts[1:]
    return pl.pallas_call(
        kernel,
        grid=(b, nt),
        in_specs=specs + [row, mod_spec, const(wout)] + [const(a) for a in rest],
        out_specs=[pl.BlockSpec((1, tm, d), lambda bb, i: (bb, i, 0)),
                   pl.BlockSpec((1, tm, d), lambda bb, i: (bb, i, 0)),
                   pl.BlockSpec((1, tm, LANES), lambda bb, i: (bb, i, 0))],
        out_shape=[jax.ShapeDtypeStruct((b, nt * tm, d), F32), jax.ShapeDtypeStruct((b, nt * tm, d), F32),
                   jax.ShapeDtypeStruct((b, nt * tm, LANES), F32)],
        compiler_params=_params("parallel", "parallel"),
        name=name,
    )(*[a for a, _, _ in acts], xs, mod, wout, *rest)


def _inproj_odd_kernel(x_ref, mod_ref, gain_ref, w_ref, cos_ref, sin_ref, qn_ref, kn_ref,
                       q_ref, k_ref, v_ref):
    h = _norm_mod(x_ref[0], gain_ref[...], mod_ref[0, 0:1, :], mod_ref[0, 1:2, :]).astype(BF16)
    c = cos_ref[...]
    s = sin_ref[...]
    nq, nk = HQ_C * DH_C, HKV_C * DH_C

    def normed_heads(lo, nheads, gain_ref_, out_ref, scale):
        p = jnp.dot(h, w_ref[:, lo:lo + nheads * DH_C], preferred_element_type=F32)
        for hh in range(nheads):
            ph = p[:, DH_C * hh:DH_C * (hh + 1)]
            ms = jnp.mean(ph * ph, axis=-1, keepdims=True)
            ph = _rope(ph * lax.rsqrt(ms + NORM_EPS) * gain_ref_[...], c, s, DH_C // 4)
            out_ref[0, :, DH_C * hh:DH_C * (hh + 1)] = (ph * scale).astype(BF16)

    normed_heads(0, HQ_C, qn_ref, q_ref, DH_C ** -0.5 * LOG2_E)
    normed_heads(nq, HKV_C, kn_ref, k_ref, 1.0)
    v_ref[0] = jnp.dot(h, w_ref[:, nq + nk:nq + 2 * nk], preferred_element_type=F32).astype(BF16)


def _inproj_odd(xs, mod, gain, w, cos, sin, qn, kn, n_ctx):
    b, t, d = xs.shape
    tm = ROW_TILE
    nq, nk = HQ_C * DH_C, HKV_C * DH_C
    row = lambda w_: pl.BlockSpec((1, tm, w_), lambda bb, i: (bb, i, 0))
    const = lambda a: pl.BlockSpec(a.shape, lambda bb, i: (0,) * a.ndim)
    return pl.pallas_call(
        _inproj_odd_kernel,
        grid=(b, t // tm),
        in_specs=[row(d), pl.BlockSpec((1, 6, d), _mod_spec(n_ctx // tm)), const(gain), const(w),
                  pl.BlockSpec((tm, LANES), lambda bb, i: (i, 0)),
                  pl.BlockSpec((tm, LANES), lambda bb, i: (i, 0)),
                  const(qn), const(kn)],
        out_specs=[row(nq), row(nk), row(nk)],
        out_shape=[jax.ShapeDtypeStruct((b, t, nq), BF16), jax.ShapeDtypeStruct((b, t, nk), BF16),
                   jax.ShapeDtypeStruct((b, t, nk), BF16)],
        compiler_params=_params("parallel", "parallel"),
        name="inproj_odd",
    )(xs, mod, gain, w, cos, sin, qn, kn)


def _attn_c_kernel(q_ref, k_ref, v_ref, o_ref, s_sc, p_sc, m_sc, l_sc, a_sc, acc_sc):
    nchunk = k_ref.shape[1] // KV_TILE
    g_heads = HQ_C // HKV_C
    ntile = KV_TILE // LANES
    nrows = g_heads * Q_TILE
    dims = (((1,), (1,)), ((), ()))
    qs = [jnp.concatenate([q_ref[0, :, DH_C * (g_heads * h + g):DH_C * (g_heads * h + g + 1)]
                           for g in range(g_heads)], axis=0) for h in range(HKV_C)]
    m_sc[...] = jnp.full(m_sc.shape, NEG_INF, F32)
    l_sc[...] = jnp.zeros(l_sc.shape, F32)
    acc_sc[...] = jnp.zeros(acc_sc.shape, F32)

    def body(ci, carry):
        r0 = pl.multiple_of(ci * KV_TILE, KV_TILE)
        for h in range(HKV_C):
            s_sc[h] = lax.dot_general(qs[h], k_ref[0, pl.ds(r0, KV_TILE), DH_C * h:DH_C * (h + 1)],
                                      dims, preferred_element_type=F32)
        for h in range(HKV_C):
            for rb in range(nrows // SOFTMAX_ROWS):
                rows = slice(SOFTMAX_ROWS * rb, SOFTMAX_ROWS * (rb + 1))
                mx = s_sc[h, rows, 0:LANES]
                for t in range(1, ntile):
                    mx = jnp.maximum(mx, s_sc[h, rows, LANES * t:LANES * (t + 1)])
                m_old = m_sc[h, rows, :]
                m_new = jnp.maximum(m_old, jnp.broadcast_to(mx.max(axis=1, keepdims=True),
                                                            (SOFTMAX_ROWS, LANES)))
                alpha = jnp.exp2(m_old - m_new)
                l_new = alpha * l_sc[h, rows, :]
                for t in range(ntile):
                    p = jnp.exp2(s_sc[h, rows, LANES * t:LANES * (t + 1)] - m_new)
                    l_new = l_new + p
                    p_sc[h, rows, LANES * t:LANES * (t + 1)] = p.astype(BF16)
                l_sc[h, rows, :] = l_new
                m_sc[h, rows, :] = m_new
                a_sc[h, rows, :] = alpha
        for h in range(HKV_C):
            pv = jnp.dot(p_sc[h], v_ref[0, pl.ds(r0, KV_TILE), DH_C * h:DH_C * (h + 1)],
                         preferred_element_type=F32)
            acc_sc[h] = a_sc[h] * acc_sc[h] + pv
        return carry

    lax.fori_loop(0, nchunk, body, 0)
    for j in range(HQ_C):
        h, rows = j // g_heads, slice(Q_TILE * (j % g_heads), Q_TILE * (j % g_heads + 1))
        o = acc_sc[h, rows, :] / l_sc[h, rows, :].sum(axis=1, keepdims=True)
        o_ref[0, :, DH_C * j:DH_C * (j + 1)] = o.astype(BF16)


def _attn_c(q, k, v, n_ctx):
    b, t, nq = q.shape
    nk = k.shape[2]
    t0 = n_ctx // Q_TILE
    rows = HQ_C // HKV_C * Q_TILE
    return pl.pallas_call(
        _attn_c_kernel,
        grid=(b, t // Q_TILE - t0),
        in_specs=[pl.BlockSpec((1, Q_TILE, nq), lambda bb, i: (bb, i + t0, 0)),
                  pl.BlockSpec((1, t, nk), lambda bb, i: (bb, 0, 0)),
                  pl.BlockSpec((1, t, nk), lambda bb, i: (bb, 0, 0))],
        out_specs=pl.BlockSpec((1, Q_TILE, nq), lambda bb, i: (bb, i, 0)),
        out_shape=jax.ShapeDtypeStruct((b, t - n_ctx, nq), BF16),
        scratch_shapes=[pltpu.VMEM((HKV_C, rows, KV_TILE), F32), pltpu.VMEM((HKV_C, rows, KV_TILE), BF16),
                        pltpu.VMEM((HKV_C, rows, LANES), F32), pltpu.VMEM((HKV_C, rows, LANES), F32),
                        pltpu.VMEM((HKV_C, rows, LANES), F32), pltpu.VMEM((HKV_C, rows, DH_C), F32)],
        compiler_params=_params("parallel", "arbitrary"),
        name="attn_global",
    )(q, k, v)


def _expert_ffn_kernel(x_ref, wg_ref, wu_ref, wd_ref, y_ref):
    x = x_ref[0].astype(BF16)
    ff = wg_ref.shape[2]
    acc = None
    for f0 in range(0, ff, FF_TILE):
        g = jnp.dot(x, wg_ref[0, :, f0:f0 + FF_TILE], preferred_element_type=F32)
        u = jnp.dot(x, wu_ref[0, :, f0:f0 + FF_TILE], preferred_element_type=F32)
        hid = (_silu(g) * u).astype(BF16)
        part = jnp.dot(hid, wd_ref[0, f0:f0 + FF_TILE, :], preferred_element_type=F32)
        acc = part if acc is None else acc + part
    y_ref[0] = acc


def _expert_ffn(xs, wg, wu, wd):
    e, r, d = xs.shape
    ff = wg.shape[2]
    nr = 2 if r % (2 * SUBLANES) == 0 else 1
    tr = r // nr
    return pl.pallas_call(
        _expert_ffn_kernel,
        grid=(e, nr),
        in_specs=[pl.BlockSpec((1, tr, d), lambda ee, i: (ee, i, 0)),
                  pl.BlockSpec((1, d, ff), lambda ee, i: (ee, 0, 0)),
                  pl.BlockSpec((1, d, ff), lambda ee, i: (ee, 0, 0)),
                  pl.BlockSpec((1, ff, d), lambda ee, i: (ee, 0, 0))],
        out_specs=pl.BlockSpec((1, tr, d), lambda ee, i: (ee, i, 0)),
        out_shape=jax.ShapeDtypeStruct((e, r, d), F32),
        compiler_params=_params("parallel", "arbitrary"),
        name="expert_ffn",
    )(xs, wg, wu, wd)


def _route(aff, cap):
    gate, idx = lax.top_k(jnp.swapaxes(aff, 1, 2), cap)
    return gate, idx


def _moe(groups, wg, wu, wd):
    e = wg.shape[0]
    gathered, meta = [], []
    for h2, aff in groups:
        b, n, d = h2.shape
        cap = max(1, CAP_FACTOR * n // e)
        gate, idx = _route(aff, cap)
        xs = jax.vmap(lambda hb, ib: hb[ib])(h2, idx)
        gathered.append(jnp.swapaxes(xs, 0, 1).reshape(e, b * cap, d))
        meta.append((b, n, d, cap, gate, idx))
    ys = _expert_ffn(jnp.concatenate(gathered, axis=1), wg, wu, wd)
    outs, r0 = [], 0
    for b, n, d, cap, gate, idx in meta:
        ye = jnp.swapaxes(ys[:, r0:r0 + b * cap].reshape(e, b, cap, d), 0, 1) * gate[..., None]
        r0 += b * cap
        outs.append(jax.vmap(lambda ib, yb: jnp.zeros((n, d), F32).at[ib.reshape(-1)].add(
            yb.reshape(-1, d)))(idx, ye))
    return outs


def _rope_tables(s_len, n_ctx, head_dim):
    quarter = head_dim // 4
    t = jnp.arange(s_len)
    row = (t // GRID_W).astype(F32)
    col = (t % GRID_W).astype(F32)
    inv = ROPE_THETA ** (-jnp.arange(quarter, dtype=F32) / quarter)
    ar, ac = row[:, None] * inv, col[:, None] * inv
    cos = jnp.concatenate([jnp.cos(ar), jnp.cos(ar), jnp.cos(ac), jnp.cos(ac)], axis=1)
    sin = jnp.concatenate([-jnp.sin(ar), jnp.sin(ar), -jnp.sin(ac), jnp.sin(ac)], axis=1)
    reps = LANES // head_dim
    cos, sin = jnp.tile(cos, (1, reps)), jnp.tile(sin, (1, reps))
    cos = jnp.concatenate([jnp.ones((n_ctx, LANES), F32), cos], axis=0)
    sin = jnp.concatenate([jnp.zeros((n_ctx, LANES), F32), sin], axis=0)
    return cos, sin


def _even_weight(w):
    nq, nkv = HQ_A * DH_A, HKV_A * DH_A
    q = w[:, :nq]
    dup = lambda m: jnp.concatenate(
        [m[:, DH_A * (h // 2):DH_A * (h // 2 + 1)] for h in range(2 * HKV_A)], axis=1)
    k = dup(w[:, nq:nq + nkv])
    v = dup(w[:, nq + nkv:nq + 2 * nkv])
    o = nq + 2 * nkv
    nconv, nz = 3 * H_B * DK_B, H_B * DK_B
    conv = w[:, o:o + nconv]
    z = w[:, o + nconv:o + nconv + nz]
    ab = w[:, o + nconv + nz:]
    ab = jnp.pad(ab, ((0, 0), (0, LANES - ab.shape[1])))
    return jnp.concatenate([q, k, v, conv, z, ab], axis=1).astype(BF16)


def _lane_vec(v):
    v = v.reshape(1, -1)
    return jnp.pad(v, ((0, 0), (0, LANES - v.shape[1])))


def kernel(x, c, ctx, c_ctx, w_mod, b_mod, norm_mix, norm_ffn, w_in_ab, w_out_ab, qnorm_a, knorm_a,
           sink_a, conv_b, a_log_b, dt_bias_b, onorm_b, w_in_c, w_out_c, qnorm_c, knorm_c,
           w_router, w_gate, w_up, w_down):
    b, s_len, d = x.shape
    n_ctx = ctx.shape[1]
    depth = w_mod.shape[0]
    assert b + 1 <= SUBLANES and n_ctx % ROW_TILE == 0 and s_len % ROW_TILE == 0
    t_all = n_ctx + s_len

    cvec = jnp.concatenate([c_ctx[None], c, jnp.zeros((SUBLANES - 1 - b, d), F32)], axis=0)
    mod = _modulation(cvec, w_mod, b_mod).reshape(depth, SUBLANES, 6, d)
    cos_a, sin_a = _rope_tables(s_len, n_ctx, DH_A)
    cos_c, sin_c = _rope_tables(s_len, n_ctx, DH_C)
    nq_a = HQ_A * DH_A
    seg = jnp.arange(nq_a) // DH_A
    ones_bd = (seg[:, None] == seg[None, :]).astype(BF16)

    xs = jnp.concatenate([ctx, x], axis=1)
    for i in range(depth):
        last = i == depth - 1
        j = i // 2
        gain1 = norm_mix[i].reshape(1, d)
        gain2 = norm_ffn[i].reshape(1, d)
        w_r = jnp.pad(w_router[i], ((0, 0), (0, LANES - N_EXPERTS)))
        if i % 2 == 0:
            q, k, v, pc, z, ab = _inproj_even(
                xs, mod[i], gain1, _even_weight(w_in_ab[j]), cos_a, sin_a,
                jnp.tile(qnorm_a[j], HQ_A).reshape(1, -1), jnp.tile(knorm_a[j], 2 * HKV_A).reshape(1, -1),
                ones_bd, n_ctx)
            oa = _attn_a(sink_a[j], q, k, v, n_ctx)
            qb, kb, vb, gb = _gdn_prep(pc, conv_b[j], ab, _lane_vec(a_log_b[j]), _lane_vec(dt_bias_b[j]),
                                       n_ctx)
            u, w, qg, a, kdt, ge = _gdn_chunk(qb, kb, vb, gb)
            o_dir = _gdn_scan(u, w, qg, a, kdt, ge, n_ctx)
            tile0 = n_ctx // ROW_TILE if last else 0
            xn, h2, aff = _outproj_call(
                _outproj_even_kernel, "outproj_even",
                [(oa, None, tile0), (o_dir, 0, tile0), (o_dir, 1, tile0), (z, None, tile0)], xs, mod[i],
                [w_out_ab[j].astype(BF16), onorm_b[j].reshape(1, -1), gain2, w_r], n_ctx, tile0)
        else:
            q, k, v = _inproj_odd(xs, mod[i], gain1, w_in_c[j].astype(BF16), cos_c, sin_c,
                                  qnorm_c[j].reshape(1, -1), knorm_c[j].reshape(1, -1), n_ctx)
            if last:
                o = _attn_c(q, k, v, n_ctx)
                tile0 = n_ctx // ROW_TILE
            else:
                raise NotImplementedError("context queries of a non-final odd layer")
            xn, h2, aff = _outproj_call(
                _outproj_odd_kernel, "outproj_odd", [(o, None, 0)], xs, mod[i],
                [w_out_c[j].astype(BF16), gain2, w_r], n_ctx, tile0)
        wg, wu, wd = w_gate[i].astype(BF16), w_up[i].astype(BF16), w_down[i].astype(BF16)
        aff = aff[..., :N_EXPERTS]
        if last:
            (moe_l,) = _moe([(h2, aff)], wg, wu, wd)
            gate2 = mod[i][1:1 + b, 5][:, None, :]
            return xn + gate2 * moe_l
        moe_c, moe_l = _moe([(h2[:, :n_ctx], aff[:, :n_ctx]), (h2[:, n_ctx:], aff[:, n_ctx:])], wg, wu, wd)
        gate2_l = mod[i][1:1 + b, 5][:, None, :]
        gate2_c = mod[i][0, 5][None, None, :]
        xs = xn + jnp.concatenate([gate2_c * moe_c, gate2_l * moe_l], axis=1)
```

```python
import functools

import jax
import jax.numpy as jnp
from jax import lax
from jax.experimental import pallas as pl
from jax.experimental.pallas import tpu as pltpu

F32 = jnp.float32
BF16 = jnp.bfloat16
HIGHEST = lax.Precision.HIGHEST

GRID_W = 64
NORM_EPS = 1e-6
ROPE_THETA = 10000.0
NEG_INF = -1e30
HQ_A, HKV_A, DH_A, WINDOW = 8, 2, 64, 128
H_B, DK_B, CONV_K, CHUNK = 4, 128, 5, 64
INV_BASE = 8
HQ_C, HKV_C, DH_C = 8, 2, 128
N_EXPERTS, CAP_FACTOR = 16, 2

LANES = 128
SUBLANES = 8
VMEM_LIMIT = 56 * 2 ** 20

ROW_TILE = 256
Q_TILE = 128
KV_TILE = 768
SOFTMAX_ROWS = 64
LOG2_E = 1.4426950408889634
FF_TILE = 512
FFN_ROW_CHUNKS = 4
MOE_TILE = 128
WINDOW_NARROW = 40
WINDOW_WIDE = MOE_TILE + SUBLANES
GATHER_UNROLL = 8


def _params(*sem):
    return pltpu.CompilerParams(dimension_semantics=sem, vmem_limit_bytes=VMEM_LIMIT)


def _silu(x):
    return x * (1.0 / (1.0 + jnp.exp(-x)))


def _sigmoid(x):
    return 1.0 / (1.0 + jnp.exp(-x))


def _norm_mod(x, gain, shift, scale):
    ms = jnp.mean(x * x, axis=-1, keepdims=True)
    y = x * lax.rsqrt(ms + NORM_EPS) * gain
    return y * (1.0 + scale) + shift


def _segment_mean_square(p, ones_bd, seg):
    sq = p * p
    hi = sq.astype(BF16)
    lo = (sq - hi.astype(F32)).astype(BF16)
    s = (jnp.dot(hi, ones_bd, preferred_element_type=F32)
         + jnp.dot(lo, ones_bd, preferred_element_type=F32))
    return s * (1.0 / seg)


def _rope(x, cos, sin_signed, dist):
    n = x.shape[-1]
    lane = lax.broadcasted_iota(jnp.int32, x.shape, 1)
    up = pltpu.roll(x, n - dist, 1)
    dn = pltpu.roll(x, dist, 1)
    partner = jnp.where((lane & dist) == 0, up, dn)
    return x * cos + partner * sin_signed


def _mod_kernel(c_ref, w_ref, b_ref, o_ref):
    s = _silu(c_ref[...])
    o_ref[0] = jnp.dot(s, w_ref[0], precision=HIGHEST, preferred_element_type=F32) + b_ref[0]


def _modulation(cvec, w_mod, b_mod):
    depth, d, n6 = w_mod.shape
    tn = 1536
    return pl.pallas_call(
        _mod_kernel,
        grid=(depth, n6 // tn),
        in_specs=[pl.BlockSpec((SUBLANES, d), lambda l, j: (0, 0)),
                  pl.BlockSpec((1, d, tn), lambda l, j: (l, 0, j)),
                  pl.BlockSpec((1, 1, tn), lambda l, j: (l, 0, j))],
        out_specs=pl.BlockSpec((1, SUBLANES, tn), lambda l, j: (l, 0, j)),
        out_shape=jax.ShapeDtypeStruct((depth, SUBLANES, n6), F32),
        compiler_params=_params("parallel", "parallel"),
        name="modulation",
    )(cvec, w_mod, b_mod.reshape(depth, 1, n6))


def _mod_spec(n_ctx_tiles, tile0=0):
    return lambda b, i: (jnp.where(i + tile0 < n_ctx_tiles, 0, 1 + b), 0, 0)


def _inproj_even_kernel(x_ref, mod_ref, gain_ref, w_ref, cos_ref, sin_ref, qn_ref, kn_ref, ones_ref,
                        q_ref, k_ref, v_ref, pc_ref, z_ref, ab_ref):
    h = _norm_mod(x_ref[0], gain_ref[...], mod_ref[0, 0:1, :], mod_ref[0, 1:2, :]).astype(BF16)

    def proj(lo, hi):
        return jnp.dot(h, w_ref[:, lo:hi], preferred_element_type=F32)

    c = cos_ref[...]
    s = sin_ref[...]
    nq, nk = HQ_A * DH_A, 2 * HKV_A * DH_A
    q = proj(0, nq)
    q = q * lax.rsqrt(_segment_mean_square(q, ones_ref[...], DH_A) + NORM_EPS) * qn_ref[...]
    q = _rope(q, jnp.concatenate([c] * (nq // LANES), axis=1),
              jnp.concatenate([s] * (nq // LANES), axis=1), DH_A // 4)
    q_ref[0] = (q * DH_A ** -0.5).astype(BF16)
    k = proj(nq, nq + nk)
    k = k * lax.rsqrt(_segment_mean_square(k, ones_ref[0:nk, 0:nk], DH_A) + NORM_EPS) * kn_ref[...]
    k = _rope(k, jnp.concatenate([c] * (nk // LANES), axis=1),
              jnp.concatenate([s] * (nk // LANES), axis=1), DH_A // 4)
    k_ref[0] = k.astype(BF16)
    o = nq + nk
    v_ref[0] = proj(o, o + nk).astype(BF16)
    o += nk
    nconv = pc_ref.shape[2]
    pc_ref[0] = proj(o, o + nconv)
    o += nconv
    nz = z_ref.shape[2]
    z_ref[0] = proj(o, o + nz)
    o += nz
    ab_ref[0] = proj(o, o + LANES)


def _inproj_even(xs, mod, gain, w, cos, sin, qn, kn, ones_bd, n_ctx):
    b, t, d = xs.shape
    tm = ROW_TILE
    nq, nk = HQ_A * DH_A, 2 * HKV_A * DH_A
    nconv, nz = 3 * H_B * DK_B, H_B * DK_B
    row = lambda w_: pl.BlockSpec((1, tm, w_), lambda bb, i: (bb, i, 0))
    const = lambda a: pl.BlockSpec(a.shape, lambda bb, i: (0,) * a.ndim)
    return pl.pallas_call(
        _inproj_even_kernel,
        grid=(b, t // tm),
        in_specs=[row(d), pl.BlockSpec((1, 6, d), _mod_spec(n_ctx // tm)), const(gain), const(w),
                  pl.BlockSpec((tm, LANES), lambda bb, i: (i, 0)),
                  pl.BlockSpec((tm, LANES), lambda bb, i: (i, 0)),
                  const(qn), const(kn), const(ones_bd)],
        out_specs=[row(nq), row(nk), row(nk), row(nconv), row(nz), row(LANES)],
        out_shape=[jax.ShapeDtypeStruct((b, t, nq), BF16), jax.ShapeDtypeStruct((b, t, nk), BF16),
                   jax.ShapeDtypeStruct((b, t, nk), BF16), jax.ShapeDtypeStruct((b, t, nconv), F32),
                   jax.ShapeDtypeStruct((b, t, nz), F32), jax.ShapeDtypeStruct((b, t, LANES), F32)],
        compiler_params=_params("parallel", "parallel"),
        name="inproj_even",
    )(xs, mod, gain, w, cos, sin, qn, kn, ones_bd)


def _attn_a_kernel(sink_ref, q_ref, k_ref, v_ref, o_ref, *, n_ctx, t_all):
    i = pl.program_id(1)
    g_heads = HQ_A // HKV_A
    band = 3 * Q_TILE
    n = i - n_ctx // Q_TILE
    start = jnp.clip(n_ctx + (n - 1) * Q_TILE, 0, t_all - band)
    start = pl.multiple_of(start, Q_TILE)
    q = q_ref[0]
    rows = g_heads * Q_TILE
    lane = lax.broadcasted_iota(jnp.int32, (Q_TILE, LANES), 1)
    qpos = n * Q_TILE + (lax.broadcasted_iota(jnp.int32, (rows, band), 0) & (Q_TILE - 1))
    kpos = (start - n_ctx) + lax.broadcasted_iota(jnp.int32, (rows, band), 1)
    valid = (n >= 0) & (jnp.abs(kpos - qpos) <= WINDOW) & (kpos >= 0)
    dims = (((1,), (1,)), ((), ()))
    for h in range(HKV_A):
        cols = slice(LANES * h, LANES * (h + 1))
        parts, sinks = [], []
        for g in range(g_heads):
            j = g_heads * h + g
            tile = q[:, LANES * (j // 2):LANES * (j // 2 + 1)]
            keep = (lane >= DH_A * (j % 2)) & (lane < DH_A * (j % 2 + 1))
            parts.append(jnp.where(keep, tile, jnp.zeros_like(tile)))
            sinks.append(jnp.full((Q_TILE, 1), sink_ref[j], F32))
        qs = jnp.concatenate(parts, axis=0)
        sk = jnp.concatenate(sinks, axis=0)
        kc = k_ref[0, 0:n_ctx, cols]
        vc = v_ref[0, 0:n_ctx, cols]
        kb = k_ref[0, pl.ds(start, band), cols]
        vb = v_ref[0, pl.ds(start, band), cols]
        s_c = lax.dot_general(qs, kc, dims, preferred_element_type=F32)
        s_b = lax.dot_general(qs, kb, dims, preferred_element_type=F32)
        s_b = jnp.where(valid, s_b, NEG_INF)
        m = jnp.maximum(jnp.maximum(s_c.max(axis=1, keepdims=True), s_b.max(axis=1, keepdims=True)), sk)
        p_c = jnp.exp(s_c - m)
        p_b = jnp.exp(s_b - m)
        den = p_c.sum(axis=1, keepdims=True) + p_b.sum(axis=1, keepdims=True) + jnp.exp(sk - m)
        o = (jnp.dot(p_c.astype(BF16), vc, preferred_element_type=F32)
             + jnp.dot(p_b.astype(BF16), vb, preferred_element_type=F32)) / den
        for pair in range(g_heads // 2):
            lo = o[(2 * pair) * Q_TILE:(2 * pair + 1) * Q_TILE]
            hi = o[(2 * pair + 1) * Q_TILE:(2 * pair + 2) * Q_TILE]
            c0 = LANES * (g_heads // 2 * h + pair)
            o_ref[0, :, c0:c0 + LANES] = jnp.where(lane < DH_A, lo, hi).astype(BF16)


def _attn_a(sink, q, k, v, n_ctx):
    b, t, nq = q.shape
    nk = k.shape[2]
    return pl.pallas_call(
        functools.partial(_attn_a_kernel, n_ctx=n_ctx, t_all=t),
        grid=(b, t // Q_TILE),
        in_specs=[pl.BlockSpec(memory_space=pltpu.SMEM),
                  pl.BlockSpec((1, Q_TILE, nq), lambda bb, i: (bb, i, 0)),
                  pl.BlockSpec((1, t, nk), lambda bb, i: (bb, 0, 0)),
                  pl.BlockSpec((1, t, nk), lambda bb, i: (bb, 0, 0))],
        out_specs=pl.BlockSpec((1, Q_TILE, nq), lambda bb, i: (bb, i, 0)),
        out_shape=jax.ShapeDtypeStruct((b, t, nq), BF16),
        compiler_params=_params("parallel", "arbitrary"),
        name="attn_window",
    )(sink, q, k, v)


def _gdn_prep_kernel(pc_ref, prev_ref, next_ref, cw_ref, ab_ref, alog_ref, dtb_ref,
                     q_ref, k_ref, v_ref, gb_ref, ext_sc, *, n_ctx, t_all):
    tm = pc_ref.shape[1]
    r0 = pl.program_id(1) * tm
    halo = SUBLANES
    prev_on = jnp.where((r0 == 0) | (r0 == n_ctx), 0.0, 1.0)
    next_on = jnp.where((r0 + tm == n_ctx) | (r0 + tm == t_all), 0.0, 1.0)
    ext_sc[0:halo, :] = prev_ref[0] * prev_on
    ext_sc[halo:halo + tm, :] = pc_ref[0]
    ext_sc[halo + tm:2 * halo + tm, :] = next_ref[0] * next_on
    nh = H_B * DK_B
    for grp, out_ref in enumerate((q_ref, k_ref, v_ref)):
        c0 = nh * grp
        acc = None
        for tap in range(CONV_K):
            off = halo - CONV_K // 2 + tap
            term = cw_ref[tap:tap + 1, c0:c0 + nh] * ext_sc[off:off + tm, c0:c0 + nh]
            acc = term if acc is None else acc + term
        y = _silu(acc)
        if grp == 2:
            out_ref[0] = y
            continue
        scale = DK_B ** -0.5 if grp == 0 else 1.0
        for h in range(H_B):
            yh = y[:, DK_B * h:DK_B * (h + 1)]
            inv = lax.rsqrt(jnp.sum(yh * yh, axis=-1, keepdims=True) + NORM_EPS)
            out_ref[0, :, DK_B * h:DK_B * (h + 1)] = yh * (inv * scale)
    ab = ab_ref[0]
    lane = lax.broadcasted_iota(jnp.int32, ab.shape, 1)
    xg = ab + dtb_ref[...]
    softplus = jnp.maximum(xg, 0.0) + jnp.log(1.0 + jnp.exp(-jnp.abs(xg)))
    g = -jnp.exp(alog_ref[...]) * softplus
    gb_ref[0] = jnp.where(lane < 2 * H_B, g, jnp.where(lane < 4 * H_B, _sigmoid(ab), 0.0))


def _gdn_prep(pc, conv_w, ab, alog, dtb, n_ctx):
    b, t, nconv = pc.shape
    tm = ROW_TILE
    nh = H_B * DK_B
    hb = tm // SUBLANES
    nblk = t // SUBLANES
    row = lambda w_: pl.BlockSpec((1, tm, w_), lambda bb, i: (bb, i, 0))
    const = lambda a: pl.BlockSpec(a.shape, lambda bb, i: (0,) * a.ndim)
    return pl.pallas_call(
        functools.partial(_gdn_prep_kernel, n_ctx=n_ctx, t_all=t),
        grid=(b, t // tm),
        in_specs=[row(nconv),
                  pl.BlockSpec((1, SUBLANES, nconv), lambda bb, i: (bb, jnp.maximum(i * hb - 1, 0), 0)),
                  pl.BlockSpec((1, SUBLANES, nconv),
                               lambda bb, i: (bb, jnp.minimum((i + 1) * hb, nblk - 1), 0)),
                  const(conv_w), row(LANES), const(alog), const(dtb)],
        out_specs=[row(nh), row(nh), row(nh), row(LANES)],
        out_shape=[jax.ShapeDtypeStruct((b, t, nh), F32)] * 3 + [jax.ShapeDtypeStruct((b, t, LANES), F32)],
        scratch_shapes=[pltpu.VMEM((tm + 2 * SUBLANES, nconv), F32)],
        compiler_params=_params("parallel", "parallel"),
        name="gdn_prep",
    )(pc, pc, pc, conv_w, ab, alog, dtb)


def _gdn_chunk_kernel(q_ref, k_ref, v_ref, gb_ref, u_ref, w_ref, qg_ref, a_ref, kdt_ref, ge_ref):
    c = CHUNK
    nchunks = q_ref.shape[1] // c
    r_i = lax.broadcasted_iota(jnp.int32, (c, c), 0)
    c_i = lax.broadcasted_iota(jnp.int32, (c, c), 1)
    tri_l = (r_i >= c_i).astype(F32)
    tri_u = (r_i <= c_i).astype(F32)
    row2 = lax.broadcasted_iota(jnp.int32, (c, LANES), 0)
    col2 = lax.broadcasted_iota(jnp.int32, (c, LANES), 1)
    colm = col2 & (c - 1)
    lane8 = lax.broadcasted_iota(jnp.int32, (1, LANES), 1)
    half_of = [(col2 // c) == (h % 2) for h in range(H_B)]
    eye_stack = jnp.concatenate(
        [jnp.where(half_of[h] & (row2 == colm), 1.0, 0.0) for h in range(H_B)], axis=0)
    stack = lambda m: jnp.concatenate([jnp.where(m, 1.0, 0.0)] * H_B, axis=0)
    base_mask = stack((row2 // INV_BASE) == (colm // INV_BASE))
    level_masks = [[], []]
    size = INV_BASE
    while size < c:
        same = (row2 // (2 * size)) == (colm // (2 * size))
        r_hi, c_hi = (row2 // size) % 2 == 1, (colm // size) % 2 == 1
        level_masks[0].append(stack(same & r_hi & ~c_hi))
        level_masks[1].append(stack(same & ~r_hi & c_hi))
        size *= 2

    def bmm(ls, rs):
        rcat = jnp.concatenate([rs[0:2 * c], rs[2 * c:4 * c]], axis=1).astype(BF16)
        full = jnp.dot(ls.astype(BF16), rcat, preferred_element_type=F32)
        return jnp.concatenate([full[0:2 * c, 0:LANES], full[2 * c:4 * c, LANES:2 * LANES]], axis=0)

    def wide(t_stack, mats):
        rv = jnp.concatenate([jnp.concatenate(mats[0:2], axis=0),
                              jnp.concatenate(mats[2:4], axis=0)], axis=1).astype(BF16)
        full = jnp.dot(t_stack.astype(BF16), rv, preferred_element_type=F32)
        return [full[c * h:c * (h + 1), LANES * (h // 2):LANES * (h // 2 + 1)] for h in range(H_B)]

    dims = (((1,), (1,)), ((), ()))
    for ci in range(nchunks):
        rows = slice(c * ci, c * (ci + 1))
        gb = gb_ref[0, rows, :]
        gc = jnp.where(lane8 < H_B,
                       jnp.dot(tri_l, gb, precision=HIGHEST, preferred_element_type=F32),
                       jnp.dot(tri_u, gb, precision=HIGHEST, preferred_element_type=F32))
        gc_t = gc.T
        eg = jnp.exp(gc)
        g_last = jnp.where(lane8 < H_B, gc[c - 1:c, :], gc[0:1, :])
        ge_ref[0, ci] = jnp.broadcast_to(jnp.exp(g_last), (SUBLANES, LANES))
        ek = jnp.exp(g_last - gc)
        qs = [q_ref[0, rows, DK_B * h:DK_B * (h + 1)] for h in range(H_B)]
        ks = [k_ref[0, rows, DK_B * h:DK_B * (h + 1)] for h in range(H_B)]
        vs = [v_ref[0, rows, DK_B * h:DK_B * (h + 1)] for h in range(H_B)]
        raw = []
        for h in range(H_B):
            kq = jnp.concatenate([ks[h], qs[h]], axis=0).astype(BF16)
            kk = jnp.concatenate([ks[h], ks[h]], axis=0).astype(BF16)
            raw.append(lax.dot_general(kq, kk, dims, preferred_element_type=F32))
        for d in range(2):
            keep = (row2 >= colm) if d == 0 else (row2 <= colm)
            strict = (row2 > colm) if d == 0 else (row2 < colm)
            a_blocks, aqk, betas, egs = [], [], [], []
            for h in range(H_B):
                idx = H_B * d + h
                g_col = gc[:, idx:idx + 1]
                g_row = jnp.concatenate([gc_t[idx:idx + 1, :]] * 2, axis=1)
                decay = jnp.where(keep, jnp.exp(jnp.where(keep, g_col - g_row, 0.0)), 0.0)
                beta = gb[:, 2 * H_B + idx:2 * H_B + idx + 1]
                betas.append(beta)
                egs.append(eg[:, idx:idx + 1])
                a_blocks.append(jnp.where(strict & half_of[h], beta * raw[h][0:c] * decay, 0.0))
                aqk.append(raw[h][c:2 * c] * decay)
            a_stack = jnp.concatenate(a_blocks, axis=0)
            x = -a_stack * base_mask
            t_inv = eye_stack + x
            p = bmm(x, x)
            t_inv = t_inv + bmm(t_inv, p)
            p = bmm(p, p)
            t_inv = t_inv + bmm(t_inv, p)
            for lvl in range(len(level_masks[d])):
                t_inv = t_inv - bmm(bmm(t_inv, a_stack * level_masks[d][lvl]), t_inv)
            us = wide(t_inv, [vs[h] * betas[h] for h in range(H_B)])
            ws = wide(t_inv, [ks[h] * (betas[h] * egs[h]) for h in range(H_B)])
            for h in range(H_B):
                idx = H_B * d + h
                cols = slice(DK_B * h, DK_B * (h + 1))
                u_ref[d, 0, rows, cols] = us[h]
                w_ref[d, 0, rows, cols] = ws[h].astype(BF16)
                qg_ref[d, 0, rows, cols] = (qs[h] * egs[h]).astype(BF16)
            for pair in range(H_B // 2):
                h0, h1 = 2 * pair, 2 * pair + 1
                a_ref[d, 0, rows, LANES * pair:LANES * (pair + 1)] = jnp.where(
                    col2 < c, aqk[h0], aqk[h1]).astype(BF16)
                kd0 = (ks[h0] * ek[:, H_B * d + h0:H_B * d + h0 + 1]).T
                kd1 = (ks[h1] * ek[:, H_B * d + h1:H_B * d + h1 + 1]).T
                kdt_ref[d, 0, ci, :, LANES * pair:LANES * (pair + 1)] = jnp.concatenate(
                    [kd0, kd1], axis=1).astype(BF16)


def _gdn_chunk(qb, kb, vb, gb):
    b, t, nh = qb.shape
    tm = ROW_TILE
    cps = tm // CHUNK
    nck = t // CHUNK
    row = lambda w_: pl.BlockSpec((1, tm, w_), lambda bb, i: (bb, i, 0))
    drow = lambda w_: pl.BlockSpec((2, 1, tm, w_), lambda bb, i: (0, bb, i, 0))
    return pl.pallas_call(
        _gdn_chunk_kernel,
        grid=(b, t // tm),
        in_specs=[row(nh), row(nh), row(nh), row(LANES)],
        out_specs=[drow(nh), drow(nh), drow(nh), drow(nh // 2),
                   pl.BlockSpec((2, 1, cps, DK_B, nh // 2), lambda bb, i: (0, bb, i, 0, 0)),
                   pl.BlockSpec((1, cps, SUBLANES, LANES), lambda bb, i: (bb, i, 0, 0))],
        out_shape=[jax.ShapeDtypeStruct((2, b, t, nh), F32), jax.ShapeDtypeStruct((2, b, t, nh), BF16),
                   jax.ShapeDtypeStruct((2, b, t, nh), BF16),
                   jax.ShapeDtypeStruct((2, b, t, nh // 2), BF16),
                   jax.ShapeDtypeStruct((2, b, nck, DK_B, nh // 2), BF16),
                   jax.ShapeDtypeStruct((b, nck, SUBLANES, LANES), F32)],
        compiler_params=_params("parallel", "parallel"),
        name="gdn_chunk",
    )(qb, kb, vb, gb)


def _gdn_scan_kernel(u_ref, w_ref, qg_ref, a_ref, kdt_ref, ge_ref, o_ref, s_sc):
    d = pl.program_id(1)

    @pl.when(pl.program_id(2) == 0)
    def _():
        s_sc[...] = jnp.zeros_like(s_sc)

    ge = ge_ref[0, 0]
    zero = jnp.zeros((CHUNK, DK_B), BF16)
    for h in range(H_B):
        cols = slice(DK_B * h, DK_B * (h + 1))
        pair = slice(LANES * (h // 2), LANES * (h // 2 + 1))
        s = s_sc[h]
        wq = jnp.concatenate([w_ref[0, 0, :, cols], qg_ref[0, 0, :, cols]], axis=0)
        r = jnp.dot(wq, s.astype(BF16), preferred_element_type=F32)
        v_new = (u_ref[0, 0, :, cols] - r[0:CHUNK]).astype(BF16)
        v_pad = jnp.concatenate([v_new, zero] if h % 2 == 0 else [zero, v_new], axis=0)
        o_ref[0, 0, :, cols] = r[CHUNK:2 * CHUNK] + jnp.dot(
            a_ref[0, 0, :, pair], v_pad, preferred_element_type=F32)
        g_end = jnp.where(d == 0, ge[0:1, h:h + 1], ge[0:1, H_B + h:H_B + h + 1])
        s_sc[h] = s * g_end + jnp.dot(kdt_ref[0, 0, 0, :, pair], v_pad, preferred_element_type=F32)


def _gdn_scan(u, w, qg, a, kdt, ge, n_ctx):
    _, b, t, nh = u.shape
    nck = t // CHUNK
    ncc = n_ctx // CHUNK

    def chunk(d, s):
        back = jnp.where(s < ncc, ncc - 1 - s, nck - 1 - (s - ncc))
        return jnp.where(d == 0, s, back)

    drow = lambda w_: pl.BlockSpec((1, 1, CHUNK, w_), lambda bb, d, s: (d, bb, chunk(d, s), 0))
    return pl.pallas_call(
        _gdn_scan_kernel,
        grid=(b, 2, nck),
        in_specs=[drow(nh), drow(nh), drow(nh), drow(nh // 2),
                  pl.BlockSpec((1, 1, 1, DK_B, nh // 2), lambda bb, d, s: (d, bb, chunk(d, s), 0, 0)),
                  pl.BlockSpec((1, 1, SUBLANES, LANES), lambda bb, d, s: (bb, chunk(d, s), 0, 0))],
        out_specs=drow(nh),
        out_shape=jax.ShapeDtypeStruct((2, b, t, nh), F32),
        scratch_shapes=[pltpu.VMEM((H_B, DK_B, DK_B), F32)],
        compiler_params=_params("parallel", "parallel", "arbitrary"),
        name="gdn_scan",
    )(u, w, qg, a, kdt, ge)


def _residual_router(y, x_ref, mod_ref, g2_ref, wr_ref, xn_ref, h2_ref, aff_ref):
    xn = x_ref[0] + mod_ref[0, 2:3, :] * y
    xn_ref[0] = xn
    h2 = _norm_mod(xn, g2_ref[...], mod_ref[0, 3:4, :], mod_ref[0, 4:5, :])
    h2_ref[0] = h2
    logits = jnp.dot(h2, wr_ref[...], precision=HIGHEST, preferred_element_type=F32)
    lane = lax.broadcasted_iota(jnp.int32, logits.shape, 1)
    logits = jnp.where(lane < N_EXPERTS, logits, NEG_INF)
    e = jnp.exp(logits - logits.max(axis=-1, keepdims=True))
    aff_ref[0] = e / e.sum(axis=-1, keepdims=True)


def _outproj_even_kernel(oa_ref, of_ref, ob_ref, z_ref, x_ref, mod_ref, w_ref, onorm_ref, g2_ref, wr_ref,
                         xn_ref, h2_ref, aff_ref):
    na = oa_ref.shape[2]
    y = jnp.dot(oa_ref[0], w_ref[0:na, :], preferred_element_type=F32)
    o = of_ref[0, 0] + ob_ref[0, 0]
    z = z_ref[0]
    for h in range(H_B):
        cols = slice(DK_B * h, DK_B * (h + 1))
        oh = o[:, cols]
        ms = jnp.mean(oh * oh, axis=-1, keepdims=True)
        yh = oh * lax.rsqrt(ms + NORM_EPS) * onorm_ref[...] * _silu(z[:, cols])
        y = y + jnp.dot(yh.astype(BF16), w_ref[na + DK_B * h:na + DK_B * (h + 1), :],
                        preferred_element_type=F32)
    _residual_router(y, x_ref, mod_ref, g2_ref, wr_ref, xn_ref, h2_ref, aff_ref)


def _outproj_odd_kernel(o_ref, x_ref, mod_ref, w_ref, g2_ref, wr_ref, xn_ref, h2_ref, aff_ref):
    y = jnp.dot(o_ref[0], w_ref[...], preferred_element_type=F32)
    _residual_router(y, x_ref, mod_ref, g2_ref, wr_ref, xn_ref, h2_ref, aff_ref)


def _outproj_call(kernel, name, acts, xs, mod, consts, n_ctx, tile0):
    b, t, d = xs.shape
    tm = ROW_TILE
    nt = t // tm - tile0
    specs = []
    for a, lead, off in acts:
        if lead is None:
            specs.append(pl.BlockSpec((1, tm, a.shape[-1]), lambda bb, i, off=off: (bb, i + off, 0)))
        else:
            specs.append(pl.BlockSpec((1, 1, tm, a.shape[-1]),
                                      lambda bb, i, lead=lead, off=off: (lead, bb, i + off, 0)))
    row = pl.BlockSpec((1, tm, d), lambda bb, i: (bb, i + tile0, 0))
    const = lambda a: pl.BlockSpec(a.shape, lambda bb, i: (0,) * a.ndim)
    mod_spec = pl.BlockSpec((1, 6, d), _mod_spec(n_ctx // tm, tile0))
    wout, rest = consts[0], consts[1:]
    return pl.pallas_call(
        kernel,
        grid=(b, nt),
        in_specs=specs + [row, mod_spec, const(wout)] + [const(a) for a in rest],
        out_specs=[pl.BlockSpec((1, tm, d), lambda bb, i: (bb, i, 0)),
                   pl.BlockSpec((1, tm, d), lambda bb, i: (bb, i, 0)),
                   pl.BlockSpec((1, tm, LANES), lambda bb, i: (bb, i, 0))],
        out_shape=[jax.ShapeDtypeStruct((b, nt * tm, d), F32), jax.ShapeDtypeStruct((b, nt * tm, d), F32),
                   jax.ShapeDtypeStruct((b, nt * tm, LANES), F32)],
        compiler_params=_params("parallel", "parallel"),
        name=name,
    )(*[a for a, _, _ in acts], xs, mod, wout, *rest)


def _inproj_odd_kernel(x_ref, mod_ref, gain_ref, w_ref, cos_ref, sin_ref, qn_ref, kn_ref,
                       q_ref, k_ref, v_ref):
    h = _norm_mod(x_ref[0], gain_ref[...], mod_ref[0, 0:1, :], mod_ref[0, 1:2, :]).astype(BF16)
    c = cos_ref[...]
    s = sin_ref[...]
    nq, nk = HQ_C * DH_C, HKV_C * DH_C

    def normed_heads(lo, nheads, gain_ref_, out_ref, scale):
        p = jnp.dot(h, w_ref[:, lo:lo + nheads * DH_C], preferred_element_type=F32)
        for hh in range(nheads):
            ph = p[:, DH_C * hh:DH_C * (hh + 1)]
            ms = jnp.mean(ph * ph, axis=-1, keepdims=True)
            ph = _rope(ph * lax.rsqrt(ms + NORM_EPS) * gain_ref_[...], c, s, DH_C // 4)
            out_ref[0, :, DH_C * hh:DH_C * (hh + 1)] = (ph * scale).astype(BF16)

    normed_heads(0, HQ_C, qn_ref, q_ref, DH_C ** -0.5 * LOG2_E)
    normed_heads(nq, HKV_C, kn_ref, k_ref, 1.0)
    v_ref[0] = jnp.dot(h, w_ref[:, nq + nk:nq + 2 * nk], preferred_element_type=F32).astype(BF16)


def _inproj_odd(xs, mod, gain, w, cos, sin, qn, kn, n_ctx):
    b, t, d = xs.shape
    tm = ROW_TILE
    nq, nk = HQ_C * DH_C, HKV_C * DH_C
    row = lambda w_: pl.BlockSpec((1, tm, w_), lambda bb, i: (bb, i, 0))
    const = lambda a: pl.BlockSpec(a.shape, lambda bb, i: (0,) * a.ndim)
    return pl.pallas_call(
        _inproj_odd_kernel,
        grid=(b, t // tm),
        in_specs=[row(d), pl.BlockSpec((1, 6, d), _mod_spec(n_ctx // tm)), const(gain), const(w),
                  pl.BlockSpec((tm, LANES), lambda bb, i: (i, 0)),
                  pl.BlockSpec((tm, LANES), lambda bb, i: (i, 0)),
                  const(qn), const(kn)],
        out_specs=[row(nq), row(nk), row(nk)],
        out_shape=[jax.ShapeDtypeStruct((b, t, nq), BF16), jax.ShapeDtypeStruct((b, t, nk), BF16),
                   jax.ShapeDtypeStruct((b, t, nk), BF16)],
        compiler_params=_params("parallel", "parallel"),
        name="inproj_odd",
    )(xs, mod, gain, w, cos, sin, qn, kn)


def _attn_c_kernel(q_ref, k_ref, v_ref, o_ref, s_sc, p_sc, m_sc, l_sc, a_sc, acc_sc):
    nchunk = k_ref.shape[1] // KV_TILE
    g_heads = HQ_C // HKV_C
    ntile = KV_TILE // LANES
    nrows = g_heads * Q_TILE
    dims = (((1,), (1,)), ((), ()))
    qs = [jnp.concatenate([q_ref[0, :, DH_C * (g_heads * h + g):DH_C * (g_heads * h + g + 1)]
                           for g in range(g_heads)], axis=0) for h in range(HKV_C)]
    m_sc[...] = jnp.full(m_sc.shape, NEG_INF, F32)
    l_sc[...] = jnp.zeros(l_sc.shape, F32)
    acc_sc[...] = jnp.zeros(acc_sc.shape, F32)

    def body(ci, carry):
        r0 = pl.multiple_of(ci * KV_TILE, KV_TILE)
        for h in range(HKV_C):
            s_sc[h] = lax.dot_general(qs[h], k_ref[0, pl.ds(r0, KV_TILE), DH_C * h:DH_C * (h + 1)],
                                      dims, preferred_element_type=F32)
        for h in range(HKV_C):
            for rb in range(nrows // SOFTMAX_ROWS):
                rows = slice(SOFTMAX_ROWS * rb, SOFTMAX_ROWS * (rb + 1))
                mx = s_sc[h, rows, 0:LANES]
                for t in range(1, ntile):
                    mx = jnp.maximum(mx, s_sc[h, rows, LANES * t:LANES * (t + 1)])
                m_old = m_sc[h, rows, :]
                m_new = jnp.maximum(m_old, jnp.broadcast_to(mx.max(axis=1, keepdims=True),
                                                            (SOFTMAX_ROWS, LANES)))
                alpha = jnp.exp2(m_old - m_new)
                l_new = alpha * l_sc[h, rows, :]
                for t in range(ntile):
                    p = jnp.exp2(s_sc[h, rows, LANES * t:LANES * (t + 1)] - m_new)
                    l_new = l_new + p
                    p_sc[h, rows, LANES * t:LANES * (t + 1)] = p.astype(BF16)
                l_sc[h, rows, :] = l_new
                m_sc[h, rows, :] = m_new
                a_sc[h, rows, :] = alpha
        for h in range(HKV_C):
            pv = jnp.dot(p_sc[h], v_ref[0, pl.ds(r0, KV_TILE), DH_C * h:DH_C * (h + 1)],
                         preferred_element_type=F32)
            acc_sc[h] = a_sc[h] * acc_sc[h] + pv
        return carry

    lax.fori_loop(0, nchunk, body, 0)
    for j in range(HQ_C):
        h, rows = j // g_heads, slice(Q_TILE * (j % g_heads), Q_TILE * (j % g_heads + 1))
        o = acc_sc[h, rows, :] / l_sc[h, rows, :].sum(axis=1, keepdims=True)
        o_ref[0, :, DH_C * j:DH_C * (j + 1)] = o.astype(BF16)


def _attn_c(q, k, v, n_ctx):
    b, t, nq = q.shape
    nk = k.shape[2]
    t0 = n_ctx // Q_TILE
    rows = HQ_C // HKV_C * Q_TILE
    return pl.pallas_call(
        _attn_c_kernel,
        grid=(b, t // Q_TILE - t0),
        in_specs=[pl.BlockSpec((1, Q_TILE, nq), lambda bb, i: (bb, i + t0, 0)),
                  pl.BlockSpec((1, t, nk), lambda bb, i: (bb, 0, 0)),
                  pl.BlockSpec((1, t, nk), lambda bb, i: (bb, 0, 0))],
        out_specs=pl.BlockSpec((1, Q_TILE, nq), lambda bb, i: (bb, i, 0)),
        out_shape=jax.ShapeDtypeStruct((b, t - n_ctx, nq), BF16),
        scratch_shapes=[pltpu.VMEM((HKV_C, rows, KV_TILE), F32), pltpu.VMEM((HKV_C, rows, KV_TILE), BF16),
                        pltpu.VMEM((HKV_C, rows, LANES), F32), pltpu.VMEM((HKV_C, rows, LANES), F32),
                        pltpu.VMEM((HKV_C, rows, LANES), F32), pltpu.VMEM((HKV_C, rows, DH_C), F32)],
        compiler_params=_params("parallel", "arbitrary"),
        name="attn_global",
    )(q, k, v)


def _route_group(aff, row0, cap, gsel_ref, cex_ref, idx_ref, sel_sc, cin_sc):
    n = aff.shape[0]
    tok = lax.broadcasted_iota(jnp.int32, (n, LANES), 0)

    def count(mask):
        return jnp.sum(jnp.where(mask, 1.0, 0.0), axis=0, keepdims=True)

    def bisect(steps, lo, hi, enough):
        def body(_, c):
            lo_, hi_ = c
            mid = lo_ + ((hi_ - lo_) >> 1)
            ok = enough(mid)
            return jnp.where(ok, mid, lo_), jnp.where(ok, hi_, mid)
        return lax.fori_loop(0, steps, body, (lo, hi))

    as_float = lambda bits: pltpu.bitcast(bits, F32)
    one_bits = 0x3F800001
    lo, hi = bisect(31, jnp.zeros((1, LANES), jnp.int32), jnp.full((1, LANES), one_bits, jnp.int32),
                    lambda mid: count(aff >= as_float(mid)) >= cap)
    thr, nxt = as_float(lo), as_float(hi)
    above = aff >= nxt
    need = cap - count(above)
    tie_tok = jnp.where(aff >= thr, jnp.where(above, n, tok), n)
    cut, _ = bisect(n.bit_length(), jnp.zeros((1, LANES), jnp.int32), jnp.full((1, LANES), n, jnp.int32),
                    lambda mid: count(tie_tok < mid) < need)
    sel = above | (tie_tok <= cut)
    sel_sc[0:n, :] = jnp.where(sel, 1.0, 0.0)
    gsel_ref[0, row0:row0 + n, :] = jnp.where(sel, aff, 0.0)

    blk = min(n, 2 * LANES)
    r_i = lax.broadcasted_iota(jnp.int32, (blk, blk), 0)
    c_i = lax.broadcasted_iota(jnp.int32, (blk, blk), 1)
    tri = (r_i >= c_i).astype(BF16)

    def cum_body(bi, carry):
        r0 = pl.multiple_of(bi * blk, blk)
        s_blk = sel_sc[pl.ds(r0, blk), :]
        c_blk = jnp.dot(tri, s_blk.astype(BF16), preferred_element_type=F32) + carry
        cin_sc[pl.ds(r0, blk), :] = c_blk
        cex_ref[0, pl.ds(pl.multiple_of(row0 + r0, SUBLANES), blk), :] = (c_blk - s_blk).astype(jnp.int32)
        return c_blk[blk - 1:blk, :]

    lax.fori_loop(0, n // blk, cum_body, jnp.zeros((1, LANES), F32))

    slot = lax.broadcasted_iota(jnp.int32, (1, cap), 1).astype(F32)
    for e in range(N_EXPERTS):
        def idx_body(bi, acc):
            r0 = pl.multiple_of(bi * blk, blk)
            col = cin_sc[pl.ds(r0, blk), e:e + 1]
            return acc + jnp.sum(jnp.where(col <= slot, 1.0, 0.0), axis=0, keepdims=True)
        acc = lax.fori_loop(0, n // blk, idx_body, jnp.zeros((1, cap), F32))
        idx_ref[0, e:e + 1, :] = acc.astype(jnp.int32)


def _route_kernel(aff_ref, gsel_ref, cex_ref, *rest, groups):
    idx_refs, (sel_sc, cin_sc) = rest[:len(groups)], rest[len(groups):]
    for (row0, n, cap), idx_ref in zip(groups, idx_refs):
        _route_group(aff_ref[0, row0:row0 + n, :], row0, cap, gsel_ref, cex_ref, idx_ref, sel_sc, cin_sc)


def _route(aff, groups):
    b, tl, _ = aff.shape
    blk = pl.BlockSpec((1, tl, LANES), lambda bb: (bb, 0, 0))
    nmax = max(n for _, n, _ in groups)
    return pl.pallas_call(
        functools.partial(_route_kernel, groups=groups),
        grid=(b,),
        in_specs=[blk],
        out_specs=[blk, blk] + [pl.BlockSpec((1, N_EXPERTS, cap), lambda bb: (bb, 0, 0)) for _, _, cap in groups],
        out_shape=[jax.ShapeDtypeStruct((b, tl, LANES), F32), jax.ShapeDtypeStruct((b, tl, LANES), jnp.int32)]
        + [jax.ShapeDtypeStruct((b, N_EXPERTS, cap), jnp.int32) for _, _, cap in groups],
        scratch_shapes=[pltpu.VMEM((nmax, LANES), F32), pltpu.VMEM((nmax, LANES), F32)],
        compiler_params=_params("parallel"),
        name="route",
    )(aff)


def _expert_ffn_kernel(idx_ref, h_hbm, wg_ref, wu_ref, wd_ref, y_ref, xbuf, xb, sem, *, nrows):
    e = pl.program_id(0)
    f = pl.program_id(1)

    def start_gather(ee, slot):
        def body(j, c):
            pltpu.make_async_copy(h_hbm.at[pl.ds(idx_ref[ee * nrows + j], 1), :],
                                  xbuf.at[slot, pl.ds(j, 1), :], sem.at[slot]).start()
            return c
        lax.fori_loop(0, nrows, body, 0, unroll=GATHER_UNROLL)

    @pl.when(f == 0)
    def _():
        slot = e % 2

        @pl.when(e == 0)
        def _():
            start_gather(0, 0)

        @pl.when(e + 1 < pl.num_programs(0))
        def _():
            start_gather(e + 1, 1 - slot)

        pltpu.make_async_copy(h_hbm.at[pl.ds(0, nrows), :], xbuf.at[slot], sem.at[slot]).wait()
        xb[...] = xbuf[slot].astype(BF16)

    wg = wg_ref[0, 0].astype(BF16)
    wu = wu_ref[0, 0].astype(BF16)
    wd = wd_ref[0, 0].astype(BF16)
    rc = nrows // FFN_ROW_CHUNKS
    for ci in range(FFN_ROW_CHUNKS):
        rows = slice(rc * ci, rc * (ci + 1))
        x = xb[rows, :]
        g = jnp.dot(x, wg, preferred_element_type=F32)
        u = jnp.dot(x, wu, preferred_element_type=F32)
        part = jnp.dot((_silu(g) * u).astype(BF16), wd, preferred_element_type=F32)

        @pl.when(f == 0)
        def _():
            y_ref[0, rows, :] = part

        @pl.when(f > 0)
        def _():
            y_ref[0, rows, :] += part


def _expert_ffn(idx_flat, h_flat, w_gate, w_up, w_down, layer, nrows):
    _, e, d, ff = w_gate.shape
    nf = ff // FF_TILE
    return pl.pallas_call(
        functools.partial(_expert_ffn_kernel, nrows=nrows),
        grid_spec=pltpu.PrefetchScalarGridSpec(
            num_scalar_prefetch=1,
            grid=(e, nf),
            in_specs=[pl.BlockSpec(memory_space=pl.ANY),
                      pl.BlockSpec((1, 1, d, FF_TILE), lambda ee, f, idx: (layer, ee, 0, f)),
                      pl.BlockSpec((1, 1, d, FF_TILE), lambda ee, f, idx: (layer, ee, 0, f)),
                      pl.BlockSpec((1, 1, FF_TILE, d), lambda ee, f, idx: (layer, ee, f, 0))],
            out_specs=pl.BlockSpec((1, nrows, d), lambda ee, f, idx: (ee, 0, 0),
                                   pipeline_mode=pl.Buffered(1)),
            scratch_shapes=[pltpu.VMEM((2, nrows, d), F32), pltpu.VMEM((nrows, d), BF16),
                            pltpu.SemaphoreType.DMA((2,))]),
        out_shape=jax.ShapeDtypeStruct((e, nrows, d), F32),
        compiler_params=_params("arbitrary", "arbitrary"),
        name="expert_ffn",
    )(idx_flat, h_flat, w_gate, w_up, w_down)


def _combine_kernel(wide_ref, start_n_ref, start_w_ref, y_hbm, xn_ref, mod_ref, cex_ref, gsel_ref,
                    delta_ref, o_ref, stage, sem):
    nt = pl.num_programs(1)
    step = pl.program_id(0) * nt + pl.program_id(1)
    nsteps = pl.num_programs(0) * nt
    slot = step % 2

    def copies(step_, slot_, start_ref, win):
        return [pltpu.make_async_copy(
            y_hbm.at[pl.ds(pl.multiple_of(start_ref[step_ * N_EXPERTS + e], SUBLANES), win), :],
            stage.at[slot_, pl.ds(e * win, win), :], sem.at[slot_]) for e in range(N_EXPERTS)]

    def by_width(step_, fn):
        @pl.when(wide_ref[step_] == 0)
        def _():
            fn(start_n_ref, WINDOW_NARROW, 0)

        @pl.when(wide_ref[step_] != 0)
        def _():
            fn(start_w_ref, WINDOW_WIDE, 1)

    def start(step_, slot_):
        by_width(step_, lambda ref, win, _: [cp.start() for cp in copies(step_, slot_, ref, win)])

    @pl.when(step == 0)
    def _():
        start(0, 0)

    @pl.when(step + 1 < nsteps)
    def _():
        start(step + 1, 1 - slot)

    gate = gsel_ref[0]
    cex = cex_ref[0]
    tile = xn_ref.shape[1]

    def reduce(start_ref, win, delta_row):
        for cp in copies(step, slot, start_ref, win):
            cp.wait()
        rpos = cex + delta_ref[0, 0, delta_row:delta_row + 1, :]
        acc = None
        for c0 in range(0, N_EXPERTS * win, LANES):
            lane_r = c0 + lax.broadcasted_iota(jnp.int32, (tile, LANES), 1)
            q = jnp.zeros((tile, LANES), F32)
            for e in range(c0 // win, min(N_EXPERTS - 1, (c0 + LANES - 1) // win) + 1):
                q = jnp.where(rpos[:, e:e + 1] == lane_r, gate[:, e:e + 1], q)
            part = jnp.dot(q.astype(BF16), stage[slot, c0:c0 + LANES, :].astype(BF16),
                           preferred_element_type=F32)
            acc = part if acc is None else acc + part
        o_ref[0] = xn_ref[0] + mod_ref[0, 5:6, :] * acc

    by_width(step, reduce)


def _combine(wide, start_n, start_w, y_flat, xn, mod, cex, gsel, delta, n_ctx_tiles):
    b, tl, d = xn.shape
    nt = tl // MOE_TILE
    tok = lambda w_: pl.BlockSpec((1, MOE_TILE, w_), lambda bb, i, *_: (bb, i, 0))
    return pl.pallas_call(
        _combine_kernel,
        grid_spec=pltpu.PrefetchScalarGridSpec(
            num_scalar_prefetch=3,
            grid=(b, nt),
            in_specs=[pl.BlockSpec(memory_space=pl.ANY), tok(d),
                      pl.BlockSpec((1, 6, d), lambda bb, i, *_: (jnp.where(i < n_ctx_tiles, 0, 1 + bb), 0, 0)),
                      tok(LANES), tok(LANES),
                      pl.BlockSpec((1, 1, SUBLANES, LANES), lambda bb, i, *_: (bb, i, 0, 0))],
            out_specs=tok(d),
            scratch_shapes=[pltpu.VMEM((2, N_EXPERTS * WINDOW_WIDE, d), F32),
                            pltpu.SemaphoreType.DMA((2,))]),
        out_shape=jax.ShapeDtypeStruct((b, tl, d), F32),
        compiler_params=_params("arbitrary", "arbitrary"),
        name="moe_combine",
    )(wide, start_n, start_w, y_flat, xn, mod, cex, gsel, delta)


def _moe(h2, aff, xn, mod, groups, w_gate, w_up, w_down, layer):
    b, tl, d = h2.shape
    ne = w_gate.shape[1]
    caps = [max(1, CAP_FACTOR * n // ne) for _, n in groups]
    nrows = sum(b * cap for cap in caps)
    assert all(n % MOE_TILE == 0 for _, n in groups) and all(cap % SUBLANES == 0 for cap in caps)
    gsel, cex, *idxs = _route(aff, [(row0, n, cap) for (row0, n), cap in zip(groups, caps)])
    idx_parts, src_parts, cnt_parts = [], [], []
    base = 0
    batch = jnp.arange(b, dtype=jnp.int32)
    for (row0, n), cap, idx in zip(groups, caps, idxs):
        rows = idx + (batch * tl + row0)[:, None, None]
        idx_parts.append(jnp.swapaxes(rows, 0, 1).reshape(ne, b * cap))
        s0 = cex[:, row0:row0 + n:MOE_TILE, :ne]
        ends = jnp.concatenate([s0[:, 1:], jnp.full((b, 1, ne), cap, jnp.int32)], axis=1)
        first = (base + batch * cap)[:, None, None] + (jnp.arange(ne, dtype=jnp.int32) * nrows)[None, None, :]
        src_parts.append(jnp.stack([s0 + first, jnp.broadcast_to(first, s0.shape)], axis=0))
        cnt_parts.append(ends - s0)
        base += b * cap
    cat = lambda parts, axis: parts[0] if len(parts) == 1 else jnp.concatenate(parts, axis=axis)
    src, first = cat(src_parts, 2)
    cnt = cat(cnt_parts, 1)
    wide = (cnt > WINDOW_NARROW - (SUBLANES - 1)).any(axis=-1).astype(jnp.int32)
    starts, deltas = [], []
    for win in (WINDOW_NARROW, WINDOW_WIDE):
        st = jnp.minimum(src // SUBLANES * SUBLANES, ne * nrows - win)
        starts.append(st.reshape(-1))
        deltas.append(jnp.arange(ne, dtype=jnp.int32) * win + first - st)
    delta = jnp.pad(jnp.stack(deltas, axis=2), ((0, 0), (0, 0), (0, SUBLANES - 2), (0, LANES - ne)))
    y = _expert_ffn(cat(idx_parts, 1).reshape(-1), h2.reshape(b * tl, d), w_gate, w_up, w_down, layer, nrows)
    return _combine(wide.reshape(-1), starts[0], starts[1], y.reshape(ne * nrows, d), xn, mod, cex, gsel,
                    delta, groups[0][1] // MOE_TILE if len(groups) > 1 else 0)


def _rope_tables(s_len, n_ctx, head_dim):
    quarter = head_dim // 4
    t = jnp.arange(s_len)
    row = (t // GRID_W).astype(F32)
    col = (t % GRID_W).astype(F32)
    inv = ROPE_THETA ** (-jnp.arange(quarter, dtype=F32) / quarter)
    ar, ac = row[:, None] * inv, col[:, None] * inv
    cos = jnp.concatenate([jnp.cos(ar), jnp.cos(ar), jnp.cos(ac), jnp.cos(ac)], axis=1)
    sin = jnp.concatenate([-jnp.sin(ar), jnp.sin(ar), -jnp.sin(ac), jnp.sin(ac)], axis=1)
    reps = LANES // head_dim
    cos, sin = jnp.tile(cos, (1, reps)), jnp.tile(sin, (1, reps))
    cos = jnp.concatenate([jnp.ones((n_ctx, LANES), F32), cos], axis=0)
    sin = jnp.concatenate([jnp.zeros((n_ctx, LANES), F32), sin], axis=0)
    return cos, sin


def _even_weight(w):
    nq, nkv = HQ_A * DH_A, HKV_A * DH_A
    q = w[:, :nq]
    dup = lambda m: jnp.concatenate(
        [m[:, DH_A * (h // 2):DH_A * (h // 2 + 1)] for h in range(2 * HKV_A)], axis=1)
    k = dup(w[:, nq:nq + nkv])
    v = dup(w[:, nq + nkv:nq + 2 * nkv])
    o = nq + 2 * nkv
    nconv, nz = 3 * H_B * DK_B, H_B * DK_B
    conv = w[:, o:o + nconv]
    z = w[:, o + nconv:o + nconv + nz]
    ab = w[:, o + nconv + nz:]
    ab = jnp.pad(ab, ((0, 0), (0, LANES - ab.shape[1])))
    return jnp.concatenate([q, k, v, conv, z, ab], axis=1).astype(BF16)


def _lane_vec(v):
    v = v.reshape(1, -1)
    return jnp.pad(v, ((0, 0), (0, LANES - v.shape[1])))


def kernel(x, c, ctx, c_ctx, w_mod, b_mod, norm_mix, norm_ffn, w_in_ab, w_out_ab, qnorm_a, knorm_a,
           sink_a, conv_b, a_log_b, dt_bias_b, onorm_b, w_in_c, w_out_c, qnorm_c, knorm_c,
           w_router, w_gate, w_up, w_down):
    b, s_len, d = x.shape
    n_ctx = ctx.shape[1]
    depth = w_mod.shape[0]
    assert b + 1 <= SUBLANES and n_ctx % ROW_TILE == 0 and s_len % ROW_TILE == 0
    t_all = n_ctx + s_len

    cvec = jnp.concatenate([c_ctx[None], c, jnp.zeros((SUBLANES - 1 - b, d), F32)], axis=0)
    mod = _modulation(cvec, w_mod, b_mod).reshape(depth, SUBLANES, 6, d)
    cos_a, sin_a = _rope_tables(s_len, n_ctx, DH_A)
    cos_c, sin_c = _rope_tables(s_len, n_ctx, DH_C)
    nq_a = HQ_A * DH_A
    seg = jnp.arange(nq_a) // DH_A
    ones_bd = (seg[:, None] == seg[None, :]).astype(BF16)

    xs = jnp.concatenate([ctx, x], axis=1)
    for i in range(depth):
        last = i == depth - 1
        j = i // 2
        gain1 = norm_mix[i].reshape(1, d)
        gain2 = norm_ffn[i].reshape(1, d)
        w_r = jnp.pad(w_router[i], ((0, 0), (0, LANES - N_EXPERTS)))
        if i % 2 == 0:
            q, k, v, pc, z, ab = _inproj_even(
                xs, mod[i], gain1, _even_weight(w_in_ab[j]), cos_a, sin_a,
                jnp.tile(qnorm_a[j], HQ_A).reshape(1, -1), jnp.tile(knorm_a[j], 2 * HKV_A).reshape(1, -1),
                ones_bd, n_ctx)
            oa = _attn_a(sink_a[j], q, k, v, n_ctx)
            qb, kb, vb, gb = _gdn_prep(pc, conv_b[j], ab, _lane_vec(a_log_b[j]), _lane_vec(dt_bias_b[j]),
                                       n_ctx)
            u, w, qg, a, kdt, ge = _gdn_chunk(qb, kb, vb, gb)
            o_dir = _gdn_scan(u, w, qg, a, kdt, ge, n_ctx)
            tile0 = n_ctx // ROW_TILE if last else 0
            xn, h2, aff = _outproj_call(
                _outproj_even_kernel, "outproj_even",
                [(oa, None, tile0), (o_dir, 0, tile0), (o_dir, 1, tile0), (z, None, tile0)], xs, mod[i],
                [w_out_ab[j].astype(BF16), onorm_b[j].reshape(1, -1), gain2, w_r], n_ctx, tile0)
        else:
            q, k, v = _inproj_odd(xs, mod[i], gain1, w_in_c[j].astype(BF16), cos_c, sin_c,
                                  qnorm_c[j].reshape(1, -1), knorm_c[j].reshape(1, -1), n_ctx)
            if last:
                o = _attn_c(q, k, v, n_ctx)
                tile0 = n_ctx // ROW_TILE
            else:
                raise NotImplementedError("context queries of a non-final odd layer")
            xn, h2, aff = _outproj_call(
                _outproj_odd_kernel, "outproj_odd", [(o, None, 0)], xs, mod[i],
                [w_out_c[j].astype(BF16), gain2, w_r], n_ctx, tile0)
        groups = [(0, s_len)] if last else [(0, n_ctx), (n_ctx, s_len)]
        xs = _moe(h2, aff, xn, mod[i], groups, w_gate, w_up, w_down, i)
    return xs
```

```python
import functools

import jax
import jax.numpy as jnp
from jax import lax
from jax.experimental import pallas as pl
from jax.experimental.pallas import tpu as pltpu

F32 = jnp.float32
BF16 = jnp.bfloat16
HIGHEST = lax.Precision.HIGHEST

GRID_W = 64
NORM_EPS = 1e-6
ROPE_THETA = 10000.0
NEG_INF = -1e30
HQ_A, HKV_A, DH_A, WINDOW = 8, 2, 64, 128
H_B, DK_B, CONV_K, CHUNK = 4, 128, 5, 64
INV_BASE = 8
HQ_C, HKV_C, DH_C = 8, 2, 128
N_EXPERTS, CAP_FACTOR = 16, 2

LANES = 128
SUBLANES = 8
VMEM_LIMIT = 56 * 2 ** 20

ROW_TILE = 256
Q_TILE = 128
KV_TILE_MAX = 1408
SOFTMAX_ROWS = 64
LOG2_E = 1.4426950408889634
FF_TILE = 512
FFN_ROW_CHUNKS = 2
MOE_TILE = 128
WINDOW_NARROW = 40
WINDOW_WIDE = MOE_TILE + SUBLANES
GATHER_UNROLL = 8


def _params(*sem):
    return pltpu.CompilerParams(dimension_semantics=sem, vmem_limit_bytes=VMEM_LIMIT)


def _silu(x):
    return x * (1.0 / (1.0 + jnp.exp(-x)))


def _sigmoid(x):
    return 1.0 / (1.0 + jnp.exp(-x))


def _norm_mod(x, gain, shift, scale):
    ms = jnp.mean(x * x, axis=-1, keepdims=True)
    y = x * lax.rsqrt(ms + NORM_EPS) * gain
    return y * (1.0 + scale) + shift


def _segment_mean_square(p, ones_bd, seg):
    sq = p * p
    hi = sq.astype(BF16)
    lo = (sq - hi.astype(F32)).astype(BF16)
    s = (jnp.dot(hi, ones_bd, preferred_element_type=F32)
         + jnp.dot(lo, ones_bd, preferred_element_type=F32))
    return s * (1.0 / seg)


def _rope(x, cos, sin_signed, dist):
    n = x.shape[-1]
    lane = lax.broadcasted_iota(jnp.int32, x.shape, 1)
    up = pltpu.roll(x, n - dist, 1)
    dn = pltpu.roll(x, dist, 1)
    partner = jnp.where((lane & dist) == 0, up, dn)
    return x * cos + partner * sin_signed


def _mod_kernel(c_ref, w_ref, b_ref, o_ref):
    s = _silu(c_ref[...])
    o_ref[0] = jnp.dot(s, w_ref[0], precision=HIGHEST, preferred_element_type=F32) + b_ref[0]


def _modulation(cvec, w_mod, b_mod):
    depth, d, n6 = w_mod.shape
    tn = 1536
    return pl.pallas_call(
        _mod_kernel,
        grid=(depth, n6 // tn),
        in_specs=[pl.BlockSpec((SUBLANES, d), lambda l, j: (0, 0)),
                  pl.BlockSpec((1, d, tn), lambda l, j: (l, 0, j)),
                  pl.BlockSpec((1, 1, tn), lambda l, j: (l, 0, j))],
        out_specs=pl.BlockSpec((1, SUBLANES, tn), lambda l, j: (l, 0, j)),
        out_shape=jax.ShapeDtypeStruct((depth, SUBLANES, n6), F32),
        compiler_params=_params("parallel", "parallel"),
        name="modulation",
    )(cvec, w_mod, b_mod.reshape(depth, 1, n6))


def _mod_spec(n_ctx_tiles, tile0=0):
    return lambda b, i: (jnp.where(i + tile0 < n_ctx_tiles, 0, 1 + b), 0, 0)


def _inproj_even_kernel(x_ref, mod_ref, gain_ref, w_ref, cos_ref, sin_ref, qn_ref, kn_ref, ones_ref,
                        q_ref, k_ref, v_ref, pc_ref, z_ref, ab_ref):
    h = _norm_mod(x_ref[0], gain_ref[...], mod_ref[0, 0:1, :], mod_ref[0, 1:2, :]).astype(BF16)

    def proj(lo, hi):
        return jnp.dot(h, w_ref[:, lo:hi], preferred_element_type=F32)

    c = cos_ref[...]
    s = sin_ref[...]
    nq, nk = HQ_A * DH_A, 2 * HKV_A * DH_A
    q = proj(0, nq)
    q = q * lax.rsqrt(_segment_mean_square(q, ones_ref[...], DH_A) + NORM_EPS) * qn_ref[...]
    q = _rope(q, jnp.concatenate([c] * (nq // LANES), axis=1),
              jnp.concatenate([s] * (nq // LANES), axis=1), DH_A // 4)
    q_ref[0] = (q * DH_A ** -0.5).astype(BF16)
    k = proj(nq, nq + nk)
    k = k * lax.rsqrt(_segment_mean_square(k, ones_ref[0:nk, 0:nk], DH_A) + NORM_EPS) * kn_ref[...]
    k = _rope(k, jnp.concatenate([c] * (nk // LANES), axis=1),
              jnp.concatenate([s] * (nk // LANES), axis=1), DH_A // 4)
    k_ref[0] = k.astype(BF16)
    o = nq + nk
    v_ref[0] = proj(o, o + nk).astype(BF16)
    o += nk
    nconv = pc_ref.shape[2]
    pc_ref[0] = proj(o, o + nconv)
    o += nconv
    nz = z_ref.shape[2]
    z_ref[0] = proj(o, o + nz)
    o += nz
    ab_ref[0] = proj(o, o + LANES)


def _inproj_even(xs, mod, gain, w, cos, sin, qn, kn, ones_bd, n_ctx):
    b, t, d = xs.shape
    tm = ROW_TILE
    nq, nk = HQ_A * DH_A, 2 * HKV_A * DH_A
    nconv, nz = 3 * H_B * DK_B, H_B * DK_B
    row = lambda w_: pl.BlockSpec((1, tm, w_), lambda bb, i: (bb, i, 0))
    const = lambda a: pl.BlockSpec(a.shape, lambda bb, i: (0,) * a.ndim)
    return pl.pallas_call(
        _inproj_even_kernel,
        grid=(b, t // tm),
        in_specs=[row(d), pl.BlockSpec((1, 6, d), _mod_spec(n_ctx // tm)), const(gain), const(w),
                  pl.BlockSpec((tm, LANES), lambda bb, i: (i, 0)),
                  pl.BlockSpec((tm, LANES), lambda bb, i: (i, 0)),
                  const(qn), const(kn), const(ones_bd)],
        out_specs=[row(nq), row(nk), row(nk), row(nconv), row(nz), row(LANES)],
        out_shape=[jax.ShapeDtypeStruct((b, t, nq), BF16), jax.ShapeDtypeStruct((b, t, nk), BF16),
                   jax.ShapeDtypeStruct((b, t, nk), BF16), jax.ShapeDtypeStruct((b, t, nconv), F32),
                   jax.ShapeDtypeStruct((b, t, nz), F32), jax.ShapeDtypeStruct((b, t, LANES), F32)],
        compiler_params=_params("parallel", "parallel"),
        name="inproj_even",
    )(xs, mod, gain, w, cos, sin, qn, kn, ones_bd)


def _attn_a_kernel(sink_ref, q_ref, k_ref, v_ref, o_ref, *, n_ctx, t_all):
    i = pl.program_id(1)
    g_heads = HQ_A // HKV_A
    band = 3 * Q_TILE
    n = i - n_ctx // Q_TILE
    start = jnp.clip(n_ctx + (n - 1) * Q_TILE, 0, t_all - band)
    start = pl.multiple_of(start, Q_TILE)
    q = q_ref[0]
    rows = g_heads * Q_TILE
    lane = lax.broadcasted_iota(jnp.int32, (Q_TILE, LANES), 1)
    qpos = n * Q_TILE + (lax.broadcasted_iota(jnp.int32, (rows, band), 0) & (Q_TILE - 1))
    kpos = (start - n_ctx) + lax.broadcasted_iota(jnp.int32, (rows, band), 1)
    valid = (n >= 0) & (jnp.abs(kpos - qpos) <= WINDOW) & (kpos >= 0)
    dims = (((1,), (1,)), ((), ()))
    for h in range(HKV_A):
        cols = slice(LANES * h, LANES * (h + 1))
        parts, sinks = [], []
        for g in range(g_heads):
            j = g_heads * h + g
            tile = q[:, LANES * (j // 2):LANES * (j // 2 + 1)]
            keep = (lane >= DH_A * (j % 2)) & (lane < DH_A * (j % 2 + 1))
            parts.append(jnp.where(keep, tile, jnp.zeros_like(tile)))
            sinks.append(jnp.full((Q_TILE, 1), sink_ref[j], F32))
        qs = jnp.concatenate(parts, axis=0)
        sk = jnp.concatenate(sinks, axis=0)
        kc = k_ref[0, 0:n_ctx, cols]
        vc = v_ref[0, 0:n_ctx, cols]
        kb = k_ref[0, pl.ds(start, band), cols]
        vb = v_ref[0, pl.ds(start, band), cols]
        s_c = lax.dot_general(qs, kc, dims, preferred_element_type=F32)
        s_b = lax.dot_general(qs, kb, dims, preferred_element_type=F32)
        s_b = jnp.where(valid, s_b, NEG_INF)
        m = jnp.maximum(jnp.maximum(s_c.max(axis=1, keepdims=True), s_b.max(axis=1, keepdims=True)), sk)
        p_c = jnp.exp(s_c - m)
        p_b = jnp.exp(s_b - m)
        den = p_c.sum(axis=1, keepdims=True) + p_b.sum(axis=1, keepdims=True) + jnp.exp(sk - m)
        o = (jnp.dot(p_c.astype(BF16), vc, preferred_element_type=F32)
             + jnp.dot(p_b.astype(BF16), vb, preferred_element_type=F32)) / den
        for pair in range(g_heads // 2):
            lo = o[(2 * pair) * Q_TILE:(2 * pair + 1) * Q_TILE]
            hi = o[(2 * pair + 1) * Q_TILE:(2 * pair + 2) * Q_TILE]
            c0 = LANES * (g_heads // 2 * h + pair)
            o_ref[0, :, c0:c0 + LANES] = jnp.where(lane < DH_A, lo, hi).astype(BF16)


def _attn_a(sink, q, k, v, n_ctx):
    b, t, nq = q.shape
    nk = k.shape[2]
    return pl.pallas_call(
        functools.partial(_attn_a_kernel, n_ctx=n_ctx, t_all=t),
        grid=(b, t // Q_TILE),
        in_specs=[pl.BlockSpec(memory_space=pltpu.SMEM),
                  pl.BlockSpec((1, Q_TILE, nq), lambda bb, i: (bb, i, 0)),
                  pl.BlockSpec((1, t, nk), lambda bb, i: (bb, 0, 0)),
                  pl.BlockSpec((1, t, nk), lambda bb, i: (bb, 0, 0))],
        out_specs=pl.BlockSpec((1, Q_TILE, nq), lambda bb, i: (bb, i, 0)),
        out_shape=jax.ShapeDtypeStruct((b, t, nq), BF16),
        compiler_params=_params("parallel", "arbitrary"),
        name="attn_window",
    )(sink, q, k, v)


def _gdn_prep_kernel(pc_ref, prev_ref, next_ref, cw_ref, ab_ref, alog_ref, dtb_ref,
                     q_ref, k_ref, v_ref, gb_ref, ext_sc, *, n_ctx, t_all):
    tm = pc_ref.shape[1]
    r0 = pl.program_id(1) * tm
    halo = SUBLANES
    prev_on = jnp.where((r0 == 0) | (r0 == n_ctx), 0.0, 1.0)
    next_on = jnp.where((r0 + tm == n_ctx) | (r0 + tm == t_all), 0.0, 1.0)
    ext_sc[0:halo, :] = prev_ref[0] * prev_on
    ext_sc[halo:halo + tm, :] = pc_ref[0]
    ext_sc[halo + tm:2 * halo + tm, :] = next_ref[0] * next_on
    nh = H_B * DK_B
    for grp, out_ref in enumerate((q_ref, k_ref, v_ref)):
        c0 = nh * grp
        acc = None
        for tap in range(CONV_K):
            off = halo - CONV_K // 2 + tap
            term = cw_ref[tap:tap + 1, c0:c0 + nh] * ext_sc[off:off + tm, c0:c0 + nh]
            acc = term if acc is None else acc + term
        y = _silu(acc)
        if grp == 2:
            out_ref[0] = y
            continue
        scale = DK_B ** -0.5 if grp == 0 else 1.0
        for h in range(H_B):
            yh = y[:, DK_B * h:DK_B * (h + 1)]
            inv = lax.rsqrt(jnp.sum(yh * yh, axis=-1, keepdims=True) + NORM_EPS)
            out_ref[0, :, DK_B * h:DK_B * (h + 1)] = yh * (inv * scale)
    ab = ab_ref[0]
    lane = lax.broadcasted_iota(jnp.int32, ab.shape, 1)
    xg = ab + dtb_ref[...]
    softplus = jnp.maximum(xg, 0.0) + jnp.log(1.0 + jnp.exp(-jnp.abs(xg)))
    g = -jnp.exp(alog_ref[...]) * softplus
    gb_ref[0] = jnp.where(lane < 2 * H_B, g, jnp.where(lane < 4 * H_B, _sigmoid(ab), 0.0))


def _gdn_prep(pc, conv_w, ab, alog, dtb, n_ctx):
    b, t, nconv = pc.shape
    tm = ROW_TILE
    nh = H_B * DK_B
    hb = tm // SUBLANES
    nblk = t // SUBLANES
    row = lambda w_: pl.BlockSpec((1, tm, w_), lambda bb, i: (bb, i, 0))
    const = lambda a: pl.BlockSpec(a.shape, lambda bb, i: (0,) * a.ndim)
    return pl.pallas_call(
        functools.partial(_gdn_prep_kernel, n_ctx=n_ctx, t_all=t),
        grid=(b, t // tm),
        in_specs=[row(nconv),
                  pl.BlockSpec((1, SUBLANES, nconv), lambda bb, i: (bb, jnp.maximum(i * hb - 1, 0), 0)),
                  pl.BlockSpec((1, SUBLANES, nconv),
                               lambda bb, i: (bb, jnp.minimum((i + 1) * hb, nblk - 1), 0)),
                  const(conv_w), row(LANES), const(alog), const(dtb)],
        out_specs=[row(nh), row(nh), row(nh), row(LANES)],
        out_shape=[jax.ShapeDtypeStruct((b, t, nh), F32)] * 3 + [jax.ShapeDtypeStruct((b, t, LANES), F32)],
        scratch_shapes=[pltpu.VMEM((tm + 2 * SUBLANES, nconv), F32)],
        compiler_params=_params("parallel", "parallel"),
        name="gdn_prep",
    )(pc, pc, pc, conv_w, ab, alog, dtb)


def _gdn_chunk_kernel(q_ref, k_ref, v_ref, gb_ref, u_ref, w_ref, qg_ref, a_ref, kdt_ref, ge_ref):
    c = CHUNK
    nchunks = q_ref.shape[1] // c
    r_i = lax.broadcasted_iota(jnp.int32, (c, c), 0)
    c_i = lax.broadcasted_iota(jnp.int32, (c, c), 1)
    tri_l = (r_i >= c_i).astype(F32)
    tri_u = (r_i <= c_i).astype(F32)
    row2 = lax.broadcasted_iota(jnp.int32, (c, LANES), 0)
    col2 = lax.broadcasted_iota(jnp.int32, (c, LANES), 1)
    colm = col2 & (c - 1)
    lane8 = lax.broadcasted_iota(jnp.int32, (1, LANES), 1)
    half_of = [(col2 // c) == (h % 2) for h in range(H_B)]
    eye_stack = jnp.concatenate(
        [jnp.where(half_of[h] & (row2 == colm), 1.0, 0.0) for h in range(H_B)], axis=0)
    stack = lambda m: jnp.concatenate([jnp.where(m, 1.0, 0.0)] * H_B, axis=0)
    base_mask = stack((row2 // INV_BASE) == (colm // INV_BASE))
    level_masks = [[], []]
    size = INV_BASE
    while size < c:
        same = (row2 // (2 * size)) == (colm // (2 * size))
        r_hi, c_hi = (row2 // size) % 2 == 1, (colm // size) % 2 == 1
        level_masks[0].append(stack(same & r_hi & ~c_hi))
        level_masks[1].append(stack(same & ~r_hi & c_hi))
        size *= 2

    def bmm(ls, rs):
        rcat = jnp.concatenate([rs[0:2 * c], rs[2 * c:4 * c]], axis=1).astype(BF16)
        full = jnp.dot(ls.astype(BF16), rcat, preferred_element_type=F32)
        return jnp.concatenate([full[0:2 * c, 0:LANES], full[2 * c:4 * c, LANES:2 * LANES]], axis=0)

    def wide(t_stack, mats):
        rv = jnp.concatenate([jnp.concatenate(mats[0:2], axis=0),
                              jnp.concatenate(mats[2:4], axis=0)], axis=1).astype(BF16)
        full = jnp.dot(t_stack.astype(BF16), rv, preferred_element_type=F32)
        return [full[c * h:c * (h + 1), LANES * (h // 2):LANES * (h // 2 + 1)] for h in range(H_B)]

    dims = (((1,), (1,)), ((), ()))
    head = lambda ref, rows, h: ref[0, rows, DK_B * h:DK_B * (h + 1)]
    combos = []
    for ci in range(nchunks):
        rows = slice(c * ci, c * (ci + 1))
        gb = gb_ref[0, rows, :]
        gc = jnp.where(lane8 < H_B,
                       jnp.dot(tri_l, gb, precision=HIGHEST, preferred_element_type=F32),
                       jnp.dot(tri_u, gb, precision=HIGHEST, preferred_element_type=F32))
        gc_t = gc.T
        eg = jnp.exp(gc)
        g_last = jnp.where(lane8 < H_B, gc[c - 1:c, :], gc[0:1, :])
        ge_ref[0, ci] = jnp.broadcast_to(jnp.exp(g_last), (SUBLANES, LANES))
        ek = jnp.exp(g_last - gc)
        raw = []
        for h in range(H_B):
            k_h = head(k_ref, rows, h)
            kq = jnp.concatenate([k_h, head(q_ref, rows, h)], axis=0).astype(BF16)
            kk = jnp.concatenate([k_h, k_h], axis=0).astype(BF16)
            raw.append(lax.dot_general(kq, kk, dims, preferred_element_type=F32))
        for d in range(2):
            keep = (row2 >= colm) if d == 0 else (row2 <= colm)
            strict = (row2 > colm) if d == 0 else (row2 < colm)
            a_blocks, aqk, betas, egs, eks = [], [], [], [], []
            for h in range(H_B):
                idx = H_B * d + h
                g_col = gc[:, idx:idx + 1]
                g_row = jnp.concatenate([gc_t[idx:idx + 1, :]] * 2, axis=1)
                decay = jnp.where(keep, jnp.exp(jnp.where(keep, g_col - g_row, 0.0)), 0.0)
                beta = gb[:, 2 * H_B + idx:2 * H_B + idx + 1]
                betas.append(beta)
                egs.append(eg[:, idx:idx + 1])
                eks.append(ek[:, idx:idx + 1])
                a_blocks.append(jnp.where(strict & half_of[h], beta * raw[h][0:c] * decay, 0.0))
                aqk.append(raw[h][c:2 * c] * decay)
            combos.append(dict(ci=ci, d=d, rows=rows, a=jnp.concatenate(a_blocks, axis=0),
                               aqk=aqk, betas=betas, egs=egs, eks=eks))

    xs = [-cb["a"] * base_mask for cb in combos]
    ts = [eye_stack + x for x in xs]
    ps = [bmm(x, x) for x in xs]
    ts = [t + bmm(t, p) for t, p in zip(ts, ps)]
    ps = [bmm(p, p) for p in ps]
    ts = [t + bmm(t, p) for t, p in zip(ts, ps)]
    for lvl in range(len(level_masks[0])):
        mids = [bmm(t, cb["a"] * level_masks[cb["d"]][lvl]) for t, cb in zip(ts, combos)]
        ts = [t - bmm(m, t) for t, m in zip(ts, mids)]

    for t_inv, cb in zip(ts, combos):
        ci, d, rows = cb["ci"], cb["d"], cb["rows"]
        qs = [head(q_ref, rows, h) for h in range(H_B)]
        ks = [head(k_ref, rows, h) for h in range(H_B)]
        us = wide(t_inv, [head(v_ref, rows, h) * cb["betas"][h] for h in range(H_B)])
        ws = wide(t_inv, [ks[h] * (cb["betas"][h] * cb["egs"][h]) for h in range(H_B)])
        for h in range(H_B):
            cols = slice(DK_B * h, DK_B * (h + 1))
            u_ref[d, 0, rows, cols] = us[h]
            w_ref[d, 0, rows, cols] = ws[h].astype(BF16)
            qg_ref[d, 0, rows, cols] = (qs[h] * cb["egs"][h]).astype(BF16)
        for pair in range(H_B // 2):
            h0, h1 = 2 * pair, 2 * pair + 1
            a_ref[d, 0, rows, LANES * pair:LANES * (pair + 1)] = jnp.where(
                col2 < c, cb["aqk"][h0], cb["aqk"][h1]).astype(BF16)
            kd0 = (ks[h0] * cb["eks"][h0]).T
            kd1 = (ks[h1] * cb["eks"][h1]).T
            kdt_ref[d, 0, ci, :, LANES * pair:LANES * (pair + 1)] = jnp.concatenate(
                [kd0, kd1], axis=1).astype(BF16)


def _gdn_chunk(qb, kb, vb, gb):
    b, t, nh = qb.shape
    tm = ROW_TILE
    cps = tm // CHUNK
    nck = t // CHUNK
    row = lambda w_: pl.BlockSpec((1, tm, w_), lambda bb, i: (bb, i, 0))
    drow = lambda w_: pl.BlockSpec((2, 1, tm, w_), lambda bb, i: (0, bb, i, 0))
    return pl.pallas_call(
        _gdn_chunk_kernel,
        grid=(b, t // tm),
        in_specs=[row(nh), row(nh), row(nh), row(LANES)],
        out_specs=[drow(nh), drow(nh), drow(nh), drow(nh // 2),
                   pl.BlockSpec((2, 1, cps, DK_B, nh // 2), lambda bb, i: (0, bb, i, 0, 0)),
                   pl.BlockSpec((1, cps, SUBLANES, LANES), lambda bb, i: (bb, i, 0, 0))],
        out_shape=[jax.ShapeDtypeStruct((2, b, t, nh), F32), jax.ShapeDtypeStruct((2, b, t, nh), BF16),
                   jax.ShapeDtypeStruct((2, b, t, nh), BF16),
                   jax.ShapeDtypeStruct((2, b, t, nh // 2), BF16),
                   jax.ShapeDtypeStruct((2, b, nck, DK_B, nh // 2), BF16),
                   jax.ShapeDtypeStruct((b, nck, SUBLANES, LANES), F32)],
        compiler_params=_params("parallel", "parallel"),
        name="gdn_chunk",
    )(qb, kb, vb, gb)


def _gdn_scan_kernel(*refs):
    ins = (refs[0:6], refs[6:12])
    outs, s_sc = refs[12:14], refs[14]
    nb = outs[0].shape[0]

    @pl.when(pl.program_id(0) == 0)
    def _():
        s_sc[...] = jnp.zeros_like(s_sc)

    zero = jnp.zeros((CHUNK, DK_B), BF16)
    chains = [(d, b, h) for d in range(2) for b in range(nb) for h in range(H_B)]
    cols = lambda h: slice(DK_B * h, DK_B * (h + 1))
    pair = lambda h: slice(LANES * (h // 2), LANES * (h // 2 + 1))
    states = [s_sc[d, b, h] for d, b, h in chains]
    rs = [jnp.dot(jnp.concatenate([ins[d][1][0, b, :, cols(h)], ins[d][2][0, b, :, cols(h)]], axis=0),
                  s.astype(BF16), preferred_element_type=F32) for (d, b, h), s in zip(chains, states)]
    v_pads = []
    for (d, b, h), r in zip(chains, rs):
        v_new = (ins[d][0][0, b, :, cols(h)] - r[0:CHUNK]).astype(BF16)
        v_pads.append(jnp.concatenate([v_new, zero] if h % 2 == 0 else [zero, v_new], axis=0))
    for (d, b, h), s, r, v_pad in zip(chains, states, rs, v_pads):
        g_end = ins[d][5][b, 0, 0:1, H_B * d + h:H_B * d + h + 1]
        s_sc[d, b, h] = s * g_end + jnp.dot(ins[d][4][0, b, 0, :, pair(h)], v_pad,
                                            preferred_element_type=F32)
    for (d, b, h), r, v_pad in zip(chains, rs, v_pads):
        outs[d][b, :, cols(h)] = r[CHUNK:2 * CHUNK] + jnp.dot(
            ins[d][3][0, b, :, pair(h)], v_pad, preferred_element_type=F32)


def _gdn_scan(u, w, qg, a, kdt, ge, n_ctx):
    _, b, t, nh = u.shape
    nck = t // CHUNK
    ncc = n_ctx // CHUNK
    chunk = (lambda s: s,
             lambda s: jnp.where(s < ncc, ncc - 1 - s, nck - 1 - (s - ncc)))

    def specs(d):
        drow = lambda w_: pl.BlockSpec((1, b, CHUNK, w_), lambda s: (d, 0, chunk[d](s), 0))
        return [drow(nh), drow(nh), drow(nh), drow(nh // 2),
                pl.BlockSpec((1, b, 1, DK_B, nh // 2), lambda s: (d, 0, chunk[d](s), 0, 0)),
                pl.BlockSpec((b, 1, SUBLANES, LANES), lambda s: (0, chunk[d](s), 0, 0))]

    return pl.pallas_call(
        _gdn_scan_kernel,
        grid=(nck,),
        in_specs=specs(0) + specs(1),
        out_specs=[pl.BlockSpec((b, CHUNK, nh), lambda s, d=d: (0, chunk[d](s), 0)) for d in range(2)],
        out_shape=[jax.ShapeDtypeStruct((b, t, nh), F32)] * 2,
        scratch_shapes=[pltpu.VMEM((2, b, H_B, DK_B, DK_B), F32)],
        compiler_params=_params("arbitrary"),
        name="gdn_scan",
    )(u, w, qg, a, kdt, ge, u, w, qg, a, kdt, ge)


def _residual_router(y, x_ref, mod_ref, g2_ref, wr_ref, xn_ref, h2_ref, aff_ref):
    xn = x_ref[0] + mod_ref[0, 2:3, :] * y
    xn_ref[0] = xn
    h2 = _norm_mod(xn, g2_ref[...], mod_ref[0, 3:4, :], mod_ref[0, 4:5, :])
    h2_ref[0] = h2
    logits = jnp.dot(h2, wr_ref[...], precision=HIGHEST, preferred_element_type=F32)
    lane = lax.broadcasted_iota(jnp.int32, logits.shape, 1)
    logits = jnp.where(lane < N_EXPERTS, logits, NEG_INF)
    e = jnp.exp(logits - logits.max(axis=-1, keepdims=True))
    aff_ref[0] = e / e.sum(axis=-1, keepdims=True)


def _outproj_even_kernel(oa_ref, of_ref, ob_ref, z_ref, x_ref, mod_ref, w_ref, onorm_ref, g2_ref, wr_ref,
                         xn_ref, h2_ref, aff_ref):
    na = oa_ref.shape[2]
    y = jnp.dot(oa_ref[0], w_ref[0:na, :], preferred_element_type=F32)
    o = of_ref[0] + ob_ref[0]
    z = z_ref[0]
    for h in range(H_B):
        cols = slice(DK_B * h, DK_B * (h + 1))
        oh = o[:, cols]
        ms = jnp.mean(oh * oh, axis=-1, keepdims=True)
        yh = oh * lax.rsqrt(ms + NORM_EPS) * onorm_ref[...] * _silu(z[:, cols])
        y = y + jnp.dot(yh.astype(BF16), w_ref[na + DK_B * h:na + DK_B * (h + 1), :],
                        preferred_element_type=F32)
    _residual_router(y, x_ref, mod_ref, g2_ref, wr_ref, xn_ref, h2_ref, aff_ref)


def _outproj_odd_kernel(o_ref, x_ref, mod_ref, w_ref, g2_ref, wr_ref, xn_ref, h2_ref, aff_ref):
    y = jnp.dot(o_ref[0], w_ref[...], preferred_element_type=F32)
    _residual_router(y, x_ref, mod_ref, g2_ref, wr_ref, xn_ref, h2_ref, aff_ref)


def _outproj_call(kernel, name, acts, xs, mod, consts, n_ctx, tile0):
    b, t, d = xs.shape
    tm = ROW_TILE
    nt = t // tm - tile0
    specs = []
    for a, lead, off in acts:
        if lead is None:
            specs.append(pl.BlockSpec((1, tm, a.shape[-1]), lambda bb, i, off=off: (bb, i + off, 0)))
        else:
            specs.append(pl.BlockSpec((1, 1, tm, a.shape[-1]),
                                      lambda bb, i, lead=lead, off=off: (lead, bb, i + off, 0)))
    row = pl.BlockSpec((1, tm, d), lambda bb, i: (bb, i + tile0, 0))
    const = lambda a: pl.BlockSpec(a.shape, lambda bb, i: (0,) * a.ndim)
    mod_spec = pl.BlockSpec((1, 6, d), _mod_spec(n_ctx // tm, tile0))
    wout, rest = consts[0], consts[1:]
    return pl.pallas_call(
        kernel,
        grid=(b, nt),
        in_specs=specs + [row, mod_spec, const(wout)] + [const(a) for a in rest],
        out_specs=[pl.BlockSpec((1, tm, d), lambda bb, i: (bb, i, 0)),
                   pl.BlockSpec((1, tm, d), lambda bb, i: (bb, i, 0)),
                   pl.BlockSpec((1, tm, LANES), lambda bb, i: (bb, i, 0))],
        out_shape=[jax.ShapeDtypeStruct((b, nt * tm, d), F32), jax.ShapeDtypeStruct((b, nt * tm, d), F32),
                   jax.ShapeDtypeStruct((b, nt * tm, LANES), F32)],
        compiler_params=_params("parallel", "parallel"),
        name=name,
    )(*[a for a, _, _ in acts], xs, mod, wout, *rest)


def _inproj_odd_kernel(x_ref, mod_ref, gain_ref, w_ref, cos_ref, sin_ref, qn_ref, kn_ref,
                       q_ref, k_ref, v_ref):
    h = _norm_mod(x_ref[0], gain_ref[...], mod_ref[0, 0:1, :], mod_ref[0, 1:2, :]).astype(BF16)
    c = cos_ref[...]
    s = sin_ref[...]
    nq, nk = HQ_C * DH_C, HKV_C * DH_C

    def normed_heads(lo, nheads, gain_ref_, out_ref, scale):
        p = jnp.dot(h, w_ref[:, lo:lo + nheads * DH_C], preferred_element_type=F32)
        for hh in range(nheads):
            ph = p[:, DH_C * hh:DH_C * (hh + 1)]
            ms = jnp.mean(ph * ph, axis=-1, keepdims=True)
            ph = _rope(ph * lax.rsqrt(ms + NORM_EPS) * gain_ref_[...], c, s, DH_C // 4)
            out_ref[0, :, DH_C * hh:DH_C * (hh + 1)] = (ph * scale).astype(BF16)

    normed_heads(0, HQ_C, qn_ref, q_ref, DH_C ** -0.5 * LOG2_E)
    normed_heads(nq, HKV_C, kn_ref, k_ref, 1.0)
    v_ref[0] = jnp.dot(h, w_ref[:, nq + nk:nq + 2 * nk], preferred_element_type=F32).astype(BF16)


def _inproj_odd(xs, mod, gain, w, cos, sin, qn, kn, n_ctx):
    b, t, d = xs.shape
    tm = ROW_TILE
    nq, nk = HQ_C * DH_C, HKV_C * DH_C
    row = lambda w_: pl.BlockSpec((1, tm, w_), lambda bb, i: (bb, i, 0))
    const = lambda a: pl.BlockSpec(a.shape, lambda bb, i: (0,) * a.ndim)
    return pl.pallas_call(
        _inproj_odd_kernel,
        grid=(b, t // tm),
        in_specs=[row(d), pl.BlockSpec((1, 6, d), _mod_spec(n_ctx // tm)), const(gain), const(w),
                  pl.BlockSpec((tm, LANES), lambda bb, i: (i, 0)),
                  pl.BlockSpec((tm, LANES), lambda bb, i: (i, 0)),
                  const(qn), const(kn)],
        out_specs=[row(nq), row(nk), row(nk)],
        out_shape=[jax.ShapeDtypeStruct((b, t, nq), BF16), jax.ShapeDtypeStruct((b, t, nk), BF16),
                   jax.ShapeDtypeStruct((b, t, nk), BF16)],
        compiler_params=_params("parallel", "parallel"),
        name="inproj_odd",
    )(xs, mod, gain, w, cos, sin, qn, kn)


def _attn_c_kernel(q_ref, k_ref, v_ref, o_ref, s_sc, p_sc, m_sc, l_sc, a_sc, acc_sc):
    kv_tile = s_sc.shape[2]
    nchunk = k_ref.shape[1] // kv_tile
    g_heads = HQ_C // HKV_C
    ntile = kv_tile // LANES
    nrows = g_heads * Q_TILE
    dims = (((1,), (1,)), ((), ()))
    qs = [jnp.concatenate([q_ref[0, :, DH_C * (g_heads * h + g):DH_C * (g_heads * h + g + 1)]
                           for g in range(g_heads)], axis=0) for h in range(HKV_C)]
    m_sc[...] = jnp.full(m_sc.shape, NEG_INF, F32)
    l_sc[...] = jnp.zeros(l_sc.shape, F32)
    acc_sc[...] = jnp.zeros(acc_sc.shape, F32)

    def body(ci, carry):
        r0 = pl.multiple_of(ci * kv_tile, kv_tile)
        for h in range(HKV_C):
            s_sc[h] = lax.dot_general(qs[h], k_ref[0, pl.ds(r0, kv_tile), DH_C * h:DH_C * (h + 1)],
                                      dims, preferred_element_type=F32)
        for h in range(HKV_C):
            for rb in range(nrows // SOFTMAX_ROWS):
                rows = slice(SOFTMAX_ROWS * rb, SOFTMAX_ROWS * (rb + 1))
                mx = s_sc[h, rows, 0:LANES]
                for t in range(1, ntile):
                    mx = jnp.maximum(mx, s_sc[h, rows, LANES * t:LANES * (t + 1)])
                m_old = m_sc[h, rows, :]
                m_new = jnp.maximum(m_old, jnp.broadcast_to(mx.max(axis=1, keepdims=True),
                                                            (SOFTMAX_ROWS, LANES)))
                alpha = jnp.exp2(m_old - m_new)
                l_new = alpha * l_sc[h, rows, :]
                for t in range(ntile):
                    p = jnp.exp2(s_sc[h, rows, LANES * t:LANES * (t + 1)] - m_new)
                    l_new = l_new + p
                    p_sc[h, rows, LANES * t:LANES * (t + 1)] = p.astype(BF16)
                l_sc[h, rows, :] = l_new
                m_sc[h, rows, :] = m_new
                a_sc[h, rows, :] = alpha
        for h in range(HKV_C):
            pv = jnp.dot(p_sc[h], v_ref[0, pl.ds(r0, kv_tile), DH_C * h:DH_C * (h + 1)],
                         preferred_element_type=F32)
            acc_sc[h] = a_sc[h] * acc_sc[h] + pv
        return carry

    lax.fori_loop(0, nchunk, body, 0)
    for j in range(HQ_C):
        h, rows = j // g_heads, slice(Q_TILE * (j % g_heads), Q_TILE * (j % g_heads + 1))
        o = acc_sc[h, rows, :] / l_sc[h, rows, :].sum(axis=1, keepdims=True)
        o_ref[0, :, DH_C * j:DH_C * (j + 1)] = o.astype(BF16)


def _attn_c(q, k, v, n_ctx):
    b, t, nq = q.shape
    nk = k.shape[2]
    t0 = n_ctx // Q_TILE
    rows = HQ_C // HKV_C * Q_TILE
    kv_tile = max(w for w in range(LANES, KV_TILE_MAX + 1, LANES) if t % w == 0)
    return pl.pallas_call(
        _attn_c_kernel,
        grid=(b, t // Q_TILE - t0),
        in_specs=[pl.BlockSpec((1, Q_TILE, nq), lambda bb, i: (bb, i + t0, 0)),
                  pl.BlockSpec((1, t, nk), lambda bb, i: (bb, 0, 0)),
                  pl.BlockSpec((1, t, nk), lambda bb, i: (bb, 0, 0))],
        out_specs=pl.BlockSpec((1, Q_TILE, nq), lambda bb, i: (bb, i, 0)),
        out_shape=jax.ShapeDtypeStruct((b, t - n_ctx, nq), BF16),
        scratch_shapes=[pltpu.VMEM((HKV_C, rows, kv_tile), F32), pltpu.VMEM((HKV_C, rows, kv_tile), BF16),
                        pltpu.VMEM((HKV_C, rows, LANES), F32), pltpu.VMEM((HKV_C, rows, LANES), F32),
                        pltpu.VMEM((HKV_C, rows, LANES), F32), pltpu.VMEM((HKV_C, rows, DH_C), F32)],
        compiler_params=_params("parallel", "arbitrary"),
        name="attn_global",
    )(q, k, v)


def _route_group(aff, row0, cap, gsel_ref, cex_ref, idx_ref, sel_sc, cin_sc):
    n = aff.shape[0]
    tok = lax.broadcasted_iota(jnp.int32, (n, LANES), 0)

    def count(mask):
        return jnp.sum(jnp.where(mask, 1.0, 0.0), axis=0, keepdims=True)

    def bisect(steps, lo, hi, enough):
        def body(_, c):
            lo_, hi_ = c
            mid = lo_ + ((hi_ - lo_) >> 1)
            ok = enough(mid)
            return jnp.where(ok, mid, lo_), jnp.where(ok, hi_, mid)
        return lax.fori_loop(0, steps, body, (lo, hi))

    as_float = lambda bits: pltpu.bitcast(bits, F32)
    one_bits = 0x3F800001
    lo, hi = bisect(31, jnp.zeros((1, LANES), jnp.int32), jnp.full((1, LANES), one_bits, jnp.int32),
                    lambda mid: count(aff >= as_float(mid)) >= cap)
    thr, nxt = as_float(lo), as_float(hi)
    above = aff >= nxt
    need = cap - count(above)
    tie_tok = jnp.where(aff >= thr, jnp.where(above, n, tok), n)
    cut, _ = bisect(n.bit_length(), jnp.zeros((1, LANES), jnp.int32), jnp.full((1, LANES), n, jnp.int32),
                    lambda mid: count(tie_tok < mid) < need)
    sel = above | (tie_tok <= cut)
    sel_sc[0:n, :] = jnp.where(sel, 1.0, 0.0)
    gsel_ref[0, row0:row0 + n, :] = jnp.where(sel, aff, 0.0)

    blk = min(n, 2 * LANES)
    r_i = lax.broadcasted_iota(jnp.int32, (blk, blk), 0)
    c_i = lax.broadcasted_iota(jnp.int32, (blk, blk), 1)
    tri = (r_i >= c_i).astype(BF16)

    def cum_body(bi, carry):
        r0 = pl.multiple_of(bi * blk, blk)
        s_blk = sel_sc[pl.ds(r0, blk), :]
        c_blk = jnp.dot(tri, s_blk.astype(BF16), preferred_element_type=F32) + carry
        cin_sc[pl.ds(r0, blk), :] = c_blk
        cex_ref[0, pl.ds(pl.multiple_of(row0 + r0, SUBLANES), blk), :] = (c_blk - s_blk).astype(jnp.int32)
        return c_blk[blk - 1:blk, :]

    lax.fori_loop(0, n // blk, cum_body, jnp.zeros((1, LANES), F32))

    slot = lax.broadcasted_iota(jnp.int32, (1, cap), 1).astype(F32)
    for e in range(N_EXPERTS):
        def idx_body(bi, acc):
            r0 = pl.multiple_of(bi * blk, blk)
            col = cin_sc[pl.ds(r0, blk), e:e + 1]
            return acc + jnp.sum(jnp.where(col <= slot, 1.0, 0.0), axis=0, keepdims=True)
        acc = lax.fori_loop(0, n // blk, idx_body, jnp.zeros((1, cap), F32))
        idx_ref[0, e:e + 1, :] = acc.astype(jnp.int32)


def _route_kernel(aff_ref, gsel_ref, cex_ref, *rest, groups):
    idx_refs, (sel_sc, cin_sc) = rest[:len(groups)], rest[len(groups):]
    for (row0, n, cap), idx_ref in zip(groups, idx_refs):
        _route_group(aff_ref[0, row0:row0 + n, :], row0, cap, gsel_ref, cex_ref, idx_ref, sel_sc, cin_sc)


def _route(aff, groups):
    b, tl, _ = aff.shape
    blk = pl.BlockSpec((1, tl, LANES), lambda bb: (bb, 0, 0))
    nmax = max(n for _, n, _ in groups)
    return pl.pallas_call(
        functools.partial(_route_kernel, groups=groups),
        grid=(b,),
        in_specs=[blk],
        out_specs=[blk, blk] + [pl.BlockSpec((1, N_EXPERTS, cap), lambda bb: (bb, 0, 0)) for _, _, cap in groups],
        out_shape=[jax.ShapeDtypeStruct((b, tl, LANES), F32), jax.ShapeDtypeStruct((b, tl, LANES), jnp.int32)]
        + [jax.ShapeDtypeStruct((b, N_EXPERTS, cap), jnp.int32) for _, _, cap in groups],
        scratch_shapes=[pltpu.VMEM((nmax, LANES), F32), pltpu.VMEM((nmax, LANES), F32)],
        compiler_params=_params("parallel"),
        name="route",
    )(aff)


def _expert_ffn_kernel(idx_ref, h_hbm, wg_ref, wu_ref, wd_ref, y_ref, xbuf, xb, sem, *, nrows, nsteps):
    e = pl.program_id(0)
    f = pl.program_id(1)

    ne, nf = pl.num_programs(0), pl.num_programs(1)

    def row_copy(ee, j):
        return pltpu.make_async_copy(h_hbm.at[pl.ds(idx_ref[ee * nrows + j], 1), :],
                                     xbuf.at[pl.ds(j, 1), :], sem.at[0])

    def wait_rows(n):
        pltpu.make_async_copy(h_hbm.at[pl.ds(0, n), :], xbuf.at[pl.ds(0, n), :], sem.at[0]).wait()

    @pl.when(f == 0)
    def _():
        @pl.when(e == 0)
        def _():
            def body(j, c):
                row_copy(0, j).start()
                return c
            lax.fori_loop(0, nrows, body, 0, unroll=GATHER_UNROLL)

        wait_rows(nrows)
        xb[...] = xbuf[...].astype(BF16)

    per_step = nrows // nsteps
    nxt = jnp.minimum(e + 1, ne - 1)
    for jj in range(per_step):
        row_copy(nxt, f * per_step + jj).start()

    wg = wg_ref[0, 0].astype(BF16)
    wu = wu_ref[0, 0].astype(BF16)
    wd = wd_ref[0, 0].astype(BF16)
    rc = nrows // FFN_ROW_CHUNKS
    for ci in range(FFN_ROW_CHUNKS):
        rows = slice(rc * ci, rc * (ci + 1))
        x = xb[rows, :]
        g = jnp.dot(x, wg, preferred_element_type=F32)
        u = jnp.dot(x, wu, preferred_element_type=F32)
        part = jnp.dot((_silu(g) * u).astype(BF16), wd, preferred_element_type=F32)

        @pl.when(f == 0)
        def _():
            y_ref[0, rows, :] = part

        @pl.when(f > 0)
        def _():
            y_ref[0, rows, :] += part

    @pl.when((e == ne - 1) & (f == nf - 1))
    def _():
        wait_rows(nrows)


def _expert_ffn(idx_flat, h_flat, w_gate, w_up, w_down, layer, nrows):
    _, e, d, ff = w_gate.shape
    nf = ff // FF_TILE
    return pl.pallas_call(
        functools.partial(_expert_ffn_kernel, nrows=nrows, nsteps=nf),
        grid_spec=pltpu.PrefetchScalarGridSpec(
            num_scalar_prefetch=1,
            grid=(e, nf),
            in_specs=[pl.BlockSpec(memory_space=pl.ANY),
                      pl.BlockSpec((1, 1, d, FF_TILE), lambda ee, f, idx: (layer, ee, 0, f)),
                      pl.BlockSpec((1, 1, d, FF_TILE), lambda ee, f, idx: (layer, ee, 0, f)),
                      pl.BlockSpec((1, 1, FF_TILE, d), lambda ee, f, idx: (layer, ee, f, 0))],
            out_specs=pl.BlockSpec((1, nrows, d), lambda ee, f, idx: (ee, 0, 0),
                                   pipeline_mode=pl.Buffered(1)),
            scratch_shapes=[pltpu.VMEM((nrows, d), F32), pltpu.VMEM((nrows, d), BF16),
                            pltpu.SemaphoreType.DMA((1,))]),
        out_shape=jax.ShapeDtypeStruct((e, nrows, d), F32),
        compiler_params=_params("arbitrary", "arbitrary"),
        name="expert_ffn",
    )(idx_flat, h_flat, w_gate, w_up, w_down)


def _combine_kernel(wide_ref, start_n_ref, start_w_ref, y_hbm, xn_ref, mod_ref, cex_ref, gsel_ref,
                    delta_ref, o_ref, stage, sem):
    nt = pl.num_programs(1)
    step = pl.program_id(0) * nt + pl.program_id(1)
    nsteps = pl.num_programs(0) * nt
    slot = step % 2

    def copies(step_, slot_, start_ref, win):
        return [pltpu.make_async_copy(
            y_hbm.at[pl.ds(pl.multiple_of(start_ref[step_ * N_EXPERTS + e], SUBLANES), win), :],
            stage.at[slot_, pl.ds(e * win, win), :], sem.at[slot_]) for e in range(N_EXPERTS)]

    def by_width(step_, fn):
        @pl.when(wide_ref[step_] == 0)
        def _():
            fn(start_n_ref, WINDOW_NARROW, 0)

        @pl.when(wide_ref[step_] != 0)
        def _():
            fn(start_w_ref, WINDOW_WIDE, 1)

    def start(step_, slot_):
        by_width(step_, lambda ref, win, _: [cp.start() for cp in copies(step_, slot_, ref, win)])

    @pl.when(step == 0)
    def _():
        start(0, 0)

    @pl.when(step + 1 < nsteps)
    def _():
        start(step + 1, 1 - slot)

    gate = gsel_ref[0]
    cex = cex_ref[0]
    tile = xn_ref.shape[1]

    def reduce(start_ref, win, delta_row):
        for cp in copies(step, slot, start_ref, win):
            cp.wait()
        rpos = cex + delta_ref[0, 0, delta_row:delta_row + 1, :]
        acc = None
        for c0 in range(0, N_EXPERTS * win, LANES):
            lane_r = c0 + lax.broadcasted_iota(jnp.int32, (tile, LANES), 1)
            q = jnp.zeros((tile, LANES), F32)
            for e in range(c0 // win, min(N_EXPERTS - 1, (c0 + LANES - 1) // win) + 1):
                q = jnp.where(rpos[:, e:e + 1] == lane_r, gate[:, e:e + 1], q)
            part = jnp.dot(q.astype(BF16), stage[slot, c0:c0 + LANES, :].astype(BF16),
                           preferred_element_type=F32)
            acc = part if acc is None else acc + part
        o_ref[0] = xn_ref[0] + mod_ref[0, 5:6, :] * acc

    by_width(step, reduce)


def _combine(wide, start_n, start_w, y_flat, xn, mod, cex, gsel, delta, n_ctx_tiles):
    b, tl, d = xn.shape
    nt = tl // MOE_TILE
    tok = lambda w_: pl.BlockSpec((1, MOE_TILE, w_), lambda bb, i, *_: (bb, i, 0))
    return pl.pallas_call(
        _combine_kernel,
        grid_spec=pltpu.PrefetchScalarGridSpec(
            num_scalar_prefetch=3,
            grid=(b, nt),
            in_specs=[pl.BlockSpec(memory_space=pl.ANY), tok(d),
                      pl.BlockSpec((1, 6, d), lambda bb, i, *_: (jnp.where(i < n_ctx_tiles, 0, 1 + bb), 0, 0)),
                      tok(LANES), tok(LANES),
                      pl.BlockSpec((1, 1, SUBLANES, LANES), lambda bb, i, *_: (bb, i, 0, 0))],
            out_specs=tok(d),
            scratch_shapes=[pltpu.VMEM((2, N_EXPERTS * WINDOW_WIDE, d), F32),
                            pltpu.SemaphoreType.DMA((2,))]),
        out_shape=jax.ShapeDtypeStruct((b, tl, d), F32),
        compiler_params=_params("arbitrary", "arbitrary"),
        name="moe_combine",
    )(wide, start_n, start_w, y_flat, xn, mod, cex, gsel, delta)


def _moe(h2, aff, xn, mod, groups, w_gate, w_up, w_down, layer):
    b, tl, d = h2.shape
    ne = w_gate.shape[1]
    caps = [max(1, CAP_FACTOR * n // ne) for _, n in groups]
    nrows = sum(b * cap for cap in caps)
    assert all(n % MOE_TILE == 0 for _, n in groups) and all(cap % SUBLANES == 0 for cap in caps)
    gsel, cex, *idxs = _route(aff, [(row0, n, cap) for (row0, n), cap in zip(groups, caps)])
    idx_parts, src_parts, cnt_parts = [], [], []
    base = 0
    batch = jnp.arange(b, dtype=jnp.int32)
    for (row0, n), cap, idx in zip(groups, caps, idxs):
        rows = idx + (batch * tl + row0)[:, None, None]
        idx_parts.append(jnp.swapaxes(rows, 0, 1).reshape(ne, b * cap))
        s0 = cex[:, row0:row0 + n:MOE_TILE, :ne]
        ends = jnp.concatenate([s0[:, 1:], jnp.full((b, 1, ne), cap, jnp.int32)], axis=1)
        first = (base + batch * cap)[:, None, None] + (jnp.arange(ne, dtype=jnp.int32) * nrows)[None, None, :]
        src_parts.append(jnp.stack([s0 + first, jnp.broadcast_to(first, s0.shape)], axis=0))
        cnt_parts.append(ends - s0)
        base += b * cap
    cat = lambda parts, axis: parts[0] if len(parts) == 1 else jnp.concatenate(parts, axis=axis)
    src, first = cat(src_parts, 2)
    cnt = cat(cnt_parts, 1)
    wide = (cnt > WINDOW_NARROW - (SUBLANES - 1)).any(axis=-1).astype(jnp.int32)
    starts, deltas = [], []
    for win in (WINDOW_NARROW, WINDOW_WIDE):
        st = jnp.minimum(src // SUBLANES * SUBLANES, ne * nrows - win)
        starts.append(st.reshape(-1))
        deltas.append(jnp.arange(ne, dtype=jnp.int32) * win + first - st)
    delta = jnp.pad(jnp.stack(deltas, axis=2), ((0, 0), (0, 0), (0, SUBLANES - 2), (0, LANES - ne)))
    y = _expert_ffn(cat(idx_parts, 1).reshape(-1), h2.reshape(b * tl, d), w_gate, w_up, w_down, layer, nrows)
    return _combine(wide.reshape(-1), starts[0], starts[1], y.reshape(ne * nrows, d), xn, mod, cex, gsel,
                    delta, groups[0][1] // MOE_TILE if len(groups) > 1 else 0)


def _rope_tables(s_len, n_ctx, head_dim):
    quarter = head_dim // 4
    t = jnp.arange(s_len)
    row = (t // GRID_W).astype(F32)
    col = (t % GRID_W).astype(F32)
    inv = ROPE_THETA ** (-jnp.arange(quarter, dtype=F32) / quarter)
    ar, ac = row[:, None] * inv, col[:, None] * inv
    cos = jnp.concatenate([jnp.cos(ar), jnp.cos(ar), jnp.cos(ac), jnp.cos(ac)], axis=1)
    sin = jnp.concatenate([-jnp.sin(ar), jnp.sin(ar), -jnp.sin(ac), jnp.sin(ac)], axis=1)
    reps = LANES // head_dim
    cos, sin = jnp.tile(cos, (1, reps)), jnp.tile(sin, (1, reps))
    cos = jnp.concatenate([jnp.ones((n_ctx, LANES), F32), cos], axis=0)
    sin = jnp.concatenate([jnp.zeros((n_ctx, LANES), F32), sin], axis=0)
    return cos, sin


def _even_weight(w):
    nq, nkv = HQ_A * DH_A, HKV_A * DH_A
    q = w[:, :nq]
    dup = lambda m: jnp.concatenate(
        [m[:, DH_A * (h // 2):DH_A * (h // 2 + 1)] for h in range(2 * HKV_A)], axis=1)
    k = dup(w[:, nq:nq + nkv])
    v = dup(w[:, nq + nkv:nq + 2 * nkv])
    o = nq + 2 * nkv
    nconv, nz = 3 * H_B * DK_B, H_B * DK_B
    conv = w[:, o:o + nconv]
    z = w[:, o + nconv:o + nconv + nz]
    ab = w[:, o + nconv + nz:]
    ab = jnp.pad(ab, ((0, 0), (0, LANES - ab.shape[1])))
    return jnp.concatenate([q, k, v, conv, z, ab], axis=1).astype(BF16)


def _lane_vec(v):
    v = v.reshape(1, -1)
    return jnp.pad(v, ((0, 0), (0, LANES - v.shape[1])))


def kernel(x, c, ctx, c_ctx, w_mod, b_mod, norm_mix, norm_ffn, w_in_ab, w_out_ab, qnorm_a, knorm_a,
           sink_a, conv_b, a_log_b, dt_bias_b, onorm_b, w_in_c, w_out_c, qnorm_c, knorm_c,
           w_router, w_gate, w_up, w_down):
    b, s_len, d = x.shape
    n_ctx = ctx.shape[1]
    depth = w_mod.shape[0]
    assert b + 1 <= SUBLANES and n_ctx % ROW_TILE == 0 and s_len % ROW_TILE == 0
    t_all = n_ctx + s_len

    cvec = jnp.concatenate([c_ctx[None], c, jnp.zeros((SUBLANES - 1 - b, d), F32)], axis=0)
    mod = _modulation(cvec, w_mod, b_mod).reshape(depth, SUBLANES, 6, d)
    cos_a, sin_a = _rope_tables(s_len, n_ctx, DH_A)
    cos_c, sin_c = _rope_tables(s_len, n_ctx, DH_C)
    nq_a = HQ_A * DH_A
    seg = jnp.arange(nq_a) // DH_A
    ones_bd = (seg[:, None] == seg[None, :]).astype(BF16)

    xs = jnp.concatenate([ctx, x], axis=1)
    for i in range(depth):
        last = i == depth - 1
        j = i // 2
        gain1 = norm_mix[i].reshape(1, d)
        gain2 = norm_ffn[i].reshape(1, d)
        w_r = jnp.pad(w_router[i], ((0, 0), (0, LANES - N_EXPERTS)))
        if i % 2 == 0:
            q, k, v, pc, z, ab = _inproj_even(
                xs, mod[i], gain1, _even_weight(w_in_ab[j]), cos_a, sin_a,
                jnp.tile(qnorm_a[j], HQ_A).reshape(1, -1), jnp.tile(knorm_a[j], 2 * HKV_A).reshape(1, -1),
                ones_bd, n_ctx)
            oa = _attn_a(sink_a[j], q, k, v, n_ctx)
            qb, kb, vb, gb = _gdn_prep(pc, conv_b[j], ab, _lane_vec(a_log_b[j]), _lane_vec(dt_bias_b[j]),
                                       n_ctx)
            u, w, qg, a, kdt, ge = _gdn_chunk(qb, kb, vb, gb)
            o_fwd, o_bwd = _gdn_scan(u, w, qg, a, kdt, ge, n_ctx)
            tile0 = n_ctx // ROW_TILE if last else 0
            xn, h2, aff = _outproj_call(
                _outproj_even_kernel, "outproj_even",
                [(oa, None, tile0), (o_fwd, None, tile0), (o_bwd, None, tile0), (z, None, tile0)], xs, mod[i],
                [w_out_ab[j].astype(BF16), onorm_b[j].reshape(1, -1), gain2, w_r], n_ctx, tile0)
        else:
            q, k, v = _inproj_odd(xs, mod[i], gain1, w_in_c[j].astype(BF16), cos_c, sin_c,
                                  qnorm_c[j].reshape(1, -1), knorm_c[j].reshape(1, -1), n_ctx)
            if last:
                o = _attn_c(q, k, v, n_ctx)
                tile0 = n_ctx // ROW_TILE
            else:
                raise NotImplementedError("context queries of a non-final odd layer")
            xn, h2, aff = _outproj_call(
                _outproj_odd_kernel, "outproj_odd", [(o, None, 0)], xs, mod[i],
                [w_out_c[j].astype(BF16), gain2, w_r], n_ctx, tile0)
        groups = [(0, s_len)] if last else [(0, n_ctx), (n_ctx, s_len)]
        xs = _moe(h2, aff, xn, mod[i], groups, w_gate, w_up, w_down, i)
    return xs
```

```python
import functools

import jax
import jax.numpy as jnp
from jax import lax
from jax.experimental import pallas as pl
from jax.experimental.pallas import tpu as pltpu

F32 = jnp.float32
BF16 = jnp.bfloat16
HIGHEST = lax.Precision.HIGHEST

GRID_W = 64
NORM_EPS = 1e-6
ROPE_THETA = 10000.0
NEG_INF = -1e30
HQ_A, HKV_A, DH_A, WINDOW = 8, 2, 64, 128
H_B, DK_B, CONV_K, CHUNK = 4, 128, 5, 64
INV_BASE = 8
HQ_C, HKV_C, DH_C = 8, 2, 128
N_EXPERTS, CAP_FACTOR = 16, 2

LANES = 128
SUBLANES = 8
VMEM_LIMIT = 56 * 2 ** 20

ROW_TILE = 256
Q_TILE = 128
KV_TILE_MAX = 1408
SOFTMAX_ROWS = 64
LOG2_E = 1.4426950408889634
FF_TILE = 512
FFN_ROW_CHUNKS = 2
MOE_TILE = 128
WINDOW_NARROW = 40
WINDOW_WIDE = MOE_TILE + SUBLANES
GATHER_UNROLL = 8


def _params(*sem):
    return pltpu.CompilerParams(dimension_semantics=sem, vmem_limit_bytes=VMEM_LIMIT)


def _silu(x):
    return x * (1.0 / (1.0 + jnp.exp(-x)))


def _sigmoid(x):
    return 1.0 / (1.0 + jnp.exp(-x))


def _norm_mod(x, gain, shift, scale):
    ms = jnp.mean(x * x, axis=-1, keepdims=True)
    y = x * lax.rsqrt(ms + NORM_EPS) * gain
    return y * (1.0 + scale) + shift


def _segment_mean_square(p, ones_bd, seg):
    sq = p * p
    hi = sq.astype(BF16)
    lo = (sq - hi.astype(F32)).astype(BF16)
    s = (jnp.dot(hi, ones_bd, preferred_element_type=F32)
         + jnp.dot(lo, ones_bd, preferred_element_type=F32))
    return s * (1.0 / seg)


def _rope(x, cos, sin_signed, dist):
    n = x.shape[-1]
    lane = lax.broadcasted_iota(jnp.int32, x.shape, 1)
    up = pltpu.roll(x, n - dist, 1)
    dn = pltpu.roll(x, dist, 1)
    partner = jnp.where((lane & dist) == 0, up, dn)
    return x * cos + partner * sin_signed


def _mod_kernel(c_ref, w_ref, b_ref, o_ref):
    s = _silu(c_ref[...])
    o_ref[0] = jnp.dot(s, w_ref[0], precision=HIGHEST, preferred_element_type=F32) + b_ref[0]


def _modulation(cvec, w_mod, b_mod):
    depth, d, n6 = w_mod.shape
    tn = 1536
    return pl.pallas_call(
        _mod_kernel,
        grid=(depth, n6 // tn),
        in_specs=[pl.BlockSpec((SUBLANES, d), lambda l, j: (0, 0)),
                  pl.BlockSpec((1, d, tn), lambda l, j: (l, 0, j)),
                  pl.BlockSpec((1, 1, tn), lambda l, j: (l, 0, j))],
        out_specs=pl.BlockSpec((1, SUBLANES, tn), lambda l, j: (l, 0, j)),
        out_shape=jax.ShapeDtypeStruct((depth, SUBLANES, n6), F32),
        compiler_params=_params("parallel", "parallel"),
        name="modulation",
    )(cvec, w_mod, b_mod.reshape(depth, 1, n6))


def _mod_spec(n_ctx_tiles, tile0=0):
    return lambda b, i: (jnp.where(i + tile0 < n_ctx_tiles, 0, 1 + b), 0, 0)


def _inproj_even_kernel(x_ref, mod_ref, gain_ref, w_ref, cos_ref, sin_ref, qn_ref, kn_ref, ones_ref,
                        q_ref, k_ref, v_ref, pc_ref, z_ref, ab_ref):
    h = _norm_mod(x_ref[0], gain_ref[...], mod_ref[0, 0:1, :], mod_ref[0, 1:2, :]).astype(BF16)

    def proj(lo, hi):
        return jnp.dot(h, w_ref[:, lo:hi], preferred_element_type=F32)

    c = cos_ref[...]
    s = sin_ref[...]
    nq, nk = HQ_A * DH_A, 2 * HKV_A * DH_A
    q = proj(0, nq)
    q = q * lax.rsqrt(_segment_mean_square(q, ones_ref[...], DH_A) + NORM_EPS) * qn_ref[...]
    q = _rope(q, jnp.concatenate([c] * (nq // LANES), axis=1),
              jnp.concatenate([s] * (nq // LANES), axis=1), DH_A // 4)
    q_ref[0] = (q * (DH_A ** -0.5 * LOG2_E)).astype(BF16)
    k = proj(nq, nq + nk)
    k = k * lax.rsqrt(_segment_mean_square(k, ones_ref[0:nk, 0:nk], DH_A) + NORM_EPS) * kn_ref[...]
    k = _rope(k, jnp.concatenate([c] * (nk // LANES), axis=1),
              jnp.concatenate([s] * (nk // LANES), axis=1), DH_A // 4)
    k_ref[0] = k.astype(BF16)
    o = nq + nk
    v_ref[0] = proj(o, o + nk).astype(BF16)
    o += nk
    nconv = pc_ref.shape[2]
    pc_ref[0] = proj(o, o + nconv)
    o += nconv
    nz = z_ref.shape[2]
    z_ref[0] = proj(o, o + nz)
    o += nz
    ab_ref[0] = proj(o, o + LANES)


def _inproj_even(xs, mod, gain, w, cos, sin, qn, kn, ones_bd, n_ctx):
    b, t, d = xs.shape
    tm = ROW_TILE
    nq, nk = HQ_A * DH_A, 2 * HKV_A * DH_A
    nconv, nz = 3 * H_B * DK_B, H_B * DK_B
    row = lambda w_: pl.BlockSpec((1, tm, w_), lambda bb, i: (bb, i, 0))
    const = lambda a: pl.BlockSpec(a.shape, lambda bb, i: (0,) * a.ndim)
    return pl.pallas_call(
        _inproj_even_kernel,
        grid=(b, t // tm),
        in_specs=[row(d), pl.BlockSpec((1, 6, d), _mod_spec(n_ctx // tm)), const(gain), const(w),
                  pl.BlockSpec((tm, LANES), lambda bb, i: (i, 0)),
                  pl.BlockSpec((tm, LANES), lambda bb, i: (i, 0)),
                  const(qn), const(kn), const(ones_bd)],
        out_specs=[row(nq), row(nk), row(nk), row(nconv), row(nz), row(LANES)],
        out_shape=[jax.ShapeDtypeStruct((b, t, nq), BF16), jax.ShapeDtypeStruct((b, t, nk), BF16),
                   jax.ShapeDtypeStruct((b, t, nk), BF16), jax.ShapeDtypeStruct((b, t, nconv), F32),
                   jax.ShapeDtypeStruct((b, t, nz), F32), jax.ShapeDtypeStruct((b, t, LANES), F32)],
        compiler_params=_params("parallel", "parallel"),
        name="inproj_even",
    )(xs, mod, gain, w, cos, sin, qn, kn, ones_bd)


def _attn_a_kernel(sink_ref, q_ref, k_ref, v_ref, o_ref, s_sc, p_sc, r_sc, *, n_ctx, t_all):
    i = pl.program_id(1)
    g_heads = HQ_A // HKV_A
    band = 3 * Q_TILE
    n = i - n_ctx // Q_TILE
    start = jnp.clip(n_ctx + (n - 1) * Q_TILE, 0, t_all - band)
    start = pl.multiple_of(start, Q_TILE)
    q = q_ref[0]
    rows = g_heads * Q_TILE
    lane = lax.broadcasted_iota(jnp.int32, (Q_TILE, LANES), 1)
    kj = lax.broadcasted_iota(jnp.int32, (rows, band), 1)
    rel = kj - (lax.broadcasted_iota(jnp.int32, (rows, band), 0) & (Q_TILE - 1))
    first = start - n_ctx
    off = first - n * Q_TILE
    valid = (n >= 0) & (rel >= -WINDOW - off) & (rel <= WINDOW - off) & (kj >= -first)
    dims = (((1,), (1,)), ((), ()))
    nkeys = n_ctx + band
    for h in range(HKV_A):
        cols = slice(LANES * h, LANES * (h + 1))
        parts = []
        for g in range(g_heads):
            j = g_heads * h + g
            tile = q[:, LANES * (j // 2):LANES * (j // 2 + 1)]
            keep = (lane >= DH_A * (j % 2)) & (lane < DH_A * (j % 2 + 1))
            parts.append(jnp.where(keep, tile, jnp.zeros_like(tile)))
        qs = jnp.concatenate(parts, axis=0)
        s_sc[h, :, 0:n_ctx] = lax.dot_general(qs, k_ref[0, 0:n_ctx, cols], dims,
                                              preferred_element_type=F32)
        s_b = lax.dot_general(qs, k_ref[0, pl.ds(start, band), cols], dims, preferred_element_type=F32)
        s_sc[h, :, n_ctx:nkeys] = jnp.where(valid, s_b, NEG_INF)
    for h in range(HKV_A):
        for rb in range(rows // SOFTMAX_ROWS):
            blk = slice(SOFTMAX_ROWS * rb, SOFTMAX_ROWS * (rb + 1))
            sink = sink_ref[g_heads * h + SOFTMAX_ROWS * rb // Q_TILE] * LOG2_E
            tiles = [slice(LANES * t, LANES * (t + 1)) for t in range(nkeys // LANES)]
            mx = s_sc[h, blk, tiles[0]]
            for t in tiles[1:]:
                mx = jnp.maximum(mx, s_sc[h, blk, t])
            m = jnp.maximum(mx.max(axis=1, keepdims=True), sink)
            m_b = jnp.broadcast_to(m, (SOFTMAX_ROWS, LANES))
            l_run = jnp.zeros((SOFTMAX_ROWS, LANES), F32)
            for t in tiles:
                p = jnp.exp2(s_sc[h, blk, t] - m_b)
                l_run = l_run + p
                p_sc[h, blk, t] = p.astype(BF16)
            den = l_run.sum(axis=1, keepdims=True) + jnp.exp2(sink - m)
            r_sc[h, blk, :] = jnp.broadcast_to(1.0 / den, (SOFTMAX_ROWS, LANES))
    for h in range(HKV_A):
        cols = slice(LANES * h, LANES * (h + 1))
        o = (jnp.dot(p_sc[h, :, 0:n_ctx], v_ref[0, 0:n_ctx, cols], preferred_element_type=F32)
             + jnp.dot(p_sc[h, :, n_ctx:nkeys], v_ref[0, pl.ds(start, band), cols],
                       preferred_element_type=F32)) * r_sc[h]
        for pair in range(g_heads // 2):
            lo = o[(2 * pair) * Q_TILE:(2 * pair + 1) * Q_TILE]
            hi = o[(2 * pair + 1) * Q_TILE:(2 * pair + 2) * Q_TILE]
            c0 = LANES * (g_heads // 2 * h + pair)
            o_ref[0, :, c0:c0 + LANES] = jnp.where(lane < DH_A, lo, hi).astype(BF16)


def _attn_a(sink, q, k, v, n_ctx):
    b, t, nq = q.shape
    nk = k.shape[2]
    rows, nkeys = HQ_A // HKV_A * Q_TILE, n_ctx + 3 * Q_TILE
    return pl.pallas_call(
        functools.partial(_attn_a_kernel, n_ctx=n_ctx, t_all=t),
        grid=(b, t // Q_TILE),
        in_specs=[pl.BlockSpec(memory_space=pltpu.SMEM),
                  pl.BlockSpec((1, Q_TILE, nq), lambda bb, i: (bb, i, 0)),
                  pl.BlockSpec((1, t, nk), lambda bb, i: (bb, 0, 0)),
                  pl.BlockSpec((1, t, nk), lambda bb, i: (bb, 0, 0))],
        out_specs=pl.BlockSpec((1, Q_TILE, nq), lambda bb, i: (bb, i, 0)),
        out_shape=jax.ShapeDtypeStruct((b, t, nq), BF16),
        scratch_shapes=[pltpu.VMEM((HKV_A, rows, nkeys), F32), pltpu.VMEM((HKV_A, rows, nkeys), BF16),
                        pltpu.VMEM((HKV_A, rows, LANES), F32)],
        compiler_params=_params("parallel", "arbitrary"),
        name="attn_window",
    )(sink, q, k, v)


def _gdn_prep_kernel(pc_ref, prev_ref, next_ref, cw_ref, ab_ref, alog_ref, dtb_ref,
                     q_ref, k_ref, v_ref, gb_ref, ext_sc, *, n_ctx, t_all):
    tm = pc_ref.shape[1]
    r0 = pl.program_id(1) * tm
    halo = SUBLANES
    prev_on = jnp.where((r0 == 0) | (r0 == n_ctx), 0.0, 1.0)
    next_on = jnp.where((r0 + tm == n_ctx) | (r0 + tm == t_all), 0.0, 1.0)
    ext_sc[0:halo, :] = prev_ref[0] * prev_on
    ext_sc[halo:halo + tm, :] = pc_ref[0]
    ext_sc[halo + tm:2 * halo + tm, :] = next_ref[0] * next_on
    nh = H_B * DK_B
    for grp, out_ref in enumerate((q_ref, k_ref, v_ref)):
        c0 = nh * grp
        acc = None
        for tap in range(CONV_K):
            off = halo - CONV_K // 2 + tap
            term = cw_ref[tap:tap + 1, c0:c0 + nh] * ext_sc[off:off + tm, c0:c0 + nh]
            acc = term if acc is None else acc + term
        y = _silu(acc)
        if grp == 2:
            out_ref[0] = y
            continue
        scale = DK_B ** -0.5 if grp == 0 else 1.0
        for h in range(H_B):
            yh = y[:, DK_B * h:DK_B * (h + 1)]
            inv = lax.rsqrt(jnp.sum(yh * yh, axis=-1, keepdims=True) + NORM_EPS)
            out_ref[0, :, DK_B * h:DK_B * (h + 1)] = yh * (inv * scale)
    ab = ab_ref[0]
    lane = lax.broadcasted_iota(jnp.int32, ab.shape, 1)
    xg = ab + dtb_ref[...]
    softplus = jnp.maximum(xg, 0.0) + jnp.log(1.0 + jnp.exp(-jnp.abs(xg)))
    g = -jnp.exp(alog_ref[...]) * softplus
    gb_ref[0] = jnp.where(lane < 2 * H_B, g, jnp.where(lane < 4 * H_B, _sigmoid(ab), 0.0))


def _gdn_prep(pc, conv_w, ab, alog, dtb, n_ctx):
    b, t, nconv = pc.shape
    tm = ROW_TILE
    nh = H_B * DK_B
    hb = tm // SUBLANES
    nblk = t // SUBLANES
    row = lambda w_: pl.BlockSpec((1, tm, w_), lambda bb, i: (bb, i, 0))
    const = lambda a: pl.BlockSpec(a.shape, lambda bb, i: (0,) * a.ndim)
    return pl.pallas_call(
        functools.partial(_gdn_prep_kernel, n_ctx=n_ctx, t_all=t),
        grid=(b, t // tm),
        in_specs=[row(nconv),
                  pl.BlockSpec((1, SUBLANES, nconv), lambda bb, i: (bb, jnp.maximum(i * hb - 1, 0), 0)),
                  pl.BlockSpec((1, SUBLANES, nconv),
                               lambda bb, i: (bb, jnp.minimum((i + 1) * hb, nblk - 1), 0)),
                  const(conv_w), row(LANES), const(alog), const(dtb)],
        out_specs=[row(nh), row(nh), row(nh), row(LANES)],
        out_shape=[jax.ShapeDtypeStruct((b, t, nh), F32)] * 3 + [jax.ShapeDtypeStruct((b, t, LANES), F32)],
        scratch_shapes=[pltpu.VMEM((tm + 2 * SUBLANES, nconv), F32)],
        compiler_params=_params("parallel", "parallel"),
        name="gdn_prep",
    )(pc, pc, pc, conv_w, ab, alog, dtb)


def _gdn_chunk_kernel(q_ref, k_ref, v_ref, gb_ref, u_ref, w_ref, qg_ref, a_ref, kdt_ref, ge_ref):
    c = CHUNK
    nchunks = q_ref.shape[1] // c
    r_i = lax.broadcasted_iota(jnp.int32, (c, c), 0)
    c_i = lax.broadcasted_iota(jnp.int32, (c, c), 1)
    tri_l = (r_i >= c_i).astype(F32)
    tri_u = (r_i <= c_i).astype(F32)
    row2 = lax.broadcasted_iota(jnp.int32, (c, LANES), 0)
    col2 = lax.broadcasted_iota(jnp.int32, (c, LANES), 1)
    colm = col2 & (c - 1)
    lane8 = lax.broadcasted_iota(jnp.int32, (1, LANES), 1)
    half_of = [(col2 // c) == (h % 2) for h in range(H_B)]
    eye_stack = jnp.concatenate(
        [jnp.where(half_of[h] & (row2 == colm), 1.0, 0.0) for h in range(H_B)], axis=0)
    stack = lambda m: jnp.concatenate([jnp.where(m, 1.0, 0.0)] * H_B, axis=0)
    base_mask = stack((row2 // INV_BASE) == (colm // INV_BASE))
    level_masks = [[], []]
    size = INV_BASE
    while size < c:
        same = (row2 // (2 * size)) == (colm // (2 * size))
        r_hi, c_hi = (row2 // size) % 2 == 1, (colm // size) % 2 == 1
        level_masks[0].append(stack(same & r_hi & ~c_hi))
        level_masks[1].append(stack(same & ~r_hi & c_hi))
        size *= 2

    def bmm(ls, rs):
        rcat = jnp.concatenate([rs[0:2 * c], rs[2 * c:4 * c]], axis=1).astype(BF16)
        full = jnp.dot(ls.astype(BF16), rcat, preferred_element_type=F32)
        return jnp.concatenate([full[0:2 * c, 0:LANES], full[2 * c:4 * c, LANES:2 * LANES]], axis=0)

    def wide(t_stack, mats):
        rv = jnp.concatenate([jnp.concatenate(mats[0:2], axis=0),
                              jnp.concatenate(mats[2:4], axis=0)], axis=1).astype(BF16)
        full = jnp.dot(t_stack.astype(BF16), rv, preferred_element_type=F32)
        return [full[c * h:c * (h + 1), LANES * (h // 2):LANES * (h // 2 + 1)] for h in range(H_B)]

    dims = (((1,), (1,)), ((), ()))
    head = lambda ref, rows, h: ref[0, rows, DK_B * h:DK_B * (h + 1)]
    combos = []
    for ci in range(nchunks):
        rows = slice(c * ci, c * (ci + 1))
        gb = gb_ref[0, rows, :]
        gc = jnp.where(lane8 < H_B,
                       jnp.dot(tri_l, gb, precision=HIGHEST, preferred_element_type=F32),
                       jnp.dot(tri_u, gb, precision=HIGHEST, preferred_element_type=F32))
        gc_t = gc.T
        eg = jnp.exp(gc)
        g_last = jnp.where(lane8 < H_B, gc[c - 1:c, :], gc[0:1, :])
        ge_ref[0, ci] = jnp.broadcast_to(jnp.exp(g_last), (SUBLANES, LANES))
        ek = jnp.exp(g_last - gc)
        raw = []
        for h in range(H_B):
            k_h = head(k_ref, rows, h)
            kq = jnp.concatenate([k_h, head(q_ref, rows, h)], axis=0).astype(BF16)
            kk = jnp.concatenate([k_h, k_h], axis=0).astype(BF16)
            raw.append(lax.dot_general(kq, kk, dims, preferred_element_type=F32))
        for d in range(2):
            keep = (row2 >= colm) if d == 0 else (row2 <= colm)
            strict = (row2 > colm) if d == 0 else (row2 < colm)
            a_blocks, aqk, betas, egs, eks = [], [], [], [], []
            for h in range(H_B):
                idx = H_B * d + h
                g_col = gc[:, idx:idx + 1]
                g_row = jnp.concatenate([gc_t[idx:idx + 1, :]] * 2, axis=1)
                decay = jnp.where(keep, jnp.exp(jnp.where(keep, g_col - g_row, 0.0)), 0.0)
                beta = gb[:, 2 * H_B + idx:2 * H_B + idx + 1]
                betas.append(beta)
                egs.append(eg[:, idx:idx + 1])
                eks.append(ek[:, idx:idx + 1])
                a_blocks.append(jnp.where(strict & half_of[h], beta * raw[h][0:c] * decay, 0.0))
                aqk.append(raw[h][c:2 * c] * decay)
            combos.append(dict(ci=ci, d=d, rows=rows, a=jnp.concatenate(a_blocks, axis=0),
                               aqk=aqk, betas=betas, egs=egs, eks=eks))

    xs = [-cb["a"] * base_mask for cb in combos]
    ts = [eye_stack + x for x in xs]
    ps = [bmm(x, x) for x in xs]
    ts = [t + bmm(t, p) for t, p in zip(ts, ps)]
    ps = [bmm(p, p) for p in ps]
    ts = [t + bmm(t, p) for t, p in zip(ts, ps)]
    for lvl in range(len(level_masks[0])):
        mids = [bmm(t, cb["a"] * level_masks[cb["d"]][lvl]) for t, cb in zip(ts, combos)]
        ts = [t - bmm(m, t) for t, m in zip(ts, mids)]

    for t_inv, cb in zip(ts, combos):
        ci, d, rows = cb["ci"], cb["d"], cb["rows"]
        qs = [head(q_ref, rows, h) for h in range(H_B)]
        ks = [head(k_ref, rows, h) for h in range(H_B)]
        us = wide(t_inv, [head(v_ref, rows, h) * cb["betas"][h] for h in range(H_B)])
        ws = wide(t_inv, [ks[h] * (cb["betas"][h] * cb["egs"][h]) for h in range(H_B)])
        for h in range(H_B):
            cols = slice(DK_B * h, DK_B * (h + 1))
            u_ref[d, 0, rows, cols] = us[h]
            w_ref[d, 0, rows, cols] = ws[h].astype(BF16)
            qg_ref[d, 0, rows, cols] = (qs[h] * cb["egs"][h]).astype(BF16)
        for pair in range(H_B // 2):
            h0, h1 = 2 * pair, 2 * pair + 1
            a_ref[d, 0, rows, LANES * pair:LANES * (pair + 1)] = jnp.where(
                col2 < c, cb["aqk"][h0], cb["aqk"][h1]).astype(BF16)
            kd0 = (ks[h0] * cb["eks"][h0]).T
            kd1 = (ks[h1] * cb["eks"][h1]).T
            kdt_ref[d, 0, ci, :, LANES * pair:LANES * (pair + 1)] = jnp.concatenate(
                [kd0, kd1], axis=1).astype(BF16)


def _gdn_chunk(qb, kb, vb, gb):
    b, t, nh = qb.shape
    tm = ROW_TILE
    cps = tm // CHUNK
    nck = t // CHUNK
    row = lambda w_: pl.BlockSpec((1, tm, w_), lambda bb, i: (bb, i, 0))
    drow = lambda w_: pl.BlockSpec((2, 1, tm, w_), lambda bb, i: (0, bb, i, 0))
    return pl.pallas_call(
        _gdn_chunk_kernel,
        grid=(b, t // tm),
        in_specs=[row(nh), row(nh), row(nh), row(LANES)],
        out_specs=[drow(nh), drow(nh), drow(nh), drow(nh // 2),
                   pl.BlockSpec((2, 1, cps, DK_B, nh // 2), lambda bb, i: (0, bb, i, 0, 0)),
                   pl.BlockSpec((1, cps, SUBLANES, LANES), lambda bb, i: (bb, i, 0, 0))],
        out_shape=[jax.ShapeDtypeStruct((2, b, t, nh), F32), jax.ShapeDtypeStruct((2, b, t, nh), BF16),
                   jax.ShapeDtypeStruct((2, b, t, nh), BF16),
                   jax.ShapeDtypeStruct((2, b, t, nh // 2), BF16),
                   jax.ShapeDtypeStruct((2, b, nck, DK_B, nh // 2), BF16),
                   jax.ShapeDtypeStruct((b, nck, SUBLANES, LANES), F32)],
        compiler_params=_params("parallel", "parallel"),
        name="gdn_chunk",
    )(qb, kb, vb, gb)


def _gdn_scan_kernel(*refs):
    ins = (refs[0:6], refs[6:12])
    outs, s_sc = refs[12:14], refs[14]
    nb = outs[0].shape[0]

    @pl.when(pl.program_id(0) == 0)
    def _():
        s_sc[...] = jnp.zeros_like(s_sc)

    zero = jnp.zeros((CHUNK, DK_B), BF16)
    chains = [(d, b, h) for d in range(2) for b in range(nb) for h in range(H_B)]
    cols = lambda h: slice(DK_B * h, DK_B * (h + 1))
    pair = lambda h: slice(LANES * (h // 2), LANES * (h // 2 + 1))
    states = [s_sc[d, b, h] for d, b, h in chains]
    rs = [jnp.dot(jnp.concatenate([ins[d][1][0, b, :, cols(h)], ins[d][2][0, b, :, cols(h)]], axis=0),
                  s.astype(BF16), preferred_element_type=F32) for (d, b, h), s in zip(chains, states)]
    v_pads = []
    for (d, b, h), r in zip(chains, rs):
        v_new = (ins[d][0][0, b, :, cols(h)] - r[0:CHUNK]).astype(BF16)
        v_pads.append(jnp.concatenate([v_new, zero] if h % 2 == 0 else [zero, v_new], axis=0))
    for (d, b, h), s, r, v_pad in zip(chains, states, rs, v_pads):
        g_end = ins[d][5][b, 0, 0:1, H_B * d + h:H_B * d + h + 1]
        s_sc[d, b, h] = s * g_end + jnp.dot(ins[d][4][0, b, 0, :, pair(h)], v_pad,
                                            preferred_element_type=F32)
    for (d, b, h), r, v_pad in zip(chains, rs, v_pads):
        outs[d][b, :, cols(h)] = r[CHUNK:2 * CHUNK] + jnp.dot(
            ins[d][3][0, b, :, pair(h)], v_pad, preferred_element_type=F32)


def _gdn_scan(u, w, qg, a, kdt, ge, n_ctx):
    _, b, t, nh = u.shape
    nck = t // CHUNK
    ncc = n_ctx // CHUNK
    chunk = (lambda s: s,
             lambda s: jnp.where(s < ncc, ncc - 1 - s, nck - 1 - (s - ncc)))

    def specs(d):
        drow = lambda w_: pl.BlockSpec((1, b, CHUNK, w_), lambda s: (d, 0, chunk[d](s), 0))
        return [drow(nh), drow(nh), drow(nh), drow(nh // 2),
                pl.BlockSpec((1, b, 1, DK_B, nh // 2), lambda s: (d, 0, chunk[d](s), 0, 0)),
                pl.BlockSpec((b, 1, SUBLANES, LANES), lambda s: (0, chunk[d](s), 0, 0))]

    return pl.pallas_call(
        _gdn_scan_kernel,
        grid=(nck,),
        in_specs=specs(0) + specs(1),
        out_specs=[pl.BlockSpec((b, CHUNK, nh), lambda s, d=d: (0, chunk[d](s), 0)) for d in range(2)],
        out_shape=[jax.ShapeDtypeStruct((b, t, nh), F32)] * 2,
        scratch_shapes=[pltpu.VMEM((2, b, H_B, DK_B, DK_B), F32)],
        compiler_params=_params("arbitrary"),
        name="gdn_scan",
    )(u, w, qg, a, kdt, ge, u, w, qg, a, kdt, ge)


def _residual_router(y, x_ref, mod_ref, g2_ref, wr_ref, xn_ref, h2_ref, aff_ref):
    xn = x_ref[0] + mod_ref[0, 2:3, :] * y
    xn_ref[0] = xn
    h2 = _norm_mod(xn, g2_ref[...], mod_ref[0, 3:4, :], mod_ref[0, 4:5, :])
    h2_ref[0] = h2
    h_hi = h2.astype(BF16)
    h_lo = (h2 - h_hi.astype(F32)).astype(BF16)
    logits = (jnp.dot(h_hi, wr_ref[0], preferred_element_type=F32)
              + jnp.dot(h_lo, wr_ref[0], preferred_element_type=F32)
              + jnp.dot(h_hi, wr_ref[1], preferred_element_type=F32))
    lane = lax.broadcasted_iota(jnp.int32, logits.shape, 1)
    logits = jnp.where(lane < N_EXPERTS, logits, NEG_INF)
    e = jnp.exp(logits - logits.max(axis=-1, keepdims=True))
    aff_ref[0] = e / e.sum(axis=-1, keepdims=True)


def _outproj_even_kernel(oa_ref, of_ref, ob_ref, z_ref, x_ref, mod_ref, w_ref, onorm_ref, g2_ref, wr_ref,
                         xn_ref, h2_ref, aff_ref):
    na = oa_ref.shape[2]
    y = jnp.dot(oa_ref[0], w_ref[0:na, :], preferred_element_type=F32)
    o = of_ref[0] + ob_ref[0]
    z = z_ref[0]
    for h in range(H_B):
        cols = slice(DK_B * h, DK_B * (h + 1))
        oh = o[:, cols]
        ms = jnp.mean(oh * oh, axis=-1, keepdims=True)
        yh = oh * lax.rsqrt(ms + NORM_EPS) * onorm_ref[...] * _silu(z[:, cols])
        y = y + jnp.dot(yh.astype(BF16), w_ref[na + DK_B * h:na + DK_B * (h + 1), :],
                        preferred_element_type=F32)
    _residual_router(y, x_ref, mod_ref, g2_ref, wr_ref, xn_ref, h2_ref, aff_ref)


def _outproj_odd_kernel(o_ref, x_ref, mod_ref, w_ref, g2_ref, wr_ref, xn_ref, h2_ref, aff_ref):
    y = jnp.dot(o_ref[0], w_ref[...], preferred_element_type=F32)
    _residual_router(y, x_ref, mod_ref, g2_ref, wr_ref, xn_ref, h2_ref, aff_ref)


def _outproj_call(kernel, name, acts, xs, mod, consts, n_ctx, tile0):
    b, t, d = xs.shape
    tm = ROW_TILE
    nt = t // tm - tile0
    specs = []
    for a, lead, off in acts:
        if lead is None:
            specs.append(pl.BlockSpec((1, tm, a.shape[-1]), lambda bb, i, off=off: (bb, i + off, 0)))
        else:
            specs.append(pl.BlockSpec((1, 1, tm, a.shape[-1]),
                                      lambda bb, i, lead=lead, off=off: (lead, bb, i + off, 0)))
    row = pl.BlockSpec((1, tm, d), lambda bb, i: (bb, i + tile0, 0))
    const = lambda a: pl.BlockSpec(a.shape, lambda bb, i: (0,) * a.ndim)
    mod_spec = pl.BlockSpec((1, 6, d), _mod_spec(n_ctx // tm, tile0))
    wout, rest = consts[0], consts[1:]
    return pl.pallas_call(
        kernel,
        grid=(b, nt),
        in_specs=specs + [row, mod_spec, const(wout)] + [const(a) for a in rest],
        out_specs=[pl.BlockSpec((1, tm, d), lambda bb, i: (bb, i, 0)),
                   pl.BlockSpec((1, tm, d), lambda bb, i: (bb, i, 0)),
                   pl.BlockSpec((1, tm, LANES), lambda bb, i: (bb, i, 0))],
        out_shape=[jax.ShapeDtypeStruct((b, nt * tm, d), F32), jax.ShapeDtypeStruct((b, nt * tm, d), F32),
                   jax.ShapeDtypeStruct((b, nt * tm, LANES), F32)],
        compiler_params=_params("parallel", "parallel"),
        name=name,
    )(*[a for a, _, _ in acts], xs, mod, wout, *rest)


def _inproj_odd_kernel(x_ref, mod_ref, gain_ref, w_ref, cos_ref, sin_ref, qn_ref, kn_ref,
                       q_ref, k_ref, v_ref):
    h = _norm_mod(x_ref[0], gain_ref[...], mod_ref[0, 0:1, :], mod_ref[0, 1:2, :]).astype(BF16)
    c = cos_ref[...]
    s = sin_ref[...]
    nq, nk = HQ_C * DH_C, HKV_C * DH_C

    def normed_heads(lo, nheads, gain_ref_, out_ref, scale):
        p = jnp.dot(h, w_ref[:, lo:lo + nheads * DH_C], preferred_element_type=F32)
        for hh in range(nheads):
            ph = p[:, DH_C * hh:DH_C * (hh + 1)]
            ms = jnp.mean(ph * ph, axis=-1, keepdims=True)
            ph = _rope(ph * lax.rsqrt(ms + NORM_EPS) * gain_ref_[...], c, s, DH_C // 4)
            out_ref[0, :, DH_C * hh:DH_C * (hh + 1)] = (ph * scale).astype(BF16)

    normed_heads(0, HQ_C, qn_ref, q_ref, DH_C ** -0.5 * LOG2_E)
    normed_heads(nq, HKV_C, kn_ref, k_ref, 1.0)
    v_ref[0] = jnp.dot(h, w_ref[:, nq + nk:nq + 2 * nk], preferred_element_type=F32).astype(BF16)


def _inproj_odd(xs, mod, gain, w, cos, sin, qn, kn, n_ctx):
    b, t, d = xs.shape
    tm = ROW_TILE
    nq, nk = HQ_C * DH_C, HKV_C * DH_C
    row = lambda w_: pl.BlockSpec((1, tm, w_), lambda bb, i: (bb, i, 0))
    const = lambda a: pl.BlockSpec(a.shape, lambda bb, i: (0,) * a.ndim)
    return pl.pallas_call(
        _inproj_odd_kernel,
        grid=(b, t // tm),
        in_specs=[row(d), pl.BlockSpec((1, 6, d), _mod_spec(n_ctx // tm)), const(gain), const(w),
                  pl.BlockSpec((tm, LANES), lambda bb, i: (i, 0)),
                  pl.BlockSpec((tm, LANES), lambda bb, i: (i, 0)),
                  const(qn), const(kn)],
        out_specs=[row(nq), row(nk), row(nk)],
        out_shape=[jax.ShapeDtypeStruct((b, t, nq), BF16), jax.ShapeDtypeStruct((b, t, nk), BF16),
                   jax.ShapeDtypeStruct((b, t, nk), BF16)],
        compiler_params=_params("parallel", "parallel"),
        name="inproj_odd",
    )(xs, mod, gain, w, cos, sin, qn, kn)


def _attn_c_kernel(q_ref, k_ref, v_ref, o_ref, s_sc, p_sc, m_sc, l_sc, a_sc, acc_sc):
    kv_tile = s_sc.shape[2]
    nchunk = k_ref.shape[1] // kv_tile
    g_heads = HQ_C // HKV_C
    ntile = kv_tile // LANES
    nrows = g_heads * Q_TILE
    dims = (((1,), (1,)), ((), ()))
    qs = [jnp.concatenate([q_ref[0, :, DH_C * (g_heads * h + g):DH_C * (g_heads * h + g + 1)]
                           for g in range(g_heads)], axis=0) for h in range(HKV_C)]
    m_sc[...] = jnp.full(m_sc.shape, NEG_INF, F32)
    l_sc[...] = jnp.zeros(l_sc.shape, F32)
    acc_sc[...] = jnp.zeros(acc_sc.shape, F32)

    def body(ci, carry):
        r0 = pl.multiple_of(ci * kv_tile, kv_tile)
        for h in range(HKV_C):
            s_sc[h] = lax.dot_general(qs[h], k_ref[0, pl.ds(r0, kv_tile), DH_C * h:DH_C * (h + 1)],
                                      dims, preferred_element_type=F32)
        for h in range(HKV_C):
            for rb in range(nrows // SOFTMAX_ROWS):
                rows = slice(SOFTMAX_ROWS * rb, SOFTMAX_ROWS * (rb + 1))
                mx = s_sc[h, rows, 0:LANES]
                for t in range(1, ntile):
                    mx = jnp.maximum(mx, s_sc[h, rows, LANES * t:LANES * (t + 1)])
                m_old = m_sc[h, rows, :]
                m_new = jnp.maximum(m_old, jnp.broadcast_to(mx.max(axis=1, keepdims=True),
                                                            (SOFTMAX_ROWS, LANES)))
                alpha = jnp.exp2(m_old - m_new)
                l_new = alpha * l_sc[h, rows, :]
                for t in range(ntile):
                    p = jnp.exp2(s_sc[h, rows, LANES * t:LANES * (t + 1)] - m_new)
                    l_new = l_new + p
                    p_sc[h, rows, LANES * t:LANES * (t + 1)] = p.astype(BF16)
                l_sc[h, rows, :] = l_new
                m_sc[h, rows, :] = m_new
                a_sc[h, rows, :] = alpha
        for h in range(HKV_C):
            pv = jnp.dot(p_sc[h], v_ref[0, pl.ds(r0, kv_tile), DH_C * h:DH_C * (h + 1)],
                         preferred_element_type=F32)
            acc_sc[h] = a_sc[h] * acc_sc[h] + pv
        return carry

    lax.fori_loop(0, nchunk, body, 0)
    for j in range(HQ_C):
        h, rows = j // g_heads, slice(Q_TILE * (j % g_heads), Q_TILE * (j % g_heads + 1))
        o = acc_sc[h, rows, :] / l_sc[h, rows, :].sum(axis=1, keepdims=True)
        o_ref[0, :, DH_C * j:DH_C * (j + 1)] = o.astype(BF16)


def _attn_c(q, k, v, n_ctx):
    b, t, nq = q.shape
    nk = k.shape[2]
    t0 = n_ctx // Q_TILE
    rows = HQ_C // HKV_C * Q_TILE
    kv_tile = max(w for w in range(LANES, KV_TILE_MAX + 1, LANES) if t % w == 0)
    return pl.pallas_call(
        _attn_c_kernel,
        grid=(b, t // Q_TILE - t0),
        in_specs=[pl.BlockSpec((1, Q_TILE, nq), lambda bb, i: (bb, i + t0, 0)),
                  pl.BlockSpec((1, t, nk), lambda bb, i: (bb, 0, 0)),
                  pl.BlockSpec((1, t, nk), lambda bb, i: (bb, 0, 0))],
        out_specs=pl.BlockSpec((1, Q_TILE, nq), lambda bb, i: (bb, i, 0)),
        out_shape=jax.ShapeDtypeStruct((b, t - n_ctx, nq), BF16),
        scratch_shapes=[pltpu.VMEM((HKV_C, rows, kv_tile), F32), pltpu.VMEM((HKV_C, rows, kv_tile), BF16),
                        pltpu.VMEM((HKV_C, rows, LANES), F32), pltpu.VMEM((HKV_C, rows, LANES), F32),
                        pltpu.VMEM((HKV_C, rows, LANES), F32), pltpu.VMEM((HKV_C, rows, DH_C), F32)],
        compiler_params=_params("parallel", "arbitrary"),
        name="attn_global",
    )(q, k, v)


def _route_group(aff_ref, row0, n, cap, gsel_ref, cex_ref, idx_ref, sel_sc, cin_sc):
    ngrp = LANES // N_EXPERTS
    rows = n // ngrp
    aff_rows = lambda g: aff_ref[0, row0 + rows * g:row0 + rows * (g + 1), :]
    packed = aff_rows(0)
    for g in range(1, ngrp):
        packed = packed + pltpu.roll(aff_rows(g), N_EXPERTS * g, 1)
    lane_p = lax.broadcasted_iota(jnp.int32, (rows, LANES), 1)
    tok = lax.broadcasted_iota(jnp.int32, (rows, LANES), 0) + rows * (lane_p // N_EXPERTS)

    def count(mask):
        c = jnp.broadcast_to(jnp.sum(jnp.where(mask, 1.0, 0.0), axis=0, keepdims=True), (SUBLANES, LANES))
        shift = N_EXPERTS
        while shift < LANES:
            c = c + pltpu.roll(c, shift, 1)
            shift *= 2
        return c[0:1, :]

    def bisect(steps, lo, hi, enough):
        def body(_, c):
            lo_, hi_ = c
            mid = lo_ + ((hi_ - lo_) >> 1)
            ok = enough(mid)
            return jnp.where(ok, mid, lo_), jnp.where(ok, hi_, mid)
        return lax.fori_loop(0, steps, body, (lo, hi))

    as_float = lambda bits: pltpu.bitcast(bits, F32)
    one_bits = 0x3F800001
    lo, hi = bisect(31, jnp.zeros((1, LANES), jnp.int32), jnp.full((1, LANES), one_bits, jnp.int32),
                    lambda mid: count(packed >= as_float(mid)) >= cap)
    thr, nxt = as_float(lo), as_float(hi)
    above = packed >= nxt
    need = cap - count(above)
    tie_tok = jnp.where(packed >= thr, jnp.where(above, n, tok), n)
    cut, _ = bisect(n.bit_length(), jnp.zeros((1, LANES), jnp.int32), jnp.full((1, LANES), n, jnp.int32),
                    lambda mid: count(tie_tok < mid) < need)
    sel_p = jnp.where(above | (tie_tok <= cut), 1.0, 0.0)
    for g in range(ngrp):
        sel_g = sel_p if g == 0 else pltpu.roll(sel_p, LANES - N_EXPERTS * g, 1)
        sel_g = jnp.where(lane_p < N_EXPERTS, sel_g, 0.0)
        sel_sc[rows * g:rows * (g + 1), :] = sel_g
        gsel_ref[0, row0 + rows * g:row0 + rows * (g + 1), :] = sel_g * aff_rows(g)

    blk = min(n, 2 * LANES)
    r_i = lax.broadcasted_iota(jnp.int32, (blk, blk), 0)
    c_i = lax.broadcasted_iota(jnp.int32, (blk, blk), 1)
    tri = (r_i >= c_i).astype(BF16)

    def cum_body(bi, carry):
        r0 = pl.multiple_of(bi * blk, blk)
        s_blk = sel_sc[pl.ds(r0, blk), :]
        c_blk = jnp.dot(tri, s_blk.astype(BF16), preferred_element_type=F32) + carry
        cin_sc[pl.ds(r0, blk), :] = c_blk
        cex_ref[0, pl.ds(pl.multiple_of(row0 + r0, SUBLANES), blk), :] = (c_blk - s_blk).astype(jnp.int32)
        return c_blk[blk - 1:blk, :]

    lax.fori_loop(0, n // blk, cum_body, jnp.zeros((1, LANES), F32))

    slot = lax.broadcasted_iota(jnp.int32, (1, cap), 1).astype(F32)
    for e in range(N_EXPERTS):
        def idx_body(bi, acc):
            r0 = pl.multiple_of(bi * blk, blk)
            col = cin_sc[pl.ds(r0, blk), e:e + 1]
            return acc + jnp.sum(jnp.where(col <= slot, 1.0, 0.0), axis=0, keepdims=True)
        acc = lax.fori_loop(0, n // blk, idx_body, jnp.zeros((1, cap), F32))
        idx_ref[0, e:e + 1, :] = acc.astype(jnp.int32)


def _route_kernel(aff_ref, gsel_ref, cex_ref, *rest, groups):
    idx_refs, (sel_sc, cin_sc) = rest[:len(groups)], rest[len(groups):]
    for (row0, n, cap), idx_ref in zip(groups, idx_refs):
        _route_group(aff_ref, row0, n, cap, gsel_ref, cex_ref, idx_ref, sel_sc, cin_sc)


def _route(aff, groups):
    b, tl, _ = aff.shape
    blk = pl.BlockSpec((1, tl, LANES), lambda bb: (bb, 0, 0))
    nmax = max(n for _, n, _ in groups)
    return pl.pallas_call(
        functools.partial(_route_kernel, groups=groups),
        grid=(b,),
        in_specs=[blk],
        out_specs=[blk, blk] + [pl.BlockSpec((1, N_EXPERTS, cap), lambda bb: (bb, 0, 0)) for _, _, cap in groups],
        out_shape=[jax.ShapeDtypeStruct((b, tl, LANES), F32), jax.ShapeDtypeStruct((b, tl, LANES), jnp.int32)]
        + [jax.ShapeDtypeStruct((b, N_EXPERTS, cap), jnp.int32) for _, _, cap in groups],
        scratch_shapes=[pltpu.VMEM((nmax, LANES), F32), pltpu.VMEM((nmax, LANES), F32)],
        compiler_params=_params("parallel"),
        name="route",
    )(aff)


def _expert_ffn_kernel(idx_ref, h_hbm, wg_ref, wu_ref, wd_ref, y_ref, xbuf, xb, sem, *, nrows, nsteps):
    e = pl.program_id(0)
    f = pl.program_id(1)

    ne, nf = pl.num_programs(0), pl.num_programs(1)

    def row_copy(ee, j):
        return pltpu.make_async_copy(h_hbm.at[pl.ds(idx_ref[ee * nrows + j], 1), :],
                                     xbuf.at[pl.ds(j, 1), :], sem.at[0])

    def wait_rows(n):
        pltpu.make_async_copy(h_hbm.at[pl.ds(0, n), :], xbuf.at[pl.ds(0, n), :], sem.at[0]).wait()

    @pl.when(f == 0)
    def _():
        @pl.when(e == 0)
        def _():
            def body(j, c):
                row_copy(0, j).start()
                return c
            lax.fori_loop(0, nrows, body, 0, unroll=GATHER_UNROLL)

        wait_rows(nrows)
        xb[...] = xbuf[...].astype(BF16)

    per_step = nrows // nsteps
    nxt = jnp.minimum(e + 1, ne - 1)
    for jj in range(per_step):
        row_copy(nxt, f * per_step + jj).start()

    wg = wg_ref[0, 0].astype(BF16)
    wu = wu_ref[0, 0].astype(BF16)
    wd = wd_ref[0, 0].astype(BF16)
    rc = nrows // FFN_ROW_CHUNKS
    for ci in range(FFN_ROW_CHUNKS):
        rows = slice(rc * ci, rc * (ci + 1))
        x = xb[rows, :]
        g = jnp.dot(x, wg, preferred_element_type=F32)
        u = jnp.dot(x, wu, preferred_element_type=F32)
        part = jnp.dot((_silu(g) * u).astype(BF16), wd, preferred_element_type=F32)

        @pl.when(f == 0)
        def _():
            y_ref[0, rows, :] = part

        @pl.when(f > 0)
        def _():
            y_ref[0, rows, :] += part

    @pl.when((e == ne - 1) & (f == nf - 1))
    def _():
        wait_rows(nrows)


def _expert_ffn(idx_flat, h_flat, w_gate, w_up, w_down, layer, nrows):
    _, e, d, ff = w_gate.shape
    nf = ff // FF_TILE
    return pl.pallas_call(
        functools.partial(_expert_ffn_kernel, nrows=nrows, nsteps=nf),
        grid_spec=pltpu.PrefetchScalarGridSpec(
            num_scalar_prefetch=1,
            grid=(e, nf),
            in_specs=[pl.BlockSpec(memory_space=pl.ANY),
                      pl.BlockSpec((1, 1, d, FF_TILE), lambda ee, f, idx: (layer, ee, 0, f)),
                      pl.BlockSpec((1, 1, d, FF_TILE), lambda ee, f, idx: (layer, ee, 0, f)),
                      pl.BlockSpec((1, 1, FF_TILE, d), lambda ee, f, idx: (layer, ee, f, 0))],
            out_specs=pl.BlockSpec((1, nrows, d), lambda ee, f, idx: (ee, 0, 0),
                                   pipeline_mode=pl.Buffered(1)),
            scratch_shapes=[pltpu.VMEM((nrows, d), F32), pltpu.VMEM((nrows, d), BF16),
                            pltpu.SemaphoreType.DMA((1,))]),
        out_shape=jax.ShapeDtypeStruct((e, nrows, d), F32),
        compiler_params=_params("arbitrary", "arbitrary"),
        name="expert_ffn",
    )(idx_flat, h_flat, w_gate, w_up, w_down)


def _combine_kernel(wide_ref, start_n_ref, start_w_ref, y_hbm, xn_ref, mod_ref, cex_ref, gsel_ref,
                    delta_ref, o_ref, stage, sem):
    nt = pl.num_programs(1)
    step = pl.program_id(0) * nt + pl.program_id(1)
    nsteps = pl.num_programs(0) * nt
    slot = step % 2

    def copies(step_, slot_, start_ref, win):
        return [pltpu.make_async_copy(
            y_hbm.at[pl.ds(pl.multiple_of(start_ref[step_ * N_EXPERTS + e], SUBLANES), win), :],
            stage.at[slot_, pl.ds(e * win, win), :], sem.at[slot_]) for e in range(N_EXPERTS)]

    def by_width(step_, fn):
        @pl.when(wide_ref[step_] == 0)
        def _():
            fn(start_n_ref, WINDOW_NARROW, 0)

        @pl.when(wide_ref[step_] != 0)
        def _():
            fn(start_w_ref, WINDOW_WIDE, 1)

    def start(step_, slot_):
        by_width(step_, lambda ref, win, _: [cp.start() for cp in copies(step_, slot_, ref, win)])

    @pl.when(step == 0)
    def _():
        start(0, 0)

    @pl.when(step + 1 < nsteps)
    def _():
        start(step + 1, 1 - slot)

    gate = gsel_ref[0]
    cex = cex_ref[0]
    tile = xn_ref.shape[1]

    def reduce(start_ref, win, delta_row):
        for cp in copies(step, slot, start_ref, win):
            cp.wait()
        rpos = cex + delta_ref[0, 0, delta_row:delta_row + 1, :]
        acc = None
        for c0 in range(0, N_EXPERTS * win, LANES):
            lane_r = c0 + lax.broadcasted_iota(jnp.int32, (tile, LANES), 1)
            q = jnp.zeros((tile, LANES), F32)
            for e in range(c0 // win, min(N_EXPERTS - 1, (c0 + LANES - 1) // win) + 1):
                q = jnp.where(rpos[:, e:e + 1] == lane_r, gate[:, e:e + 1], q)
            part = jnp.dot(q.astype(BF16), stage[slot, c0:c0 + LANES, :].astype(BF16),
                           preferred_element_type=F32)
            acc = part if acc is None else acc + part
        o_ref[0] = xn_ref[0] + mod_ref[0, 5:6, :] * acc

    by_width(step, reduce)


def _combine(wide, start_n, start_w, y_flat, xn, mod, cex, gsel, delta, n_ctx_tiles):
    b, tl, d = xn.shape
    nt = tl // MOE_TILE
    tok = lambda w_: pl.BlockSpec((1, MOE_TILE, w_), lambda bb, i, *_: (bb, i, 0))
    return pl.pallas_call(
        _combine_kernel,
        grid_spec=pltpu.PrefetchScalarGridSpec(
            num_scalar_prefetch=3,
            grid=(b, nt),
            in_specs=[pl.BlockSpec(memory_space=pl.ANY), tok(d),
                      pl.BlockSpec((1, 6, d), lambda bb, i, *_: (jnp.where(i < n_ctx_tiles, 0, 1 + bb), 0, 0)),
                      tok(LANES), tok(LANES),
                      pl.BlockSpec((1, 1, SUBLANES, LANES), lambda bb, i, *_: (bb, i, 0, 0))],
            out_specs=tok(d),
            scratch_shapes=[pltpu.VMEM((2, N_EXPERTS * WINDOW_WIDE, d), F32),
                            pltpu.SemaphoreType.DMA((2,))]),
        out_shape=jax.ShapeDtypeStruct((b, tl, d), F32),
        compiler_params=_params("arbitrary", "arbitrary"),
        name="moe_combine",
    )(wide, start_n, start_w, y_flat, xn, mod, cex, gsel, delta)


def _moe(h2, aff, xn, mod, groups, w_gate, w_up, w_down, layer):
    b, tl, d = h2.shape
    ne = w_gate.shape[1]
    caps = [max(1, CAP_FACTOR * n // ne) for _, n in groups]
    nrows = sum(b * cap for cap in caps)
    assert all(n % MOE_TILE == 0 for _, n in groups) and all(cap % SUBLANES == 0 for cap in caps)
    gsel, cex, *idxs = _route(aff, [(row0, n, cap) for (row0, n), cap in zip(groups, caps)])
    idx_parts, src_parts, cnt_parts = [], [], []
    base = 0
    batch = jnp.arange(b, dtype=jnp.int32)
    for (row0, n), cap, idx in zip(groups, caps, idxs):
        rows = idx + (batch * tl + row0)[:, None, None]
        idx_parts.append(jnp.swapaxes(rows, 0, 1).reshape(ne, b * cap))
        s0 = cex[:, row0:row0 + n:MOE_TILE, :ne]
        ends = jnp.concatenate([s0[:, 1:], jnp.full((b, 1, ne), cap, jnp.int32)], axis=1)
        first = (base + batch * cap)[:, None, None] + (jnp.arange(ne, dtype=jnp.int32) * nrows)[None, None, :]
        src_parts.append(jnp.stack([s0 + first, jnp.broadcast_to(first, s0.shape)], axis=0))
        cnt_parts.append(ends - s0)
        base += b * cap
    cat = lambda parts, axis: parts[0] if len(parts) == 1 else jnp.concatenate(parts, axis=axis)
    src, first = cat(src_parts, 2)
    cnt = cat(cnt_parts, 1)
    wide = (cnt > WINDOW_NARROW - (SUBLANES - 1)).any(axis=-1).astype(jnp.int32)
    starts, deltas = [], []
    for win in (WINDOW_NARROW, WINDOW_WIDE):
        st = jnp.minimum(src // SUBLANES * SUBLANES, ne * nrows - win)
        starts.append(st.reshape(-1))
        deltas.append(jnp.arange(ne, dtype=jnp.int32) * win + first - st)
    delta = jnp.pad(jnp.stack(deltas, axis=2), ((0, 0), (0, 0), (0, SUBLANES - 2), (0, LANES - ne)))
    y = _expert_ffn(cat(idx_parts, 1).reshape(-1), h2.reshape(b * tl, d), w_gate, w_up, w_down, layer, nrows)
    return _combine(wide.reshape(-1), starts[0], starts[1], y.reshape(ne * nrows, d), xn, mod, cex, gsel,
                    delta, groups[0][1] // MOE_TILE if len(groups) > 1 else 0)


def _rope_tables(s_len, n_ctx, head_dim):
    quarter = head_dim // 4
    t = jnp.arange(s_len)
    row = (t // GRID_W).astype(F32)
    col = (t % GRID_W).astype(F32)
    inv = ROPE_THETA ** (-jnp.arange(quarter, dtype=F32) / quarter)
    ar, ac = row[:, None] * inv, col[:, None] * inv
    cos = jnp.concatenate([jnp.cos(ar), jnp.cos(ar), jnp.cos(ac), jnp.cos(ac)], axis=1)
    sin = jnp.concatenate([-jnp.sin(ar), jnp.sin(ar), -jnp.sin(ac), jnp.sin(ac)], axis=1)
    reps = LANES // head_dim
    cos, sin = jnp.tile(cos, (1, reps)), jnp.tile(sin, (1, reps))
    cos = jnp.concatenate([jnp.ones((n_ctx, LANES), F32), cos], axis=0)
    sin = jnp.concatenate([jnp.zeros((n_ctx, LANES), F32), sin], axis=0)
    return cos, sin


def _even_weight(w):
    nq, nkv = HQ_A * DH_A, HKV_A * DH_A
    q = w[:, :nq]
    dup = lambda m: jnp.concatenate(
        [m[:, DH_A * (h // 2):DH_A * (h // 2 + 1)] for h in range(2 * HKV_A)], axis=1)
    k = dup(w[:, nq:nq + nkv])
    v = dup(w[:, nq + nkv:nq + 2 * nkv])
    o = nq + 2 * nkv
    nconv, nz = 3 * H_B * DK_B, H_B * DK_B
    conv = w[:, o:o + nconv]
    z = w[:, o + nconv:o + nconv + nz]
    ab = w[:, o + nconv + nz:]
    ab = jnp.pad(ab, ((0, 0), (0, LANES - ab.shape[1])))
    return jnp.concatenate([q, k, v, conv, z, ab], axis=1).astype(BF16)


def _lane_vec(v):
    v = v.reshape(1, -1)
    return jnp.pad(v, ((0, 0), (0, LANES - v.shape[1])))


def kernel(x, c, ctx, c_ctx, w_mod, b_mod, norm_mix, norm_ffn, w_in_ab, w_out_ab, qnorm_a, knorm_a,
           sink_a, conv_b, a_log_b, dt_bias_b, onorm_b, w_in_c, w_out_c, qnorm_c, knorm_c,
           w_router, w_gate, w_up, w_down):
    b, s_len, d = x.shape
    n_ctx = ctx.shape[1]
    depth = w_mod.shape[0]
    assert b + 1 <= SUBLANES and n_ctx % ROW_TILE == 0 and s_len % ROW_TILE == 0
    t_all = n_ctx + s_len

    cvec = jnp.concatenate([c_ctx[None], c, jnp.zeros((SUBLANES - 1 - b, d), F32)], axis=0)
    mod = _modulation(cvec, w_mod, b_mod).reshape(depth, SUBLANES, 6, d)
    cos_a, sin_a = _rope_tables(s_len, n_ctx, DH_A)
    cos_c, sin_c = _rope_tables(s_len, n_ctx, DH_C)
    nq_a = HQ_A * DH_A
    seg = jnp.arange(nq_a) // DH_A
    ones_bd = (seg[:, None] == seg[None, :]).astype(BF16)

    xs = jnp.concatenate([ctx, x], axis=1)
    for i in range(depth):
        last = i == depth - 1
        j = i // 2
        gain1 = norm_mix[i].reshape(1, d)
        gain2 = norm_ffn[i].reshape(1, d)
        w_r = jnp.pad(w_router[i], ((0, 0), (0, LANES - N_EXPERTS)))
        w_r_hi = w_r.astype(BF16)
        w_r = jnp.stack([w_r_hi, (w_r - w_r_hi.astype(F32)).astype(BF16)])
        if i % 2 == 0:
            q, k, v, pc, z, ab = _inproj_even(
                xs, mod[i], gain1, _even_weight(w_in_ab[j]), cos_a, sin_a,
                jnp.tile(qnorm_a[j], HQ_A).reshape(1, -1), jnp.tile(knorm_a[j], 2 * HKV_A).reshape(1, -1),
                ones_bd, n_ctx)
            oa = _attn_a(sink_a[j], q, k, v, n_ctx)
            qb, kb, vb, gb = _gdn_prep(pc, conv_b[j], ab, _lane_vec(a_log_b[j]), _lane_vec(dt_bias_b[j]),
                                       n_ctx)
            u, w, qg, a, kdt, ge = _gdn_chunk(qb, kb, vb, gb)
            o_fwd, o_bwd = _gdn_scan(u, w, qg, a, kdt, ge, n_ctx)
            tile0 = n_ctx // ROW_TILE if last else 0
            xn, h2, aff = _outproj_call(
                _outproj_even_kernel, "outproj_even",
                [(oa, None, tile0), (o_fwd, None, tile0), (o_bwd, None, tile0), (z, None, tile0)], xs, mod[i],
                [w_out_ab[j].astype(BF16), onorm_b[j].reshape(1, -1), gain2, w_r], n_ctx, tile0)
        else:
            q, k, v = _inproj_odd(xs, mod[i], gain1, w_in_c[j].astype(BF16), cos_c, sin_c,
                                  qnorm_c[j].reshape(1, -1), knorm_c[j].reshape(1, -1), n_ctx)
            if last:
                o = _attn_c(q, k, v, n_ctx)
                tile0 = n_ctx // ROW_TILE
            else:
                raise NotImplementedError("context queries of a non-final odd layer")
            xn, h2, aff = _outproj_call(
                _outproj_odd_kernel, "outproj_odd", [(o, None, 0)], xs, mod[i],
                [w_out_c[j].astype(BF16), gain2, w_r], n_ctx, tile0)
        groups = [(0, s_len)] if last else [(0, n_ctx), (n_ctx, s_len)]
        xs = _moe(h2, aff, xn, mod[i], groups, w_gate, w_up, w_down, i)
    return xs
```

```python
import functools

import jax
import jax.numpy as jnp
from jax import lax
from jax.experimental import pallas as pl
from jax.experimental.pallas import tpu as pltpu

F32 = jnp.float32
BF16 = jnp.bfloat16
HIGHEST = lax.Precision.HIGHEST

GRID_W = 64
NORM_EPS = 1e-6
ROPE_THETA = 10000.0
NEG_INF = -1e30
HQ_A, HKV_A, DH_A, WINDOW = 8, 2, 64, 128
H_B, DK_B, CONV_K, CHUNK = 4, 128, 5, 64
INV_BASE = 8
HQ_C, HKV_C, DH_C = 8, 2, 128
N_EXPERTS, CAP_FACTOR = 16, 2

LANES = 128
SUBLANES = 8
VMEM_LIMIT = 56 * 2 ** 20

ROW_TILE = 256
Q_TILE = 128
KV_TILE_MAX = 1408
SOFTMAX_ROWS = 64
LOG2_E = 1.4426950408889634
FF_TILE = 512
FFN_ROW_CHUNKS = 2
MOE_TILE = 128
WINDOW_NARROW = 40
WINDOW_WIDE = MOE_TILE + SUBLANES
GATHER_UNROLL = 8
SLOT_DIGIT = 32
LIST_TABLE_MIN = 256


def _params(*sem):
    return pltpu.CompilerParams(dimension_semantics=sem, vmem_limit_bytes=VMEM_LIMIT)


def _silu(x):
    return x * (1.0 / (1.0 + jnp.exp(-x)))


def _sigmoid(x):
    return 1.0 / (1.0 + jnp.exp(-x))


def _norm_mod(x, gain, shift, scale):
    ms = jnp.mean(x * x, axis=-1, keepdims=True)
    y = x * lax.rsqrt(ms + NORM_EPS) * gain
    return y * (1.0 + scale) + shift


def _segment_mean_square(p, ones_bd, seg):
    sq = p * p
    hi = sq.astype(BF16)
    lo = (sq - hi.astype(F32)).astype(BF16)
    s = (jnp.dot(hi, ones_bd, preferred_element_type=F32)
         + jnp.dot(lo, ones_bd, preferred_element_type=F32))
    return s * (1.0 / seg)


def _rope(x, cos, sin_signed, dist):
    n = x.shape[-1]
    lane = lax.broadcasted_iota(jnp.int32, x.shape, 1)
    up = pltpu.roll(x, n - dist, 1)
    dn = pltpu.roll(x, dist, 1)
    partner = jnp.where((lane & dist) == 0, up, dn)
    return x * cos + partner * sin_signed


def _mod_kernel(c_ref, w_ref, b_ref, o_ref):
    s = _silu(c_ref[...])
    o_ref[0] = jnp.dot(s, w_ref[0], precision=HIGHEST, preferred_element_type=F32) + b_ref[0]


def _modulation(cvec, w_mod, b_mod):
    depth, d, n6 = w_mod.shape
    tn = 1536
    return pl.pallas_call(
        _mod_kernel,
        grid=(depth, n6 // tn),
        in_specs=[pl.BlockSpec((SUBLANES, d), lambda l, j: (0, 0)),
                  pl.BlockSpec((1, d, tn), lambda l, j: (l, 0, j)),
                  pl.BlockSpec((1, 1, tn), lambda l, j: (l, 0, j))],
        out_specs=pl.BlockSpec((1, SUBLANES, tn), lambda l, j: (l, 0, j)),
        out_shape=jax.ShapeDtypeStruct((depth, SUBLANES, n6), F32),
        compiler_params=_params("parallel", "parallel"),
        name="modulation",
    )(cvec, w_mod, b_mod.reshape(depth, 1, n6))


def _mod_spec(n_ctx_tiles, tile0=0):
    return lambda b, i: (jnp.where(i + tile0 < n_ctx_tiles, 0, 1 + b), 0, 0)


def _inproj_even_kernel(x_ref, mod_ref, gain_ref, w_ref, cos_ref, sin_ref, qn_ref, kn_ref, ones_ref,
                        q_ref, k_ref, v_ref, pc_ref, z_ref, ab_ref):
    h = _norm_mod(x_ref[0], gain_ref[...], mod_ref[0, 0:1, :], mod_ref[0, 1:2, :]).astype(BF16)

    def proj(lo, hi):
        return jnp.dot(h, w_ref[:, lo:hi], preferred_element_type=F32)

    c = cos_ref[...]
    s = sin_ref[...]
    nq, nk = HQ_A * DH_A, 2 * HKV_A * DH_A
    q = proj(0, nq)
    q = q * lax.rsqrt(_segment_mean_square(q, ones_ref[...], DH_A) + NORM_EPS) * qn_ref[...]
    q = _rope(q, jnp.concatenate([c] * (nq // LANES), axis=1),
              jnp.concatenate([s] * (nq // LANES), axis=1), DH_A // 4)
    q_ref[0] = (q * (DH_A ** -0.5 * LOG2_E)).astype(BF16)
    k = proj(nq, nq + nk)
    k = k * lax.rsqrt(_segment_mean_square(k, ones_ref[0:nk, 0:nk], DH_A) + NORM_EPS) * kn_ref[...]
    k = _rope(k, jnp.concatenate([c] * (nk // LANES), axis=1),
              jnp.concatenate([s] * (nk // LANES), axis=1), DH_A // 4)
    k_ref[0] = k.astype(BF16)
    o = nq + nk
    v_ref[0] = proj(o, o + nk).astype(BF16)
    o += nk
    nconv = pc_ref.shape[2]
    pc_ref[0] = proj(o, o + nconv)
    o += nconv
    nz = z_ref.shape[2]
    z_ref[0] = proj(o, o + nz)
    o += nz
    ab_ref[0] = proj(o, o + LANES)


def _inproj_even(xs, mod, gain, w, cos, sin, qn, kn, ones_bd, n_ctx):
    b, t, d = xs.shape
    tm = ROW_TILE
    nq, nk = HQ_A * DH_A, 2 * HKV_A * DH_A
    nconv, nz = 3 * H_B * DK_B, H_B * DK_B
    row = lambda w_: pl.BlockSpec((1, tm, w_), lambda bb, i: (bb, i, 0))
    const = lambda a: pl.BlockSpec(a.shape, lambda bb, i: (0,) * a.ndim)
    return pl.pallas_call(
        _inproj_even_kernel,
        grid=(b, t // tm),
        in_specs=[row(d), pl.BlockSpec((1, 6, d), _mod_spec(n_ctx // tm)), const(gain), const(w),
                  pl.BlockSpec((tm, LANES), lambda bb, i: (i, 0)),
                  pl.BlockSpec((tm, LANES), lambda bb, i: (i, 0)),
                  const(qn), const(kn), const(ones_bd)],
        out_specs=[row(nq), row(nk), row(nk), row(nconv), row(nz), row(LANES)],
        out_shape=[jax.ShapeDtypeStruct((b, t, nq), BF16), jax.ShapeDtypeStruct((b, t, nk), BF16),
                   jax.ShapeDtypeStruct((b, t, nk), BF16), jax.ShapeDtypeStruct((b, t, nconv), F32),
                   jax.ShapeDtypeStruct((b, t, nz), F32), jax.ShapeDtypeStruct((b, t, LANES), F32)],
        compiler_params=_params("parallel", "parallel"),
        name="inproj_even",
    )(xs, mod, gain, w, cos, sin, qn, kn, ones_bd)


def _attn_a_kernel(sink_ref, q_ref, k_ref, v_ref, o_ref, s_sc, p_sc, r_sc, *, n_ctx, t_all):
    i = pl.program_id(1)
    g_heads = HQ_A // HKV_A
    band = 3 * Q_TILE
    n = i - n_ctx // Q_TILE
    start = jnp.clip(n_ctx + (n - 1) * Q_TILE, 0, t_all - band)
    start = pl.multiple_of(start, Q_TILE)
    q = q_ref[0]
    rows = g_heads * Q_TILE
    lane = lax.broadcasted_iota(jnp.int32, (Q_TILE, LANES), 1)
    kj = lax.broadcasted_iota(jnp.int32, (rows, band), 1)
    rel = kj - (lax.broadcasted_iota(jnp.int32, (rows, band), 0) & (Q_TILE - 1))
    first = start - n_ctx
    off = first - n * Q_TILE
    valid = (n >= 0) & (rel >= -WINDOW - off) & (rel <= WINDOW - off) & (kj >= -first)
    dims = (((1,), (1,)), ((), ()))
    nkeys = n_ctx + band
    for h in range(HKV_A):
        cols = slice(LANES * h, LANES * (h + 1))
        parts = []
        for g in range(g_heads):
            j = g_heads * h + g
            tile = q[:, LANES * (j // 2):LANES * (j // 2 + 1)]
            keep = (lane >= DH_A * (j % 2)) & (lane < DH_A * (j % 2 + 1))
            parts.append(jnp.where(keep, tile, jnp.zeros_like(tile)))
        qs = jnp.concatenate(parts, axis=0)
        s_sc[h, :, 0:n_ctx] = lax.dot_general(qs, k_ref[0, 0:n_ctx, cols], dims,
                                              preferred_element_type=F32)
        s_b = lax.dot_general(qs, k_ref[0, pl.ds(start, band), cols], dims, preferred_element_type=F32)
        s_sc[h, :, n_ctx:nkeys] = jnp.where(valid, s_b, NEG_INF)
    for h in range(HKV_A):
        for rb in range(rows // SOFTMAX_ROWS):
            blk = slice(SOFTMAX_ROWS * rb, SOFTMAX_ROWS * (rb + 1))
            sink = sink_ref[g_heads * h + SOFTMAX_ROWS * rb // Q_TILE] * LOG2_E
            tiles = [slice(LANES * t, LANES * (t + 1)) for t in range(nkeys // LANES)]
            mx = s_sc[h, blk, tiles[0]]
            for t in tiles[1:]:
                mx = jnp.maximum(mx, s_sc[h, blk, t])
            m = jnp.maximum(mx.max(axis=1, keepdims=True), sink)
            m_b = jnp.broadcast_to(m, (SOFTMAX_ROWS, LANES))
            l_run = jnp.zeros((SOFTMAX_ROWS, LANES), F32)
            for t in tiles:
                p = jnp.exp2(s_sc[h, blk, t] - m_b)
                l_run = l_run + p
                p_sc[h, blk, t] = p.astype(BF16)
            den = l_run.sum(axis=1, keepdims=True) + jnp.exp2(sink - m)
            r_sc[h, blk, :] = jnp.broadcast_to(1.0 / den, (SOFTMAX_ROWS, LANES))
    for h in range(HKV_A):
        cols = slice(LANES * h, LANES * (h + 1))
        o = (jnp.dot(p_sc[h, :, 0:n_ctx], v_ref[0, 0:n_ctx, cols], preferred_element_type=F32)
             + jnp.dot(p_sc[h, :, n_ctx:nkeys], v_ref[0, pl.ds(start, band), cols],
                       preferred_element_type=F32)) * r_sc[h]
        for pair in range(g_heads // 2):
            lo = o[(2 * pair) * Q_TILE:(2 * pair + 1) * Q_TILE]
            hi = o[(2 * pair + 1) * Q_TILE:(2 * pair + 2) * Q_TILE]
            c0 = LANES * (g_heads // 2 * h + pair)
            o_ref[0, :, c0:c0 + LANES] = jnp.where(lane < DH_A, lo, hi).astype(BF16)


def _attn_a(sink, q, k, v, n_ctx):
    b, t, nq = q.shape
    nk = k.shape[2]
    rows, nkeys = HQ_A // HKV_A * Q_TILE, n_ctx + 3 * Q_TILE
    return pl.pallas_call(
        functools.partial(_attn_a_kernel, n_ctx=n_ctx, t_all=t),
        grid=(b, t // Q_TILE),
        in_specs=[pl.BlockSpec(memory_space=pltpu.SMEM),
                  pl.BlockSpec((1, Q_TILE, nq), lambda bb, i: (bb, i, 0)),
                  pl.BlockSpec((1, t, nk), lambda bb, i: (bb, 0, 0)),
                  pl.BlockSpec((1, t, nk), lambda bb, i: (bb, 0, 0))],
        out_specs=pl.BlockSpec((1, Q_TILE, nq), lambda bb, i: (bb, i, 0)),
        out_shape=jax.ShapeDtypeStruct((b, t, nq), BF16),
        scratch_shapes=[pltpu.VMEM((HKV_A, rows, nkeys), F32), pltpu.VMEM((HKV_A, rows, nkeys), BF16),
                        pltpu.VMEM((HKV_A, rows, LANES), F32)],
        compiler_params=_params("parallel", "arbitrary"),
        name="attn_window",
    )(sink, q, k, v)


def _gdn_prep_kernel(pc_ref, prev_ref, next_ref, cw_ref, ab_ref, alog_ref, dtb_ref,
                     q_ref, k_ref, v_ref, gb_ref, ext_sc, *, n_ctx, t_all):
    tm = pc_ref.shape[1]
    r0 = pl.program_id(1) * tm
    halo = SUBLANES
    prev_on = jnp.where((r0 == 0) | (r0 == n_ctx), 0.0, 1.0)
    next_on = jnp.where((r0 + tm == n_ctx) | (r0 + tm == t_all), 0.0, 1.0)
    ext_sc[0:halo, :] = prev_ref[0] * prev_on
    ext_sc[halo:halo + tm, :] = pc_ref[0]
    ext_sc[halo + tm:2 * halo + tm, :] = next_ref[0] * next_on
    nh = H_B * DK_B
    for grp, out_ref in enumerate((q_ref, k_ref, v_ref)):
        c0 = nh * grp
        acc = None
        for tap in range(CONV_K):
            off = halo - CONV_K // 2 + tap
            term = cw_ref[tap:tap + 1, c0:c0 + nh] * ext_sc[off:off + tm, c0:c0 + nh]
            acc = term if acc is None else acc + term
        y = _silu(acc)
        if grp == 2:
            out_ref[0] = y
            continue
        scale = DK_B ** -0.5 if grp == 0 else 1.0
        for h in range(H_B):
            yh = y[:, DK_B * h:DK_B * (h + 1)]
            inv = lax.rsqrt(jnp.sum(yh * yh, axis=-1, keepdims=True) + NORM_EPS)
            out_ref[0, :, DK_B * h:DK_B * (h + 1)] = yh * (inv * scale)
    ab = ab_ref[0]
    lane = lax.broadcasted_iota(jnp.int32, ab.shape, 1)
    xg = ab + dtb_ref[...]
    softplus = jnp.maximum(xg, 0.0) + jnp.log(1.0 + jnp.exp(-jnp.abs(xg)))
    g = -jnp.exp(alog_ref[...]) * softplus
    gb_ref[0] = jnp.where(lane < 2 * H_B, g, jnp.where(lane < 4 * H_B, _sigmoid(ab), 0.0))


def _gdn_prep(pc, conv_w, ab, alog, dtb, n_ctx):
    b, t, nconv = pc.shape
    tm = ROW_TILE
    nh = H_B * DK_B
    hb = tm // SUBLANES
    nblk = t // SUBLANES
    row = lambda w_: pl.BlockSpec((1, tm, w_), lambda bb, i: (bb, i, 0))
    const = lambda a: pl.BlockSpec(a.shape, lambda bb, i: (0,) * a.ndim)
    return pl.pallas_call(
        functools.partial(_gdn_prep_kernel, n_ctx=n_ctx, t_all=t),
        grid=(b, t // tm),
        in_specs=[row(nconv),
                  pl.BlockSpec((1, SUBLANES, nconv), lambda bb, i: (bb, jnp.maximum(i * hb - 1, 0), 0)),
                  pl.BlockSpec((1, SUBLANES, nconv),
                               lambda bb, i: (bb, jnp.minimum((i + 1) * hb, nblk - 1), 0)),
                  const(conv_w), row(LANES), const(alog), const(dtb)],
        out_specs=[row(nh), row(nh), row(nh), row(LANES)],
        out_shape=[jax.ShapeDtypeStruct((b, t, nh), F32)] * 3 + [jax.ShapeDtypeStruct((b, t, LANES), F32)],
        scratch_shapes=[pltpu.VMEM((tm + 2 * SUBLANES, nconv), F32)],
        compiler_params=_params("parallel", "parallel"),
        name="gdn_prep",
    )(pc, pc, pc, conv_w, ab, alog, dtb)


def _gdn_chunk_kernel(q_ref, k_ref, v_ref, gb_ref, u_ref, w_ref, qg_ref, a_ref, kdt_ref, ge_ref):
    c = CHUNK
    nchunks = q_ref.shape[1] // c
    r_i = lax.broadcasted_iota(jnp.int32, (c, c), 0)
    c_i = lax.broadcasted_iota(jnp.int32, (c, c), 1)
    tri_l = (r_i >= c_i).astype(F32)
    tri_u = (r_i <= c_i).astype(F32)
    row2 = lax.broadcasted_iota(jnp.int32, (c, LANES), 0)
    col2 = lax.broadcasted_iota(jnp.int32, (c, LANES), 1)
    colm = col2 & (c - 1)
    lane8 = lax.broadcasted_iota(jnp.int32, (1, LANES), 1)
    half_of = [(col2 // c) == (h % 2) for h in range(H_B)]
    eye_stack = jnp.concatenate(
        [jnp.where(half_of[h] & (row2 == colm), 1.0, 0.0) for h in range(H_B)], axis=0)
    stack = lambda m: jnp.concatenate([jnp.where(m, 1.0, 0.0)] * H_B, axis=0)
    base_mask = stack((row2 // INV_BASE) == (colm // INV_BASE))
    level_masks = [[], []]
    size = INV_BASE
    while size < c:
        same = (row2 // (2 * size)) == (colm // (2 * size))
        r_hi, c_hi = (row2 // size) % 2 == 1, (colm // size) % 2 == 1
        level_masks[0].append(stack(same & r_hi & ~c_hi))
        level_masks[1].append(stack(same & ~r_hi & c_hi))
        size *= 2

    def bmm(ls, rs):
        rcat = jnp.concatenate([rs[0:2 * c], rs[2 * c:4 * c]], axis=1).astype(BF16)
        full = jnp.dot(ls.astype(BF16), rcat, preferred_element_type=F32)
        return jnp.concatenate([full[0:2 * c, 0:LANES], full[2 * c:4 * c, LANES:2 * LANES]], axis=0)

    def wide(t_stack, mats):
        rv = jnp.concatenate([jnp.concatenate(mats[0:2], axis=0),
                              jnp.concatenate(mats[2:4], axis=0)], axis=1).astype(BF16)
        full = jnp.dot(t_stack.astype(BF16), rv, preferred_element_type=F32)
        return [full[c * h:c * (h + 1), LANES * (h // 2):LANES * (h // 2 + 1)] for h in range(H_B)]

    dims = (((1,), (1,)), ((), ()))
    head = lambda ref, rows, h: ref[0, rows, DK_B * h:DK_B * (h + 1)]
    combos = []
    for ci in range(nchunks):
        rows = slice(c * ci, c * (ci + 1))
        gb = gb_ref[0, rows, :]
        gc = jnp.where(lane8 < H_B,
                       jnp.dot(tri_l, gb, precision=HIGHEST, preferred_element_type=F32),
                       jnp.dot(tri_u, gb, precision=HIGHEST, preferred_element_type=F32))
        gc_t = gc.T
        eg = jnp.exp(gc)
        g_last = jnp.where(lane8 < H_B, gc[c - 1:c, :], gc[0:1, :])
        ge_ref[0, ci] = jnp.broadcast_to(jnp.exp(g_last), (SUBLANES, LANES))
        ek = jnp.exp(g_last - gc)
        raw = []
        for h in range(H_B):
            k_h = head(k_ref, rows, h)
            kq = jnp.concatenate([k_h, head(q_ref, rows, h)], axis=0).astype(BF16)
            kk = jnp.concatenate([k_h, k_h], axis=0).astype(BF16)
            raw.append(lax.dot_general(kq, kk, dims, preferred_element_type=F32))
        for d in range(2):
            keep = (row2 >= colm) if d == 0 else (row2 <= colm)
            strict = (row2 > colm) if d == 0 else (row2 < colm)
            a_blocks, aqk, betas, egs, eks = [], [], [], [], []
            for h in range(H_B):
                idx = H_B * d + h
                g_col = gc[:, idx:idx + 1]
                g_row = jnp.concatenate([gc_t[idx:idx + 1, :]] * 2, axis=1)
                decay = jnp.where(keep, jnp.exp(jnp.where(keep, g_col - g_row, 0.0)), 0.0)
                beta = gb[:, 2 * H_B + idx:2 * H_B + idx + 1]
                betas.append(beta)
                egs.append(eg[:, idx:idx + 1])
                eks.append(ek[:, idx:idx + 1])
                a_blocks.append(jnp.where(strict & half_of[h], beta * raw[h][0:c] * decay, 0.0))
                aqk.append(raw[h][c:2 * c] * decay)
            combos.append(dict(ci=ci, d=d, rows=rows, a=jnp.concatenate(a_blocks, axis=0),
                               aqk=aqk, betas=betas, egs=egs, eks=eks))

    xs = [-cb["a"] * base_mask for cb in combos]
    ts = [eye_stack + x for x in xs]
    ps = [bmm(x, x) for x in xs]
    ts = [t + bmm(t, p) for t, p in zip(ts, ps)]
    ps = [bmm(p, p) for p in ps]
    ts = [t + bmm(t, p) for t, p in zip(ts, ps)]
    for lvl in range(len(level_masks[0])):
        mids = [bmm(t, cb["a"] * level_masks[cb["d"]][lvl]) for t, cb in zip(ts, combos)]
        ts = [t - bmm(m, t) for t, m in zip(ts, mids)]

    for t_inv, cb in zip(ts, combos):
        ci, d, rows = cb["ci"], cb["d"], cb["rows"]
        qs = [head(q_ref, rows, h) for h in range(H_B)]
        ks = [head(k_ref, rows, h) for h in range(H_B)]
        us = wide(t_inv, [head(v_ref, rows, h) * cb["betas"][h] for h in range(H_B)])
        ws = wide(t_inv, [ks[h] * (cb["betas"][h] * cb["egs"][h]) for h in range(H_B)])
        for h in range(H_B):
            cols = slice(DK_B * h, DK_B * (h + 1))
            u_ref[d, 0, rows, cols] = us[h]
            w_ref[d, 0, rows, cols] = ws[h].astype(BF16)
            qg_ref[d, 0, rows, cols] = (qs[h] * cb["egs"][h]).astype(BF16)
        for pair in range(H_B // 2):
            h0, h1 = 2 * pair, 2 * pair + 1
            a_ref[d, 0, rows, LANES * pair:LANES * (pair + 1)] = jnp.where(
                col2 < c, cb["aqk"][h0], cb["aqk"][h1]).astype(BF16)
            kd0 = (ks[h0] * cb["eks"][h0]).T
            kd1 = (ks[h1] * cb["eks"][h1]).T
            kdt_ref[d, 0, ci, :, LANES * pair:LANES * (pair + 1)] = jnp.concatenate(
                [kd0, kd1], axis=1).astype(BF16)


def _gdn_chunk(qb, kb, vb, gb):
    b, t, nh = qb.shape
    tm = ROW_TILE
    cps = tm // CHUNK
    nck = t // CHUNK
    row = lambda w_: pl.BlockSpec((1, tm, w_), lambda bb, i: (bb, i, 0))
    drow = lambda w_: pl.BlockSpec((2, 1, tm, w_), lambda bb, i: (0, bb, i, 0))
    return pl.pallas_call(
        _gdn_chunk_kernel,
        grid=(b, t // tm),
        in_specs=[row(nh), row(nh), row(nh), row(LANES)],
        out_specs=[drow(nh), drow(nh), drow(nh), drow(nh // 2),
                   pl.BlockSpec((2, 1, cps, DK_B, nh // 2), lambda bb, i: (0, bb, i, 0, 0)),
                   pl.BlockSpec((1, cps, SUBLANES, LANES), lambda bb, i: (bb, i, 0, 0))],
        out_shape=[jax.ShapeDtypeStruct((2, b, t, nh), F32), jax.ShapeDtypeStruct((2, b, t, nh), BF16),
                   jax.ShapeDtypeStruct((2, b, t, nh), BF16),
                   jax.ShapeDtypeStruct((2, b, t, nh // 2), BF16),
                   jax.ShapeDtypeStruct((2, b, nck, DK_B, nh // 2), BF16),
                   jax.ShapeDtypeStruct((b, nck, SUBLANES, LANES), F32)],
        compiler_params=_params("parallel", "parallel"),
        name="gdn_chunk",
    )(qb, kb, vb, gb)


def _gdn_scan_kernel(*refs):
    ins = (refs[0:6], refs[6:12])
    outs, s_sc = refs[12:14], refs[14]
    nb = outs[0].shape[0]

    @pl.when(pl.program_id(0) == 0)
    def _():
        s_sc[...] = jnp.zeros_like(s_sc)

    zero = jnp.zeros((CHUNK, DK_B), BF16)
    chains = [(d, b, h) for d in range(2) for b in range(nb) for h in range(H_B)]
    cols = lambda h: slice(DK_B * h, DK_B * (h + 1))
    pair = lambda h: slice(LANES * (h // 2), LANES * (h // 2 + 1))
    states = [s_sc[d, b, h] for d, b, h in chains]
    rs = [jnp.dot(jnp.concatenate([ins[d][1][0, b, :, cols(h)], ins[d][2][0, b, :, cols(h)]], axis=0),
                  s.astype(BF16), preferred_element_type=F32) for (d, b, h), s in zip(chains, states)]
    v_pads = []
    for (d, b, h), r in zip(chains, rs):
        v_new = (ins[d][0][0, b, :, cols(h)] - r[0:CHUNK]).astype(BF16)
        v_pads.append(jnp.concatenate([v_new, zero] if h % 2 == 0 else [zero, v_new], axis=0))
    for (d, b, h), s, r, v_pad in zip(chains, states, rs, v_pads):
        g_end = ins[d][5][b, 0, 0:1, H_B * d + h:H_B * d + h + 1]
        s_sc[d, b, h] = s * g_end + jnp.dot(ins[d][4][0, b, 0, :, pair(h)], v_pad,
                                            preferred_element_type=F32)
    for (d, b, h), r, v_pad in zip(chains, rs, v_pads):
        outs[d][b, :, cols(h)] = r[CHUNK:2 * CHUNK] + jnp.dot(
            ins[d][3][0, b, :, pair(h)], v_pad, preferred_element_type=F32)


def _gdn_scan(u, w, qg, a, kdt, ge, n_ctx):
    _, b, t, nh = u.shape
    nck = t // CHUNK
    ncc = n_ctx // CHUNK
    chunk = (lambda s: s,
             lambda s: jnp.where(s < ncc, ncc - 1 - s, nck - 1 - (s - ncc)))

    def specs(d):
        drow = lambda w_: pl.BlockSpec((1, b, CHUNK, w_), lambda s: (d, 0, chunk[d](s), 0))
        return [drow(nh), drow(nh), drow(nh), drow(nh // 2),
                pl.BlockSpec((1, b, 1, DK_B, nh // 2), lambda s: (d, 0, chunk[d](s), 0, 0)),
                pl.BlockSpec((b, 1, SUBLANES, LANES), lambda s: (0, chunk[d](s), 0, 0))]

    return pl.pallas_call(
        _gdn_scan_kernel,
        grid=(nck,),
        in_specs=specs(0) + specs(1),
        out_specs=[pl.BlockSpec((b, CHUNK, nh), lambda s, d=d: (0, chunk[d](s), 0)) for d in range(2)],
        out_shape=[jax.ShapeDtypeStruct((b, t, nh), F32)] * 2,
        scratch_shapes=[pltpu.VMEM((2, b, H_B, DK_B, DK_B), F32)],
        compiler_params=_params("arbitrary"),
        name="gdn_scan",
    )(u, w, qg, a, kdt, ge, u, w, qg, a, kdt, ge)


def _residual_router(y, x_ref, mod_ref, g2_ref, wr_ref, xn_ref, h2_ref, aff_ref):
    xn = x_ref[0] + mod_ref[0, 2:3, :] * y
    xn_ref[0] = xn
    h2 = _norm_mod(xn, g2_ref[...], mod_ref[0, 3:4, :], mod_ref[0, 4:5, :])
    h2_ref[0] = h2
    h_hi = h2.astype(BF16)
    h_lo = (h2 - h_hi.astype(F32)).astype(BF16)
    logits = (jnp.dot(h_hi, wr_ref[0], preferred_element_type=F32)
              + jnp.dot(h_lo, wr_ref[0], preferred_element_type=F32)
              + jnp.dot(h_hi, wr_ref[1], preferred_element_type=F32))
    lane = lax.broadcasted_iota(jnp.int32, logits.shape, 1)
    logits = jnp.where(lane < N_EXPERTS, logits, NEG_INF)
    e = jnp.exp(logits - logits.max(axis=-1, keepdims=True))
    aff_ref[0] = e / e.sum(axis=-1, keepdims=True)


def _outproj_even_kernel(oa_ref, of_ref, ob_ref, z_ref, x_ref, mod_ref, w_ref, onorm_ref, g2_ref, wr_ref,
                         xn_ref, h2_ref, aff_ref):
    na = oa_ref.shape[2]
    y = jnp.dot(oa_ref[0], w_ref[0:na, :], preferred_element_type=F32)
    o = of_ref[0] + ob_ref[0]
    z = z_ref[0]
    for h in range(H_B):
        cols = slice(DK_B * h, DK_B * (h + 1))
        oh = o[:, cols]
        ms = jnp.mean(oh * oh, axis=-1, keepdims=True)
        yh = oh * lax.rsqrt(ms + NORM_EPS) * onorm_ref[...] * _silu(z[:, cols])
        y = y + jnp.dot(yh.astype(BF16), w_ref[na + DK_B * h:na + DK_B * (h + 1), :],
                        preferred_element_type=F32)
    _residual_router(y, x_ref, mod_ref, g2_ref, wr_ref, xn_ref, h2_ref, aff_ref)


def _outproj_odd_kernel(o_ref, x_ref, mod_ref, w_ref, g2_ref, wr_ref, xn_ref, h2_ref, aff_ref):
    y = jnp.dot(o_ref[0], w_ref[...], preferred_element_type=F32)
    _residual_router(y, x_ref, mod_ref, g2_ref, wr_ref, xn_ref, h2_ref, aff_ref)


def _outproj_call(kernel, name, acts, xs, mod, consts, n_ctx, tile0):
    b, t, d = xs.shape
    tm = ROW_TILE
    nt = t // tm - tile0
    specs = []
    for a, lead, off in acts:
        if lead is None:
            specs.append(pl.BlockSpec((1, tm, a.shape[-1]), lambda bb, i, off=off: (bb, i + off, 0)))
        else:
            specs.append(pl.BlockSpec((1, 1, tm, a.shape[-1]),
                                      lambda bb, i, lead=lead, off=off: (lead, bb, i + off, 0)))
    row = pl.BlockSpec((1, tm, d), lambda bb, i: (bb, i + tile0, 0))
    const = lambda a: pl.BlockSpec(a.shape, lambda bb, i: (0,) * a.ndim)
    mod_spec = pl.BlockSpec((1, 6, d), _mod_spec(n_ctx // tm, tile0))
    wout, rest = consts[0], consts[1:]
    return pl.pallas_call(
        kernel,
        grid=(b, nt),
        in_specs=specs + [row, mod_spec, const(wout)] + [const(a) for a in rest],
        out_specs=[pl.BlockSpec((1, tm, d), lambda bb, i: (bb, i, 0)),
                   pl.BlockSpec((1, tm, d), lambda bb, i: (bb, i, 0)),
                   pl.BlockSpec((1, tm, LANES), lambda bb, i: (bb, i, 0))],
        out_shape=[jax.ShapeDtypeStruct((b, nt * tm, d), F32), jax.ShapeDtypeStruct((b, nt * tm, d), F32),
                   jax.ShapeDtypeStruct((b, nt * tm, LANES), F32)],
        compiler_params=_params("parallel", "parallel"),
        name=name,
    )(*[a for a, _, _ in acts], xs, mod, wout, *rest)


def _inproj_odd_kernel(x_ref, mod_ref, gain_ref, w_ref, cos_ref, sin_ref, qn_ref, kn_ref,
                       q_ref, k_ref, v_ref):
    h = _norm_mod(x_ref[0], gain_ref[...], mod_ref[0, 0:1, :], mod_ref[0, 1:2, :]).astype(BF16)
    c = cos_ref[...]
    s = sin_ref[...]
    nq, nk = HQ_C * DH_C, HKV_C * DH_C

    def normed_heads(lo, nheads, gain_ref_, out_ref, scale):
        p = jnp.dot(h, w_ref[:, lo:lo + nheads * DH_C], preferred_element_type=F32)
        for hh in range(nheads):
            ph = p[:, DH_C * hh:DH_C * (hh + 1)]
            ms = jnp.mean(ph * ph, axis=-1, keepdims=True)
            ph = _rope(ph * lax.rsqrt(ms + NORM_EPS) * gain_ref_[...], c, s, DH_C // 4)
            out_ref[0, :, DH_C * hh:DH_C * (hh + 1)] = (ph * scale).astype(BF16)

    normed_heads(0, HQ_C, qn_ref, q_ref, DH_C ** -0.5 * LOG2_E)
    normed_heads(nq, HKV_C, kn_ref, k_ref, 1.0)
    v_ref[0] = jnp.dot(h, w_ref[:, nq + nk:nq + 2 * nk], preferred_element_type=F32).astype(BF16)


def _inproj_odd(xs, mod, gain, w, cos, sin, qn, kn, n_ctx):
    b, t, d = xs.shape
    tm = ROW_TILE
    nq, nk = HQ_C * DH_C, HKV_C * DH_C
    row = lambda w_: pl.BlockSpec((1, tm, w_), lambda bb, i: (bb, i, 0))
    const = lambda a: pl.BlockSpec(a.shape, lambda bb, i: (0,) * a.ndim)
    return pl.pallas_call(
        _inproj_odd_kernel,
        grid=(b, t // tm),
        in_specs=[row(d), pl.BlockSpec((1, 6, d), _mod_spec(n_ctx // tm)), const(gain), const(w),
                  pl.BlockSpec((tm, LANES), lambda bb, i: (i, 0)),
                  pl.BlockSpec((tm, LANES), lambda bb, i: (i, 0)),
                  const(qn), const(kn)],
        out_specs=[row(nq), row(nk), row(nk)],
        out_shape=[jax.ShapeDtypeStruct((b, t, nq), BF16), jax.ShapeDtypeStruct((b, t, nk), BF16),
                   jax.ShapeDtypeStruct((b, t, nk), BF16)],
        compiler_params=_params("parallel", "parallel"),
        name="inproj_odd",
    )(xs, mod, gain, w, cos, sin, qn, kn)


def _attn_c_kernel(q_ref, k_ref, v_ref, o_ref, s_sc, p_sc, m_sc, l_sc, a_sc, acc_sc):
    kv_tile = s_sc.shape[2]
    nchunk = k_ref.shape[1] // kv_tile
    g_heads = HQ_C // HKV_C
    ntile = kv_tile // LANES
    nrows = g_heads * Q_TILE
    dims = (((1,), (1,)), ((), ()))
    qs = [jnp.concatenate([q_ref[0, :, DH_C * (g_heads * h + g):DH_C * (g_heads * h + g + 1)]
                           for g in range(g_heads)], axis=0) for h in range(HKV_C)]
    m_sc[...] = jnp.full(m_sc.shape, NEG_INF, F32)
    l_sc[...] = jnp.zeros(l_sc.shape, F32)
    acc_sc[...] = jnp.zeros(acc_sc.shape, F32)

    def body(ci, carry):
        r0 = pl.multiple_of(ci * kv_tile, kv_tile)
        for h in range(HKV_C):
            s_sc[h] = lax.dot_general(qs[h], k_ref[0, pl.ds(r0, kv_tile), DH_C * h:DH_C * (h + 1)],
                                      dims, preferred_element_type=F32)
        for h in range(HKV_C):
            for rb in range(nrows // SOFTMAX_ROWS):
                rows = slice(SOFTMAX_ROWS * rb, SOFTMAX_ROWS * (rb + 1))
                mx = s_sc[h, rows, 0:LANES]
                for t in range(1, ntile):
                    mx = jnp.maximum(mx, s_sc[h, rows, LANES * t:LANES * (t + 1)])
                m_old = m_sc[h, rows, :]
                m_new = jnp.maximum(m_old, jnp.broadcast_to(mx.max(axis=1, keepdims=True),
                                                            (SOFTMAX_ROWS, LANES)))
                alpha = jnp.exp2(m_old - m_new)
                l_new = alpha * l_sc[h, rows, :]
                for t in range(ntile):
                    p = jnp.exp2(s_sc[h, rows, LANES * t:LANES * (t + 1)] - m_new)
                    l_new = l_new + p
                    p_sc[h, rows, LANES * t:LANES * (t + 1)] = p.astype(BF16)
                l_sc[h, rows, :] = l_new
                m_sc[h, rows, :] = m_new
                a_sc[h, rows, :] = alpha
        for h in range(HKV_C):
            pv = jnp.dot(p_sc[h], v_ref[0, pl.ds(r0, kv_tile), DH_C * h:DH_C * (h + 1)],
                         preferred_element_type=F32)
            acc_sc[h] = a_sc[h] * acc_sc[h] + pv
        return carry

    lax.fori_loop(0, nchunk, body, 0)
    for j in range(HQ_C):
        h, rows = j // g_heads, slice(Q_TILE * (j % g_heads), Q_TILE * (j % g_heads + 1))
        o = acc_sc[h, rows, :] / l_sc[h, rows, :].sum(axis=1, keepdims=True)
        o_ref[0, :, DH_C * j:DH_C * (j + 1)] = o.astype(BF16)


def _attn_c(q, k, v, n_ctx):
    b, t, nq = q.shape
    nk = k.shape[2]
    t0 = n_ctx // Q_TILE
    rows = HQ_C // HKV_C * Q_TILE
    kv_tile = max(w for w in range(LANES, KV_TILE_MAX + 1, LANES) if t % w == 0)
    return pl.pallas_call(
        _attn_c_kernel,
        grid=(b, t // Q_TILE - t0),
        in_specs=[pl.BlockSpec((1, Q_TILE, nq), lambda bb, i: (bb, i + t0, 0)),
                  pl.BlockSpec((1, t, nk), lambda bb, i: (bb, 0, 0)),
                  pl.BlockSpec((1, t, nk), lambda bb, i: (bb, 0, 0))],
        out_specs=pl.BlockSpec((1, Q_TILE, nq), lambda bb, i: (bb, i, 0)),
        out_shape=jax.ShapeDtypeStruct((b, t - n_ctx, nq), BF16),
        scratch_shapes=[pltpu.VMEM((HKV_C, rows, kv_tile), F32), pltpu.VMEM((HKV_C, rows, kv_tile), BF16),
                        pltpu.VMEM((HKV_C, rows, LANES), F32), pltpu.VMEM((HKV_C, rows, LANES), F32),
                        pltpu.VMEM((HKV_C, rows, LANES), F32), pltpu.VMEM((HKV_C, rows, DH_C), F32)],
        compiler_params=_params("parallel", "arbitrary"),
        name="attn_global",
    )(q, k, v)


def _route_group(aff_ref, row0, n, cap, gsel_ref, cex_ref, idx_ref, sel_sc, cin_sc):
    ngrp = LANES // N_EXPERTS
    rows = n // ngrp
    aff_rows = lambda g: aff_ref[0, row0 + rows * g:row0 + rows * (g + 1), :]
    packed = aff_rows(0)
    for g in range(1, ngrp):
        packed = packed + pltpu.roll(aff_rows(g), N_EXPERTS * g, 1)
    lane_p = lax.broadcasted_iota(jnp.int32, (rows, LANES), 1)
    tok = lax.broadcasted_iota(jnp.int32, (rows, LANES), 0) + rows * (lane_p // N_EXPERTS)

    def count(mask):
        c = jnp.broadcast_to(jnp.sum(jnp.where(mask, 1.0, 0.0), axis=0, keepdims=True), (SUBLANES, LANES))
        shift = N_EXPERTS
        while shift < LANES:
            c = c + pltpu.roll(c, shift, 1)
            shift *= 2
        return c[0:1, :]

    def bisect(steps, lo, hi, enough):
        def body(_, c):
            lo_, hi_ = c
            mid = lo_ + ((hi_ - lo_) >> 1)
            ok = enough(mid)
            return jnp.where(ok, mid, lo_), jnp.where(ok, hi_, mid)
        return lax.fori_loop(0, steps, body, (lo, hi))

    as_float = lambda bits: pltpu.bitcast(bits, F32)
    one_bits = 0x3F800001
    lo, hi = bisect(31, jnp.zeros((1, LANES), jnp.int32), jnp.full((1, LANES), one_bits, jnp.int32),
                    lambda mid: count(packed >= as_float(mid)) >= cap)
    thr, nxt = as_float(lo), as_float(hi)
    above = packed >= nxt
    need = cap - count(above)
    tie_tok = jnp.where(packed >= thr, jnp.where(above, n, tok), n)
    cut, _ = bisect(n.bit_length(), jnp.zeros((1, LANES), jnp.int32), jnp.full((1, LANES), n, jnp.int32),
                    lambda mid: count(tie_tok < mid) < need)
    sel_p = jnp.where(above | (tie_tok <= cut), 1.0, 0.0)
    for g in range(ngrp):
        sel_g = sel_p if g == 0 else pltpu.roll(sel_p, LANES - N_EXPERTS * g, 1)
        sel_g = jnp.where(lane_p < N_EXPERTS, sel_g, 0.0)
        sel_sc[rows * g:rows * (g + 1), :] = sel_g
        gsel_ref[0, row0 + rows * g:row0 + rows * (g + 1), :] = sel_g * aff_rows(g)

    blk = min(n, 2 * LANES)
    r_i = lax.broadcasted_iota(jnp.int32, (blk, blk), 0)
    c_i = lax.broadcasted_iota(jnp.int32, (blk, blk), 1)
    tri = (r_i >= c_i).astype(BF16)

    def cum_body(bi, carry):
        r0 = pl.multiple_of(bi * blk, blk)
        s_blk = sel_sc[pl.ds(r0, blk), :]
        c_blk = jnp.dot(tri, s_blk.astype(BF16), preferred_element_type=F32) + carry
        cin_sc[pl.ds(r0, blk), :] = c_blk
        cex_ref[0, pl.ds(pl.multiple_of(row0 + r0, SUBLANES), blk), :] = (c_blk - s_blk).astype(jnp.int32)
        return c_blk[blk - 1:blk, :]

    lax.fori_loop(0, n // blk, cum_body, jnp.zeros((1, LANES), F32))

    if len(idx_ref.shape) == 3:
        slot = lax.broadcasted_iota(jnp.int32, (1, cap), 1).astype(F32)
        for e in range(N_EXPERTS):
            def idx_body(bi, acc):
                r0 = pl.multiple_of(bi * blk, blk)
                col = cin_sc[pl.ds(r0, blk), e:e + 1]
                return acc + jnp.sum(jnp.where(col <= slot, 1.0, 0.0), axis=0, keepdims=True)
            acc = lax.fori_loop(0, n // blk, idx_body, jnp.zeros((1, cap), F32))
            idx_ref[0, e:e + 1, :] = acc.astype(jnp.int32)
        return

    nh = cap // SLOT_DIGIT
    jh = lax.broadcasted_iota(jnp.int32, (nh, blk), 0).astype(F32)
    jl = lax.broadcasted_iota(jnp.int32, (blk, SLOT_DIGIT), 1).astype(F32)
    ones = jnp.ones((blk, SLOT_DIGIT), BF16)
    one_if = lambda m: jnp.where(m, 1.0, 0.0).astype(BF16)

    def table_body(bi, tables):
        r0 = pl.multiple_of(bi * blk, blk)
        c = cin_sc[pl.ds(r0, blk), :]
        hi = jnp.floor(c * (1.0 / SLOT_DIGIT))
        lo = c - SLOT_DIGIT * hi
        hi_t = hi.T
        out = []
        for e in range(N_EXPERTS):
            h_row = hi_t[e:e + 1, :]
            out.append(tables[e]
                       + jnp.dot(one_if(h_row == jh), one_if(lo[:, e:e + 1] <= jl), preferred_element_type=F32)
                       + jnp.dot(one_if(h_row < jh), ones, preferred_element_type=F32))
        return tuple(out)

    tables = lax.fori_loop(0, n // blk, table_body,
                           tuple(jnp.zeros((nh, SLOT_DIGIT), F32) for _ in range(N_EXPERTS)))
    for e in range(N_EXPERTS):
        idx_ref[0, e] = tables[e].astype(jnp.int32)


def _route_kernel(aff_ref, gsel_ref, cex_ref, *rest, groups):
    idx_refs, (sel_sc, cin_sc) = rest[:len(groups)], rest[len(groups):]
    for (row0, n, cap), idx_ref in zip(groups, idx_refs):
        _route_group(aff_ref, row0, n, cap, gsel_ref, cex_ref, idx_ref, sel_sc, cin_sc)


def _route(aff, groups):
    b, tl, _ = aff.shape
    blk = pl.BlockSpec((1, tl, LANES), lambda bb: (bb, 0, 0))
    nmax = max(n for _, n, _ in groups)
    list_shape = lambda cap: ((cap // SLOT_DIGIT, SLOT_DIGIT) if cap >= LIST_TABLE_MIN and cap % SLOT_DIGIT == 0
                              else (cap,))
    outs = pl.pallas_call(
        functools.partial(_route_kernel, groups=groups),
        grid=(b,),
        in_specs=[blk],
        out_specs=[blk, blk] + [pl.BlockSpec((1, N_EXPERTS) + list_shape(cap),
                                             lambda bb, nd=len(list_shape(cap)): (bb,) + (0,) * (nd + 1))
                                for _, _, cap in groups],
        out_shape=[jax.ShapeDtypeStruct((b, tl, LANES), F32), jax.ShapeDtypeStruct((b, tl, LANES), jnp.int32)]
        + [jax.ShapeDtypeStruct((b, N_EXPERTS) + list_shape(cap), jnp.int32) for _, _, cap in groups],
        scratch_shapes=[pltpu.VMEM((nmax, LANES), F32), pltpu.VMEM((nmax, LANES), F32)],
        compiler_params=_params("parallel"),
        name="route",
    )(aff)
    return list(outs[:2]) + [idx.reshape(b, N_EXPERTS, cap) for idx, (_, _, cap) in zip(outs[2:], groups)]


def _expert_ffn_kernel(idx_ref, h_hbm, wg_ref, wu_ref, wd_ref, y_ref, xbuf, xb, sem, *, nrows, nsteps):
    e = pl.program_id(0)
    f = pl.program_id(1)

    ne, nf = pl.num_programs(0), pl.num_programs(1)

    def row_copy(ee, j):
        return pltpu.make_async_copy(h_hbm.at[pl.ds(idx_ref[ee * nrows + j], 1), :],
                                     xbuf.at[pl.ds(j, 1), :], sem.at[0])

    def wait_rows(n):
        pltpu.make_async_copy(h_hbm.at[pl.ds(0, n), :], xbuf.at[pl.ds(0, n), :], sem.at[0]).wait()

    @pl.when(f == 0)
    def _():
        @pl.when(e == 0)
        def _():
            def body(j, c):
                row_copy(0, j).start()
                return c
            lax.fori_loop(0, nrows, body, 0, unroll=GATHER_UNROLL)

        wait_rows(nrows)
        xb[...] = xbuf[...].astype(BF16)

    per_step = nrows // nsteps
    nxt = jnp.minimum(e + 1, ne - 1)
    for jj in range(per_step):
        row_copy(nxt, f * per_step + jj).start()

    wg = wg_ref[0, 0].astype(BF16)
    wu = wu_ref[0, 0].astype(BF16)
    wd = wd_ref[0, 0].astype(BF16)
    rc = nrows // FFN_ROW_CHUNKS
    for ci in range(FFN_ROW_CHUNKS):
        rows = slice(rc * ci, rc * (ci + 1))
        x = xb[rows, :]
        g = jnp.dot(x, wg, preferred_element_type=F32)
        u = jnp.dot(x, wu, preferred_element_type=F32)
        part = jnp.dot((_silu(g) * u).astype(BF16), wd, preferred_element_type=F32)

        @pl.when(f == 0)
        def _():
            y_ref[0, rows, :] = part

        @pl.when(f > 0)
        def _():
            y_ref[0, rows, :] += part

    @pl.when((e == ne - 1) & (f == nf - 1))
    def _():
        wait_rows(nrows)


def _expert_ffn(idx_flat, h_flat, w_gate, w_up, w_down, layer, nrows):
    _, e, d, ff = w_gate.shape
    nf = ff // FF_TILE
    return pl.pallas_call(
        functools.partial(_expert_ffn_kernel, nrows=nrows, nsteps=nf),
        grid_spec=pltpu.PrefetchScalarGridSpec(
            num_scalar_prefetch=1,
            grid=(e, nf),
            in_specs=[pl.BlockSpec(memory_space=pl.ANY),
                      pl.BlockSpec((1, 1, d, FF_TILE), lambda ee, f, idx: (layer, ee, 0, f)),
                      pl.BlockSpec((1, 1, d, FF_TILE), lambda ee, f, idx: (layer, ee, 0, f)),
                      pl.BlockSpec((1, 1, FF_TILE, d), lambda ee, f, idx: (layer, ee, f, 0))],
            out_specs=pl.BlockSpec((1, nrows, d), lambda ee, f, idx: (ee, 0, 0),
                                   pipeline_mode=pl.Buffered(1)),
            scratch_shapes=[pltpu.VMEM((nrows, d), F32), pltpu.VMEM((nrows, d), BF16),
                            pltpu.SemaphoreType.DMA((1,))]),
        out_shape=jax.ShapeDtypeStruct((e, nrows, d), F32),
        compiler_params=_params("arbitrary", "arbitrary"),
        name="expert_ffn",
    )(idx_flat, h_flat, w_gate, w_up, w_down)


def _combine_kernel(wide_ref, start_n_ref, start_w_ref, y_hbm, xn_ref, mod_ref, cex_ref, gsel_ref,
                    delta_ref, o_ref, stage, sem):
    nt = pl.num_programs(1)
    step = pl.program_id(0) * nt + pl.program_id(1)
    nsteps = pl.num_programs(0) * nt
    slot = step % 2

    def copies(step_, slot_, start_ref, win):
        return [pltpu.make_async_copy(
            y_hbm.at[pl.ds(pl.multiple_of(start_ref[step_ * N_EXPERTS + e], SUBLANES), win), :],
            stage.at[slot_, pl.ds(e * win, win), :], sem.at[slot_]) for e in range(N_EXPERTS)]

    def by_width(step_, fn):
        @pl.when(wide_ref[step_] == 0)
        def _():
            fn(start_n_ref, WINDOW_NARROW, 0)

        @pl.when(wide_ref[step_] != 0)
        def _():
            fn(start_w_ref, WINDOW_WIDE, 1)

    def start(step_, slot_):
        by_width(step_, lambda ref, win, _: [cp.start() for cp in copies(step_, slot_, ref, win)])

    @pl.when(step == 0)
    def _():
        start(0, 0)

    @pl.when(step + 1 < nsteps)
    def _():
        start(step + 1, 1 - slot)

    gate = gsel_ref[0]
    cex = cex_ref[0]
    tile = xn_ref.shape[1]

    def reduce(start_ref, win, delta_row):
        for cp in copies(step, slot, start_ref, win):
            cp.wait()
        rpos = cex + delta_ref[0, 0, delta_row:delta_row + 1, :]
        acc = None
        for c0 in range(0, N_EXPERTS * win, LANES):
            lane_r = c0 + lax.broadcasted_iota(jnp.int32, (tile, LANES), 1)
            q = jnp.zeros((tile, LANES), F32)
            for e in range(c0 // win, min(N_EXPERTS - 1, (c0 + LANES - 1) // win) + 1):
                q = jnp.where(rpos[:, e:e + 1] == lane_r, gate[:, e:e + 1], q)
            part = jnp.dot(q.astype(BF16), stage[slot, c0:c0 + LANES, :].astype(BF16),
                           preferred_element_type=F32)
            acc = part if acc is None else acc + part
        o_ref[0] = xn_ref[0] + mod_ref[0, 5:6, :] * acc

    by_width(step, reduce)


def _combine(wide, start_n, start_w, y_flat, xn, mod, cex, gsel, delta, n_ctx_tiles):
    b, tl, d = xn.shape
    nt = tl // MOE_TILE
    tok = lambda w_: pl.BlockSpec((1, MOE_TILE, w_), lambda bb, i, *_: (bb, i, 0))
    return pl.pallas_call(
        _combine_kernel,
        grid_spec=pltpu.PrefetchScalarGridSpec(
            num_scalar_prefetch=3,
            grid=(b, nt),
            in_specs=[pl.BlockSpec(memory_space=pl.ANY), tok(d),
                      pl.BlockSpec((1, 6, d), lambda bb, i, *_: (jnp.where(i < n_ctx_tiles, 0, 1 + bb), 0, 0)),
                      tok(LANES), tok(LANES),
                      pl.BlockSpec((1, 1, SUBLANES, LANES), lambda bb, i, *_: (bb, i, 0, 0))],
            out_specs=tok(d),
            scratch_shapes=[pltpu.VMEM((2, N_EXPERTS * WINDOW_WIDE, d), F32),
                            pltpu.SemaphoreType.DMA((2,))]),
        out_shape=jax.ShapeDtypeStruct((b, tl, d), F32),
        compiler_params=_params("arbitrary", "arbitrary"),
        name="moe_combine",
    )(wide, start_n, start_w, y_flat, xn, mod, cex, gsel, delta)


def _moe(h2, aff, xn, mod, groups, w_gate, w_up, w_down, layer):
    b, tl, d = h2.shape
    ne = w_gate.shape[1]
    caps = [max(1, CAP_FACTOR * n // ne) for _, n in groups]
    nrows = sum(b * cap for cap in caps)
    assert all(n % MOE_TILE == 0 for _, n in groups) and all(cap % SUBLANES == 0 for cap in caps)
    gsel, cex, *idxs = _route(aff, [(row0, n, cap) for (row0, n), cap in zip(groups, caps)])
    idx_parts, src_parts, cnt_parts = [], [], []
    base = 0
    batch = jnp.arange(b, dtype=jnp.int32)
    for (row0, n), cap, idx in zip(groups, caps, idxs):
        rows = idx + (batch * tl + row0)[:, None, None]
        idx_parts.append(jnp.swapaxes(rows, 0, 1).reshape(ne, b * cap))
        s0 = cex[:, row0:row0 + n:MOE_TILE, :ne]
        ends = jnp.concatenate([s0[:, 1:], jnp.full((b, 1, ne), cap, jnp.int32)], axis=1)
        first = (base + batch * cap)[:, None, None] + (jnp.arange(ne, dtype=jnp.int32) * nrows)[None, None, :]
        src_parts.append(jnp.stack([s0 + first, jnp.broadcast_to(first, s0.shape)], axis=0))
        cnt_parts.append(ends - s0)
        base += b * cap
    cat = lambda parts, axis: parts[0] if len(parts) == 1 else jnp.concatenate(parts, axis=axis)
    src, first = cat(src_parts, 2)
    cnt = cat(cnt_parts, 1)
    wide = (cnt > WINDOW_NARROW - (SUBLANES - 1)).any(axis=-1).astype(jnp.int32)
    starts, deltas = [], []
    for win in (WINDOW_NARROW, WINDOW_WIDE):
        st = jnp.minimum(src // SUBLANES * SUBLANES, ne * nrows - win)
        starts.append(st.reshape(-1))
        deltas.append(jnp.arange(ne, dtype=jnp.int32) * win + first - st)
    delta = jnp.pad(jnp.stack(deltas, axis=2), ((0, 0), (0, 0), (0, SUBLANES - 2), (0, LANES - ne)))
    y = _expert_ffn(cat(idx_parts, 1).reshape(-1), h2.reshape(b * tl, d), w_gate, w_up, w_down, layer, nrows)
    return _combine(wide.reshape(-1), starts[0], starts[1], y.reshape(ne * nrows, d), xn, mod, cex, gsel,
                    delta, groups[0][1] // MOE_TILE if len(groups) > 1 else 0)


def _rope_tables(s_len, n_ctx, head_dim):
    quarter = head_dim // 4
    t = jnp.arange(s_len)
    row = (t // GRID_W).astype(F32)
    col = (t % GRID_W).astype(F32)
    inv = ROPE_THETA ** (-jnp.arange(quarter, dtype=F32) / quarter)
    ar, ac = row[:, None] * inv, col[:, None] * inv
    cos = jnp.concatenate([jnp.cos(ar), jnp.cos(ar), jnp.cos(ac), jnp.cos(ac)], axis=1)
    sin = jnp.concatenate([-jnp.sin(ar), jnp.sin(ar), -jnp.sin(ac), jnp.sin(ac)], axis=1)
    reps = LANES // head_dim
    cos, sin = jnp.tile(cos, (1, reps)), jnp.tile(sin, (1, reps))
    cos = jnp.concatenate([jnp.ones((n_ctx, LANES), F32), cos], axis=0)
    sin = jnp.concatenate([jnp.zeros((n_ctx, LANES), F32), sin], axis=0)
    return cos, sin


def _even_weight(w):
    nq, nkv = HQ_A * DH_A, HKV_A * DH_A
    q = w[:, :nq]
    dup = lambda m: jnp.concatenate(
        [m[:, DH_A * (h // 2):DH_A * (h // 2 + 1)] for h in range(2 * HKV_A)], axis=1)
    k = dup(w[:, nq:nq + nkv])
    v = dup(w[:, nq + nkv:nq + 2 * nkv])
    o = nq + 2 * nkv
    nconv, nz = 3 * H_B * DK_B, H_B * DK_B
    conv = w[:, o:o + nconv]
    z = w[:, o + nconv:o + nconv + nz]
    ab = w[:, o + nconv + nz:]
    ab = jnp.pad(ab, ((0, 0), (0, LANES - ab.shape[1])))
    return jnp.concatenate([q, k, v, conv, z, ab], axis=1).astype(BF16)


def _lane_vec(v):
    v = v.reshape(1, -1)
    return jnp.pad(v, ((0, 0), (0, LANES - v.shape[1])))


def kernel(x, c, ctx, c_ctx, w_mod, b_mod, norm_mix, norm_ffn, w_in_ab, w_out_ab, qnorm_a, knorm_a,
           sink_a, conv_b, a_log_b, dt_bias_b, onorm_b, w_in_c, w_out_c, qnorm_c, knorm_c,
           w_router, w_gate, w_up, w_down):
    b, s_len, d = x.shape
    n_ctx = ctx.shape[1]
    depth = w_mod.shape[0]
    assert b + 1 <= SUBLANES and n_ctx % ROW_TILE == 0 and s_len % ROW_TILE == 0
    t_all = n_ctx + s_len

    cvec = jnp.concatenate([c_ctx[None], c, jnp.zeros((SUBLANES - 1 - b, d), F32)], axis=0)
    mod = _modulation(cvec, w_mod, b_mod).reshape(depth, SUBLANES, 6, d)
    cos_a, sin_a = _rope_tables(s_len, n_ctx, DH_A)
    cos_c, sin_c = _rope_tables(s_len, n_ctx, DH_C)
    nq_a = HQ_A * DH_A
    seg = jnp.arange(nq_a) // DH_A
    ones_bd = (seg[:, None] == seg[None, :]).astype(BF16)

    xs = jnp.concatenate([ctx, x], axis=1)
    for i in range(depth):
        last = i == depth - 1
        j = i // 2
        gain1 = norm_mix[i].reshape(1, d)
        gain2 = norm_ffn[i].reshape(1, d)
        w_r = jnp.pad(w_router[i], ((0, 0), (0, LANES - N_EXPERTS)))
        w_r_hi = w_r.astype(BF16)
        w_r = jnp.stack([w_r_hi, (w_r - w_r_hi.astype(F32)).astype(BF16)])
        if i % 2 == 0:
            q, k, v, pc, z, ab = _inproj_even(
                xs, mod[i], gain1, _even_weight(w_in_ab[j]), cos_a, sin_a,
                jnp.tile(qnorm_a[j], HQ_A).reshape(1, -1), jnp.tile(knorm_a[j], 2 * HKV_A).reshape(1, -1),
                ones_bd, n_ctx)
            oa = _attn_a(sink_a[j], q, k, v, n_ctx)
            qb, kb, vb, gb = _gdn_prep(pc, conv_b[j], ab, _lane_vec(a_log_b[j]), _lane_vec(dt_bias_b[j]),
                                       n_ctx)
            u, w, qg, a, kdt, ge = _gdn_chunk(qb, kb, vb, gb)
            o_fwd, o_bwd = _gdn_scan(u, w, qg, a, kdt, ge, n_ctx)
            tile0 = n_ctx // ROW_TILE if last else 0
            xn, h2, aff = _outproj_call(
                _outproj_even_kernel, "outproj_even",
                [(oa, None, tile0), (o_fwd, None, tile0), (o_bwd, None, tile0), (z, None, tile0)], xs, mod[i],
                [w_out_ab[j].astype(BF16), onorm_b[j].reshape(1, -1), gain2, w_r], n_ctx, tile0)
        else:
            q, k, v = _inproj_odd(xs, mod[i], gain1, w_in_c[j].astype(BF16), cos_c, sin_c,
                                  qnorm_c[j].reshape(1, -1), knorm_c[j].reshape(1, -1), n_ctx)
            if last:
                o = _attn_c(q, k, v, n_ctx)
                tile0 = n_ctx // ROW_TILE
            else:
                raise NotImplementedError("context queries of a non-final odd layer")
            xn, h2, aff = _outproj_call(
                _outproj_odd_kernel, "outproj_odd", [(o, None, 0)], xs, mod[i],
                [w_out_c[j].astype(BF16), gain2, w_r], n_ctx, tile0)
        groups = [(0, s_len)] if last else [(0, n_ctx), (n_ctx, s_len)]
        xs = _moe(h2, aff, xn, mod[i], groups, w_gate, w_up, w_down, i)
    return xs
```

```python
import functools

import jax
import jax.numpy as jnp
from jax import lax
from jax.experimental import pallas as pl
from jax.experimental.pallas import tpu as pltpu

F32 = jnp.float32
BF16 = jnp.bfloat16
HIGHEST = lax.Precision.HIGHEST

GRID_W = 64
NORM_EPS = 1e-6
ROPE_THETA = 10000.0
NEG_INF = -1e30
HQ_A, HKV_A, DH_A, WINDOW = 8, 2, 64, 128
H_B, DK_B, CONV_K, CHUNK = 4, 128, 5, 64
INV_BASE = 8
HQ_C, HKV_C, DH_C = 8, 2, 128
N_EXPERTS, CAP_FACTOR = 16, 2

LANES = 128
SUBLANES = 8
VMEM_LIMIT = 56 * 2 ** 20

ROW_TILE = 256
Q_TILE = 128
GLOBAL_Q_TILE = 256
KV_TILE_MAX = 1408
SOFTMAX_ROWS = 64
LOG2_E = 1.4426950408889634
FF_TILE = 512
FFN_ROW_CHUNKS = 2
MOE_TILE = 128
WINDOW_NARROW = 40
WINDOW_WIDE = MOE_TILE + SUBLANES
GATHER_UNROLL = 8
SLOT_DIGIT = 32
LIST_TABLE_MIN = 256


def _params(*sem):
    return pltpu.CompilerParams(dimension_semantics=sem, vmem_limit_bytes=VMEM_LIMIT)


def _silu(x):
    return x * (1.0 / (1.0 + jnp.exp(-x)))


def _sigmoid(x):
    return 1.0 / (1.0 + jnp.exp(-x))


def _norm_mod(x, gain, shift, scale):
    ms = jnp.mean(x * x, axis=-1, keepdims=True)
    y = x * lax.rsqrt(ms + NORM_EPS) * gain
    return y * (1.0 + scale) + shift


def _segment_mean_square(p, ones_bd, seg):
    sq = p * p
    hi = sq.astype(BF16)
    lo = (sq - hi.astype(F32)).astype(BF16)
    s = (jnp.dot(hi, ones_bd, preferred_element_type=F32)
         + jnp.dot(lo, ones_bd, preferred_element_type=F32))
    return s * (1.0 / seg)


def _rope(x, cos, sin_signed, dist):
    n = x.shape[-1]
    lane = lax.broadcasted_iota(jnp.int32, x.shape, 1)
    up = pltpu.roll(x, n - dist, 1)
    dn = pltpu.roll(x, dist, 1)
    partner = jnp.where((lane & dist) == 0, up, dn)
    return x * cos + partner * sin_signed


def _mod_kernel(c_ref, w_ref, b_ref, o_ref):
    s = _silu(c_ref[...])
    o_ref[0] = jnp.dot(s, w_ref[0], precision=HIGHEST, preferred_element_type=F32) + b_ref[0]


def _modulation(cvec, w_mod, b_mod):
    depth, d, n6 = w_mod.shape
    tn = 1536
    return pl.pallas_call(
        _mod_kernel,
        grid=(depth, n6 // tn),
        in_specs=[pl.BlockSpec((SUBLANES, d), lambda l, j: (0, 0)),
                  pl.BlockSpec((1, d, tn), lambda l, j: (l, 0, j)),
                  pl.BlockSpec((1, 1, tn), lambda l, j: (l, 0, j))],
        out_specs=pl.BlockSpec((1, SUBLANES, tn), lambda l, j: (l, 0, j)),
        out_shape=jax.ShapeDtypeStruct((depth, SUBLANES, n6), F32),
        compiler_params=_params("parallel", "parallel"),
        name="modulation",
    )(cvec, w_mod, b_mod.reshape(depth, 1, n6))


def _mod_spec(n_ctx_tiles, tile0=0):
    return lambda b, i: (jnp.where(i + tile0 < n_ctx_tiles, 0, 1 + b), 0, 0)


def _inproj_even_kernel(x_ref, mod_ref, gain_ref, w_ref, cos_ref, sin_ref, qn_ref, kn_ref, ones_ref,
                        q_ref, k_ref, v_ref, pc_ref, z_ref, ab_ref):
    h = _norm_mod(x_ref[0], gain_ref[...], mod_ref[0, 0:1, :], mod_ref[0, 1:2, :]).astype(BF16)

    def proj(lo, hi):
        return jnp.dot(h, w_ref[:, lo:hi], preferred_element_type=F32)

    c = cos_ref[...]
    s = sin_ref[...]
    nq, nk = HQ_A * DH_A, 2 * HKV_A * DH_A
    q = proj(0, nq)
    q = q * lax.rsqrt(_segment_mean_square(q, ones_ref[...], DH_A) + NORM_EPS) * qn_ref[...]
    q = _rope(q, jnp.concatenate([c] * (nq // LANES), axis=1),
              jnp.concatenate([s] * (nq // LANES), axis=1), DH_A // 4)
    q_ref[0] = (q * (DH_A ** -0.5 * LOG2_E)).astype(BF16)
    k = proj(nq, nq + nk)
    k = k * lax.rsqrt(_segment_mean_square(k, ones_ref[0:nk, 0:nk], DH_A) + NORM_EPS) * kn_ref[...]
    k = _rope(k, jnp.concatenate([c] * (nk // LANES), axis=1),
              jnp.concatenate([s] * (nk // LANES), axis=1), DH_A // 4)
    k_ref[0] = k.astype(BF16)
    o = nq + nk
    v_ref[0] = proj(o, o + nk).astype(BF16)
    o += nk
    nconv = pc_ref.shape[2]
    pc_ref[0] = proj(o, o + nconv)
    o += nconv
    nz = z_ref.shape[2]
    z_ref[0] = proj(o, o + nz)
    o += nz
    ab_ref[0] = proj(o, o + LANES)


def _inproj_even(xs, mod, gain, w, cos, sin, qn, kn, ones_bd, n_ctx):
    b, t, d = xs.shape
    tm = ROW_TILE
    nq, nk = HQ_A * DH_A, 2 * HKV_A * DH_A
    nconv, nz = 3 * H_B * DK_B, H_B * DK_B
    row = lambda w_: pl.BlockSpec((1, tm, w_), lambda bb, i: (bb, i, 0))
    const = lambda a: pl.BlockSpec(a.shape, lambda bb, i: (0,) * a.ndim)
    return pl.pallas_call(
        _inproj_even_kernel,
        grid=(b, t // tm),
        in_specs=[row(d), pl.BlockSpec((1, 6, d), _mod_spec(n_ctx // tm)), const(gain), const(w),
                  pl.BlockSpec((tm, LANES), lambda bb, i: (i, 0)),
                  pl.BlockSpec((tm, LANES), lambda bb, i: (i, 0)),
                  const(qn), const(kn), const(ones_bd)],
        out_specs=[row(nq), row(nk), row(nk), row(nconv), row(nz), row(LANES)],
        out_shape=[jax.ShapeDtypeStruct((b, t, nq), BF16), jax.ShapeDtypeStruct((b, t, nk), BF16),
                   jax.ShapeDtypeStruct((b, t, nk), BF16), jax.ShapeDtypeStruct((b, t, nconv), F32),
                   jax.ShapeDtypeStruct((b, t, nz), F32), jax.ShapeDtypeStruct((b, t, LANES), F32)],
        compiler_params=_params("parallel", "parallel"),
        name="inproj_even",
    )(xs, mod, gain, w, cos, sin, qn, kn, ones_bd)


def _attn_a_kernel(sink_ref, q_ref, k_ref, v_ref, o_ref, s_sc, p_sc, r_sc, *, n_ctx, t_all):
    i = pl.program_id(1)
    g_heads = HQ_A // HKV_A
    band = 3 * Q_TILE
    n = i - n_ctx // Q_TILE
    start = jnp.clip(n_ctx + (n - 1) * Q_TILE, 0, t_all - band)
    start = pl.multiple_of(start, Q_TILE)
    q = q_ref[0]
    rows = g_heads * Q_TILE
    lane = lax.broadcasted_iota(jnp.int32, (Q_TILE, LANES), 1)
    kj = lax.broadcasted_iota(jnp.int32, (rows, band), 1)
    rel = kj - (lax.broadcasted_iota(jnp.int32, (rows, band), 0) & (Q_TILE - 1))
    first = start - n_ctx
    off = first - n * Q_TILE
    valid = (n >= 0) & (rel >= -WINDOW - off) & (rel <= WINDOW - off) & (kj >= -first)
    dims = (((1,), (1,)), ((), ()))
    nkeys = n_ctx + band
    for h in range(HKV_A):
        cols = slice(LANES * h, LANES * (h + 1))
        parts = []
        for g in range(g_heads):
            j = g_heads * h + g
            tile = q[:, LANES * (j // 2):LANES * (j // 2 + 1)]
            keep = (lane >= DH_A * (j % 2)) & (lane < DH_A * (j % 2 + 1))
            parts.append(jnp.where(keep, tile, jnp.zeros_like(tile)))
        qs = jnp.concatenate(parts, axis=0)
        s_sc[h, :, 0:n_ctx] = lax.dot_general(qs, k_ref[0, 0:n_ctx, cols], dims,
                                              preferred_element_type=F32)
        s_b = lax.dot_general(qs, k_ref[0, pl.ds(start, band), cols], dims, preferred_element_type=F32)
        s_sc[h, :, n_ctx:nkeys] = jnp.where(valid, s_b, NEG_INF)
    for h in range(HKV_A):
        for rb in range(rows // SOFTMAX_ROWS):
            blk = slice(SOFTMAX_ROWS * rb, SOFTMAX_ROWS * (rb + 1))
            sink = sink_ref[g_heads * h + SOFTMAX_ROWS * rb // Q_TILE] * LOG2_E
            tiles = [slice(LANES * t, LANES * (t + 1)) for t in range(nkeys // LANES)]
            mx = s_sc[h, blk, tiles[0]]
            for t in tiles[1:]:
                mx = jnp.maximum(mx, s_sc[h, blk, t])
            m = jnp.maximum(mx.max(axis=1, keepdims=True), sink)
            m_b = jnp.broadcast_to(m, (SOFTMAX_ROWS, LANES))
            l_run = jnp.zeros((SOFTMAX_ROWS, LANES), F32)
            for t in tiles:
                p = jnp.exp2(s_sc[h, blk, t] - m_b)
                l_run = l_run + p
                p_sc[h, blk, t] = p.astype(BF16)
            den = l_run.sum(axis=1, keepdims=True) + jnp.exp2(sink - m)
            r_sc[h, blk, :] = jnp.broadcast_to(1.0 / den, (SOFTMAX_ROWS, LANES))
    for h in range(HKV_A):
        cols = slice(LANES * h, LANES * (h + 1))
        o = (jnp.dot(p_sc[h, :, 0:n_ctx], v_ref[0, 0:n_ctx, cols], preferred_element_type=F32)
             + jnp.dot(p_sc[h, :, n_ctx:nkeys], v_ref[0, pl.ds(start, band), cols],
                       preferred_element_type=F32)) * r_sc[h]
        for pair in range(g_heads // 2):
            lo = o[(2 * pair) * Q_TILE:(2 * pair + 1) * Q_TILE]
            hi = o[(2 * pair + 1) * Q_TILE:(2 * pair + 2) * Q_TILE]
            c0 = LANES * (g_heads // 2 * h + pair)
            o_ref[0, :, c0:c0 + LANES] = jnp.where(lane < DH_A, lo, hi).astype(BF16)


def _attn_a(sink, q, k, v, n_ctx):
    b, t, nq = q.shape
    nk = k.shape[2]
    rows, nkeys = HQ_A // HKV_A * Q_TILE, n_ctx + 3 * Q_TILE
    return pl.pallas_call(
        functools.partial(_attn_a_kernel, n_ctx=n_ctx, t_all=t),
        grid=(b, t // Q_TILE),
        in_specs=[pl.BlockSpec(memory_space=pltpu.SMEM),
                  pl.BlockSpec((1, Q_TILE, nq), lambda bb, i: (bb, i, 0)),
                  pl.BlockSpec((1, t, nk), lambda bb, i: (bb, 0, 0)),
                  pl.BlockSpec((1, t, nk), lambda bb, i: (bb, 0, 0))],
        out_specs=pl.BlockSpec((1, Q_TILE, nq), lambda bb, i: (bb, i, 0)),
        out_shape=jax.ShapeDtypeStruct((b, t, nq), BF16),
        scratch_shapes=[pltpu.VMEM((HKV_A, rows, nkeys), F32), pltpu.VMEM((HKV_A, rows, nkeys), BF16),
                        pltpu.VMEM((HKV_A, rows, LANES), F32)],
        compiler_params=_params("parallel", "arbitrary"),
        name="attn_window",
    )(sink, q, k, v)


def _gdn_prep_kernel(pc_ref, prev_ref, next_ref, cw_ref, ab_ref, alog_ref, dtb_ref,
                     q_ref, k_ref, v_ref, gb_ref, ext_sc, *, n_ctx, t_all):
    tm = pc_ref.shape[1]
    r0 = pl.program_id(1) * tm
    halo = SUBLANES
    prev_on = jnp.where((r0 == 0) | (r0 == n_ctx), 0.0, 1.0)
    next_on = jnp.where((r0 + tm == n_ctx) | (r0 + tm == t_all), 0.0, 1.0)
    ext_sc[0:halo, :] = prev_ref[0] * prev_on
    ext_sc[halo:halo + tm, :] = pc_ref[0]
    ext_sc[halo + tm:2 * halo + tm, :] = next_ref[0] * next_on
    nh = H_B * DK_B
    for grp, out_ref in enumerate((q_ref, k_ref, v_ref)):
        c0 = nh * grp
        acc = None
        for tap in range(CONV_K):
            off = halo - CONV_K // 2 + tap
            term = cw_ref[tap:tap + 1, c0:c0 + nh] * ext_sc[off:off + tm, c0:c0 + nh]
            acc = term if acc is None else acc + term
        y = _silu(acc)
        if grp == 2:
            out_ref[0] = y
            continue
        scale = DK_B ** -0.5 if grp == 0 else 1.0
        for h in range(H_B):
            yh = y[:, DK_B * h:DK_B * (h + 1)]
            inv = lax.rsqrt(jnp.sum(yh * yh, axis=-1, keepdims=True) + NORM_EPS)
            out_ref[0, :, DK_B * h:DK_B * (h + 1)] = yh * (inv * scale)
    ab = ab_ref[0]
    lane = lax.broadcasted_iota(jnp.int32, ab.shape, 1)
    xg = ab + dtb_ref[...]
    softplus = jnp.maximum(xg, 0.0) + jnp.log(1.0 + jnp.exp(-jnp.abs(xg)))
    g = -jnp.exp(alog_ref[...]) * softplus
    gb_ref[0] = jnp.where(lane < 2 * H_B, g, jnp.where(lane < 4 * H_B, _sigmoid(ab), 0.0))


def _gdn_prep(pc, conv_w, ab, alog, dtb, n_ctx):
    b, t, nconv = pc.shape
    tm = ROW_TILE
    nh = H_B * DK_B
    hb = tm // SUBLANES
    nblk = t // SUBLANES
    row = lambda w_: pl.BlockSpec((1, tm, w_), lambda bb, i: (bb, i, 0))
    const = lambda a: pl.BlockSpec(a.shape, lambda bb, i: (0,) * a.ndim)
    return pl.pallas_call(
        functools.partial(_gdn_prep_kernel, n_ctx=n_ctx, t_all=t),
        grid=(b, t // tm),
        in_specs=[row(nconv),
                  pl.BlockSpec((1, SUBLANES, nconv), lambda bb, i: (bb, jnp.maximum(i * hb - 1, 0), 0)),
                  pl.BlockSpec((1, SUBLANES, nconv),
                               lambda bb, i: (bb, jnp.minimum((i + 1) * hb, nblk - 1), 0)),
                  const(conv_w), row(LANES), const(alog), const(dtb)],
        out_specs=[row(nh), row(nh), row(nh), row(LANES)],
        out_shape=[jax.ShapeDtypeStruct((b, t, nh), F32)] * 3 + [jax.ShapeDtypeStruct((b, t, LANES), F32)],
        scratch_shapes=[pltpu.VMEM((tm + 2 * SUBLANES, nconv), F32)],
        compiler_params=_params("parallel", "parallel"),
        name="gdn_prep",
    )(pc, pc, pc, conv_w, ab, alog, dtb)


def _gdn_chunk_kernel(q_ref, k_ref, v_ref, gb_ref, u_ref, w_ref, qg_ref, a_ref, kdt_ref, ge_ref):
    c = CHUNK
    nchunks = q_ref.shape[1] // c
    r_i = lax.broadcasted_iota(jnp.int32, (c, c), 0)
    c_i = lax.broadcasted_iota(jnp.int32, (c, c), 1)
    tri_lu = jnp.concatenate([(r_i >= c_i).astype(BF16), (r_i <= c_i).astype(BF16)], axis=0)

    def cumsums(g):
        total = None
        for _ in range(3):
            piece = g.astype(BF16)
            part = jnp.dot(tri_lu, piece, preferred_element_type=F32)
            total = part if total is None else total + part
            g = g - piece.astype(F32)
        return total[0:c], total[c:2 * c]
    row2 = lax.broadcasted_iota(jnp.int32, (c, LANES), 0)
    col2 = lax.broadcasted_iota(jnp.int32, (c, LANES), 1)
    colm = col2 & (c - 1)
    lane8 = lax.broadcasted_iota(jnp.int32, (1, LANES), 1)
    half_of = [(col2 // c) == (h % 2) for h in range(H_B)]
    eye_stack = jnp.concatenate(
        [jnp.where(half_of[h] & (row2 == colm), 1.0, 0.0) for h in range(H_B)], axis=0)
    stack = lambda m: jnp.concatenate([jnp.where(m, 1.0, 0.0)] * H_B, axis=0)
    base_mask = stack((row2 // INV_BASE) == (colm // INV_BASE))
    level_masks = [[], []]
    size = INV_BASE
    while size < c:
        same = (row2 // (2 * size)) == (colm // (2 * size))
        r_hi, c_hi = (row2 // size) % 2 == 1, (colm // size) % 2 == 1
        level_masks[0].append(stack(same & r_hi & ~c_hi))
        level_masks[1].append(stack(same & ~r_hi & c_hi))
        size *= 2

    def bmm(ls, rs):
        rcat = jnp.concatenate([rs[0:2 * c], rs[2 * c:4 * c]], axis=1).astype(BF16)
        full = jnp.dot(ls.astype(BF16), rcat, preferred_element_type=F32)
        return jnp.concatenate([full[0:2 * c, 0:LANES], full[2 * c:4 * c, LANES:2 * LANES]], axis=0)

    def wide(t_stack, mats):
        rv = jnp.concatenate([jnp.concatenate(mats[0:2], axis=0),
                              jnp.concatenate(mats[2:4], axis=0)], axis=1).astype(BF16)
        full = jnp.dot(t_stack.astype(BF16), rv, preferred_element_type=F32)
        return [full[c * h:c * (h + 1), LANES * (h // 2):LANES * (h // 2 + 1)] for h in range(H_B)]

    dims = (((1,), (1,)), ((), ()))
    head = lambda ref, rows, h: ref[0, rows, DK_B * h:DK_B * (h + 1)]
    combos = []
    for ci in range(nchunks):
        rows = slice(c * ci, c * (ci + 1))
        gb = gb_ref[0, rows, :]
        prefix, suffix = cumsums(gb)
        gc = jnp.where(lane8 < H_B, prefix, suffix)
        gc_t = gc.T
        eg = jnp.exp(gc)
        g_last = jnp.where(lane8 < H_B, gc[c - 1:c, :], gc[0:1, :])
        ge_ref[0, ci] = jnp.broadcast_to(jnp.exp(g_last), (SUBLANES, LANES))
        ek = jnp.exp(g_last - gc)
        raw = []
        for h in range(H_B):
            k_h = head(k_ref, rows, h)
            kq = jnp.concatenate([k_h, head(q_ref, rows, h)], axis=0).astype(BF16)
            kk = jnp.concatenate([k_h, k_h], axis=0).astype(BF16)
            raw.append(lax.dot_general(kq, kk, dims, preferred_element_type=F32))
        for d in range(2):
            keep = (row2 >= colm) if d == 0 else (row2 <= colm)
            strict = (row2 > colm) if d == 0 else (row2 < colm)
            a_blocks, aqk, betas, egs, eks = [], [], [], [], []
            for h in range(H_B):
                idx = H_B * d + h
                g_col = gc[:, idx:idx + 1]
                g_row = jnp.concatenate([gc_t[idx:idx + 1, :]] * 2, axis=1)
                decay = jnp.where(keep, jnp.exp(jnp.where(keep, g_col - g_row, 0.0)), 0.0)
                beta = gb[:, 2 * H_B + idx:2 * H_B + idx + 1]
                betas.append(beta)
                egs.append(eg[:, idx:idx + 1])
                eks.append(ek[:, idx:idx + 1])
                a_blocks.append(jnp.where(strict & half_of[h], beta * raw[h][0:c] * decay, 0.0))
                aqk.append(raw[h][c:2 * c] * decay)
            combos.append(dict(ci=ci, d=d, rows=rows, a=jnp.concatenate(a_blocks, axis=0),
                               aqk=aqk, betas=betas, egs=egs, eks=eks))

    xs = [-cb["a"] * base_mask for cb in combos]
    ts = [eye_stack + x for x in xs]
    ps = [bmm(x, x) for x in xs]
    ts = [t + bmm(t, p) for t, p in zip(ts, ps)]
    ps = [bmm(p, p) for p in ps]
    ts = [t + bmm(t, p) for t, p in zip(ts, ps)]
    for lvl in range(len(level_masks[0])):
        mids = [bmm(t, cb["a"] * level_masks[cb["d"]][lvl]) for t, cb in zip(ts, combos)]
        ts = [t - bmm(m, t) for t, m in zip(ts, mids)]

    for t_inv, cb in zip(ts, combos):
        ci, d, rows = cb["ci"], cb["d"], cb["rows"]
        qs = [head(q_ref, rows, h) for h in range(H_B)]
        ks = [head(k_ref, rows, h) for h in range(H_B)]
        us = wide(t_inv, [head(v_ref, rows, h) * cb["betas"][h] for h in range(H_B)])
        ws = wide(t_inv, [ks[h] * (cb["betas"][h] * cb["egs"][h]) for h in range(H_B)])
        for h in range(H_B):
            cols = slice(DK_B * h, DK_B * (h + 1))
            u_ref[d, 0, rows, cols] = us[h]
            w_ref[d, 0, rows, cols] = ws[h].astype(BF16)
            qg_ref[d, 0, rows, cols] = (qs[h] * cb["egs"][h]).astype(BF16)
        for pair in range(H_B // 2):
            h0, h1 = 2 * pair, 2 * pair + 1
            a_ref[d, 0, rows, LANES * pair:LANES * (pair + 1)] = jnp.where(
                col2 < c, cb["aqk"][h0], cb["aqk"][h1]).astype(BF16)
            kd0 = (ks[h0] * cb["eks"][h0]).T
            kd1 = (ks[h1] * cb["eks"][h1]).T
            kdt_ref[d, 0, ci, :, LANES * pair:LANES * (pair + 1)] = jnp.concatenate(
                [kd0, kd1], axis=1).astype(BF16)


def _gdn_chunk(qb, kb, vb, gb):
    b, t, nh = qb.shape
    tm = ROW_TILE
    cps = tm // CHUNK
    nck = t // CHUNK
    row = lambda w_: pl.BlockSpec((1, tm, w_), lambda bb, i: (bb, i, 0))
    drow = lambda w_: pl.BlockSpec((2, 1, tm, w_), lambda bb, i: (0, bb, i, 0))
    return pl.pallas_call(
        _gdn_chunk_kernel,
        grid=(b, t // tm),
        in_specs=[row(nh), row(nh), row(nh), row(LANES)],
        out_specs=[drow(nh), drow(nh), drow(nh), drow(nh // 2),
                   pl.BlockSpec((2, 1, cps, DK_B, nh // 2), lambda bb, i: (0, bb, i, 0, 0)),
                   pl.BlockSpec((1, cps, SUBLANES, LANES), lambda bb, i: (bb, i, 0, 0))],
        out_shape=[jax.ShapeDtypeStruct((2, b, t, nh), F32), jax.ShapeDtypeStruct((2, b, t, nh), BF16),
                   jax.ShapeDtypeStruct((2, b, t, nh), BF16),
                   jax.ShapeDtypeStruct((2, b, t, nh // 2), BF16),
                   jax.ShapeDtypeStruct((2, b, nck, DK_B, nh // 2), BF16),
                   jax.ShapeDtypeStruct((b, nck, SUBLANES, LANES), F32)],
        compiler_params=_params("parallel", "parallel"),
        name="gdn_chunk",
    )(qb, kb, vb, gb)


def _gdn_scan_kernel(*refs):
    ins = (refs[0:6], refs[6:12])
    outs, s_sc = refs[12:14], refs[14]
    nb = outs[0].shape[0]

    @pl.when(pl.program_id(0) == 0)
    def _():
        s_sc[...] = jnp.zeros_like(s_sc)

    zero = jnp.zeros((CHUNK, DK_B), BF16)
    chains = [(d, b, h) for d in range(2) for b in range(nb) for h in range(H_B)]
    cols = lambda h: slice(DK_B * h, DK_B * (h + 1))
    pair = lambda h: slice(LANES * (h // 2), LANES * (h // 2 + 1))
    states = [s_sc[d, b, h] for d, b, h in chains]
    rs = [jnp.dot(jnp.concatenate([ins[d][1][0, b, :, cols(h)], ins[d][2][0, b, :, cols(h)]], axis=0),
                  s.astype(BF16), preferred_element_type=F32) for (d, b, h), s in zip(chains, states)]
    v_pads = []
    for (d, b, h), r in zip(chains, rs):
        v_new = (ins[d][0][0, b, :, cols(h)] - r[0:CHUNK]).astype(BF16)
        v_pads.append(jnp.concatenate([v_new, zero] if h % 2 == 0 else [zero, v_new], axis=0))
    for (d, b, h), s, r, v_pad in zip(chains, states, rs, v_pads):
        g_end = ins[d][5][b, 0, 0:1, H_B * d + h:H_B * d + h + 1]
        s_sc[d, b, h] = s * g_end + jnp.dot(ins[d][4][0, b, 0, :, pair(h)], v_pad,
                                            preferred_element_type=F32)
    for (d, b, h), r, v_pad in zip(chains, rs, v_pads):
        outs[d][b, :, cols(h)] = r[CHUNK:2 * CHUNK] + jnp.dot(
            ins[d][3][0, b, :, pair(h)], v_pad, preferred_element_type=F32)


def _gdn_scan(u, w, qg, a, kdt, ge, n_ctx):
    _, b, t, nh = u.shape
    nck = t // CHUNK
    ncc = n_ctx // CHUNK
    chunk = (lambda s: s,
             lambda s: jnp.where(s < ncc, ncc - 1 - s, nck - 1 - (s - ncc)))

    def specs(d):
        drow = lambda w_: pl.BlockSpec((1, b, CHUNK, w_), lambda s: (d, 0, chunk[d](s), 0))
        return [drow(nh), drow(nh), drow(nh), drow(nh // 2),
                pl.BlockSpec((1, b, 1, DK_B, nh // 2), lambda s: (d, 0, chunk[d](s), 0, 0)),
                pl.BlockSpec((b, 1, SUBLANES, LANES), lambda s: (0, chunk[d](s), 0, 0))]

    return pl.pallas_call(
        _gdn_scan_kernel,
        grid=(nck,),
        in_specs=specs(0) + specs(1),
        out_specs=[pl.BlockSpec((b, CHUNK, nh), lambda s, d=d: (0, chunk[d](s), 0)) for d in range(2)],
        out_shape=[jax.ShapeDtypeStruct((b, t, nh), F32)] * 2,
        scratch_shapes=[pltpu.VMEM((2, b, H_B, DK_B, DK_B), F32)],
        compiler_params=_params("arbitrary"),
        name="gdn_scan",
    )(u, w, qg, a, kdt, ge, u, w, qg, a, kdt, ge)


def _residual_router(y, x_ref, mod_ref, g2_ref, wr_ref, xn_ref, h2_ref, aff_ref):
    xn = x_ref[0] + mod_ref[0, 2:3, :] * y
    xn_ref[0] = xn
    h2 = _norm_mod(xn, g2_ref[...], mod_ref[0, 3:4, :], mod_ref[0, 4:5, :])
    h2_ref[0] = h2
    h_hi = h2.astype(BF16)
    h_lo = (h2 - h_hi.astype(F32)).astype(BF16)
    logits = (jnp.dot(h_hi, wr_ref[0], preferred_element_type=F32)
              + jnp.dot(h_lo, wr_ref[0], preferred_element_type=F32)
              + jnp.dot(h_hi, wr_ref[1], preferred_element_type=F32))
    lane = lax.broadcasted_iota(jnp.int32, logits.shape, 1)
    logits = jnp.where(lane < N_EXPERTS, logits, NEG_INF)
    e = jnp.exp(logits - logits.max(axis=-1, keepdims=True))
    aff_ref[0] = e / e.sum(axis=-1, keepdims=True)


def _outproj_even_kernel(oa_ref, of_ref, ob_ref, z_ref, x_ref, mod_ref, w_ref, onorm_ref, g2_ref, wr_ref,
                         xn_ref, h2_ref, aff_ref):
    na = oa_ref.shape[2]
    y = jnp.dot(oa_ref[0], w_ref[0:na, :], preferred_element_type=F32)
    o = of_ref[0] + ob_ref[0]
    z = z_ref[0]
    for h in range(H_B):
        cols = slice(DK_B * h, DK_B * (h + 1))
        oh = o[:, cols]
        ms = jnp.mean(oh * oh, axis=-1, keepdims=True)
        yh = oh * lax.rsqrt(ms + NORM_EPS) * onorm_ref[...] * _silu(z[:, cols])
        y = y + jnp.dot(yh.astype(BF16), w_ref[na + DK_B * h:na + DK_B * (h + 1), :],
                        preferred_element_type=F32)
    _residual_router(y, x_ref, mod_ref, g2_ref, wr_ref, xn_ref, h2_ref, aff_ref)


def _outproj_odd_kernel(o_ref, x_ref, mod_ref, w_ref, g2_ref, wr_ref, xn_ref, h2_ref, aff_ref):
    y = jnp.dot(o_ref[0], w_ref[...], preferred_element_type=F32)
    _residual_router(y, x_ref, mod_ref, g2_ref, wr_ref, xn_ref, h2_ref, aff_ref)


def _outproj_call(kernel, name, acts, xs, mod, consts, n_ctx, tile0):
    b, t, d = xs.shape
    tm = ROW_TILE
    nt = t // tm - tile0
    specs = []
    for a, lead, off in acts:
        if lead is None:
            specs.append(pl.BlockSpec((1, tm, a.shape[-1]), lambda bb, i, off=off: (bb, i + off, 0)))
        else:
            specs.append(pl.BlockSpec((1, 1, tm, a.shape[-1]),
                                      lambda bb, i, lead=lead, off=off: (lead, bb, i + off, 0)))
    row = pl.BlockSpec((1, tm, d), lambda bb, i: (bb, i + tile0, 0))
    const = lambda a: pl.BlockSpec(a.shape, lambda bb, i: (0,) * a.ndim)
    mod_spec = pl.BlockSpec((1, 6, d), _mod_spec(n_ctx // tm, tile0))
    wout, rest = consts[0], consts[1:]
    return pl.pallas_call(
        kernel,
        grid=(b, nt),
        in_specs=specs + [row, mod_spec, const(wout)] + [const(a) for a in rest],
        out_specs=[pl.BlockSpec((1, tm, d), lambda bb, i: (bb, i, 0)),
                   pl.BlockSpec((1, tm, d), lambda bb, i: (bb, i, 0)),
                   pl.BlockSpec((1, tm, LANES), lambda bb, i: (bb, i, 0))],
        out_shape=[jax.ShapeDtypeStruct((b, nt * tm, d), F32), jax.ShapeDtypeStruct((b, nt * tm, d), F32),
                   jax.ShapeDtypeStruct((b, nt * tm, LANES), F32)],
        compiler_params=_params("parallel", "parallel"),
        name=name,
    )(*[a for a, _, _ in acts], xs, mod, wout, *rest)


def _inproj_odd_kernel(x_ref, mod_ref, gain_ref, w_ref, cos_ref, sin_ref, qn_ref, kn_ref,
                       q_ref, k_ref, v_ref):
    h = _norm_mod(x_ref[0], gain_ref[...], mod_ref[0, 0:1, :], mod_ref[0, 1:2, :]).astype(BF16)
    c = cos_ref[...]
    s = sin_ref[...]
    nq, nk = HQ_C * DH_C, HKV_C * DH_C

    def normed_heads(lo, nheads, gain_ref_, out_ref, scale):
        p = jnp.dot(h, w_ref[:, lo:lo + nheads * DH_C], preferred_element_type=F32)
        for hh in range(nheads):
            ph = p[:, DH_C * hh:DH_C * (hh + 1)]
            ms = jnp.mean(ph * ph, axis=-1, keepdims=True)
            ph = _rope(ph * lax.rsqrt(ms + NORM_EPS) * gain_ref_[...], c, s, DH_C // 4)
            out_ref[0, :, DH_C * hh:DH_C * (hh + 1)] = (ph * scale).astype(BF16)

    normed_heads(0, HQ_C, qn_ref, q_ref, DH_C ** -0.5 * LOG2_E)
    normed_heads(nq, HKV_C, kn_ref, k_ref, 1.0)
    v_ref[0] = jnp.dot(h, w_ref[:, nq + nk:nq + 2 * nk], preferred_element_type=F32).astype(BF16)


def _inproj_odd(xs, mod, gain, w, cos, sin, qn, kn, n_ctx):
    b, t, d = xs.shape
    tm = ROW_TILE
    nq, nk = HQ_C * DH_C, HKV_C * DH_C
    row = lambda w_: pl.BlockSpec((1, tm, w_), lambda bb, i: (bb, i, 0))
    const = lambda a: pl.BlockSpec(a.shape, lambda bb, i: (0,) * a.ndim)
    return pl.pallas_call(
        _inproj_odd_kernel,
        grid=(b, t // tm),
        in_specs=[row(d), pl.BlockSpec((1, 6, d), _mod_spec(n_ctx // tm)), const(gain), const(w),
                  pl.BlockSpec((tm, LANES), lambda bb, i: (i, 0)),
                  pl.BlockSpec((tm, LANES), lambda bb, i: (i, 0)),
                  const(qn), const(kn)],
        out_specs=[row(nq), row(nk), row(nk)],
        out_shape=[jax.ShapeDtypeStruct((b, t, nq), BF16), jax.ShapeDtypeStruct((b, t, nk), BF16),
                   jax.ShapeDtypeStruct((b, t, nk), BF16)],
        compiler_params=_params("parallel", "parallel"),
        name="inproj_odd",
    )(xs, mod, gain, w, cos, sin, qn, kn)


def _attn_c_kernel(q_ref, k_ref, v_ref, o_ref, s_sc, p_sc, m_sc, l_sc, a_sc, acc_sc):
    kv_tile = s_sc.shape[2]
    nchunk = k_ref.shape[1] // kv_tile
    g_heads = HQ_C // HKV_C
    ntile = kv_tile // LANES
    q_tile = q_ref.shape[1]
    nrows = g_heads * q_tile
    dims = (((1,), (1,)), ((), ()))
    qs = [jnp.concatenate([q_ref[0, :, DH_C * (g_heads * h + g):DH_C * (g_heads * h + g + 1)]
                           for g in range(g_heads)], axis=0) for h in range(HKV_C)]
    m_sc[...] = jnp.full(m_sc.shape, NEG_INF, F32)
    l_sc[...] = jnp.zeros(l_sc.shape, F32)
    acc_sc[...] = jnp.zeros(acc_sc.shape, F32)

    def body(ci, carry):
        r0 = pl.multiple_of(ci * kv_tile, kv_tile)
        for h in range(HKV_C):
            s_sc[h] = lax.dot_general(qs[h], k_ref[0, pl.ds(r0, kv_tile), DH_C * h:DH_C * (h + 1)],
                                      dims, preferred_element_type=F32)
        for h in range(HKV_C):
            for rb in range(nrows // SOFTMAX_ROWS):
                rows = slice(SOFTMAX_ROWS * rb, SOFTMAX_ROWS * (rb + 1))
                mx = s_sc[h, rows, 0:LANES]
                for t in range(1, ntile):
                    mx = jnp.maximum(mx, s_sc[h, rows, LANES * t:LANES * (t + 1)])
                m_old = m_sc[h, rows, :]
                m_new = jnp.maximum(m_old, jnp.broadcast_to(mx.max(axis=1, keepdims=True),
                                                            (SOFTMAX_ROWS, LANES)))
                alpha = jnp.exp2(m_old - m_new)
                l_new = alpha * l_sc[h, rows, :]
                for t in range(ntile):
                    p = jnp.exp2(s_sc[h, rows, LANES * t:LANES * (t + 1)] - m_new)
                    l_new = l_new + p
                    p_sc[h, rows, LANES * t:LANES * (t + 1)] = p.astype(BF16)
                l_sc[h, rows, :] = l_new
                m_sc[h, rows, :] = m_new
                a_sc[h, rows, :] = alpha
        for h in range(HKV_C):
            pv = jnp.dot(p_sc[h], v_ref[0, pl.ds(r0, kv_tile), DH_C * h:DH_C * (h + 1)],
                         preferred_element_type=F32)
            acc_sc[h] = a_sc[h] * acc_sc[h] + pv
        return carry

    lax.fori_loop(0, nchunk, body, 0)
    for j in range(HQ_C):
        h, rows = j // g_heads, slice(q_tile * (j % g_heads), q_tile * (j % g_heads + 1))
        o = acc_sc[h, rows, :] / l_sc[h, rows, :].sum(axis=1, keepdims=True)
        o_ref[0, :, DH_C * j:DH_C * (j + 1)] = o.astype(BF16)


def _attn_c(q, k, v, n_ctx):
    b, t, nq = q.shape
    nk = k.shape[2]
    qt = GLOBAL_Q_TILE
    t0 = n_ctx // qt
    rows = HQ_C // HKV_C * qt
    kv_tile = max(w for w in range(LANES, KV_TILE_MAX + 1, LANES) if t % w == 0)
    return pl.pallas_call(
        _attn_c_kernel,
        grid=(b, t // qt - t0),
        in_specs=[pl.BlockSpec((1, qt, nq), lambda bb, i: (bb, i + t0, 0)),
                  pl.BlockSpec((1, t, nk), lambda bb, i: (bb, 0, 0)),
                  pl.BlockSpec((1, t, nk), lambda bb, i: (bb, 0, 0))],
        out_specs=pl.BlockSpec((1, qt, nq), lambda bb, i: (bb, i, 0)),
        out_shape=jax.ShapeDtypeStruct((b, t - n_ctx, nq), BF16),
        scratch_shapes=[pltpu.VMEM((HKV_C, rows, kv_tile), F32), pltpu.VMEM((HKV_C, rows, kv_tile), BF16),
                        pltpu.VMEM((HKV_C, rows, LANES), F32), pltpu.VMEM((HKV_C, rows, LANES), F32),
                        pltpu.VMEM((HKV_C, rows, LANES), F32), pltpu.VMEM((HKV_C, rows, DH_C), F32)],
        compiler_params=_params("parallel", "arbitrary"),
        name="attn_global",
    )(q, k, v)


def _route_group(aff_ref, row0, n, cap, gsel_ref, cex_ref, idx_ref, sel_sc, cin_sc):
    ngrp = LANES // N_EXPERTS
    rows = n // ngrp
    aff_rows = lambda g: aff_ref[0, row0 + rows * g:row0 + rows * (g + 1), :]
    packed = aff_rows(0)
    for g in range(1, ngrp):
        packed = packed + pltpu.roll(aff_rows(g), N_EXPERTS * g, 1)
    lane_p = lax.broadcasted_iota(jnp.int32, (rows, LANES), 1)
    tok = lax.broadcasted_iota(jnp.int32, (rows, LANES), 0) + rows * (lane_p // N_EXPERTS)

    def count(mask):
        c = jnp.broadcast_to(jnp.sum(jnp.where(mask, 1.0, 0.0), axis=0, keepdims=True), (SUBLANES, LANES))
        shift = N_EXPERTS
        while shift < LANES:
            c = c + pltpu.roll(c, shift, 1)
            shift *= 2
        return c[0:1, :]

    def bisect(steps, lo, hi, enough):
        def body(_, c):
            lo_, hi_ = c
            mid = lo_ + ((hi_ - lo_) >> 1)
            ok = enough(mid)
            return jnp.where(ok, mid, lo_), jnp.where(ok, hi_, mid)
        return lax.fori_loop(0, steps, body, (lo, hi))

    as_float = lambda bits: pltpu.bitcast(bits, F32)
    one_bits = 0x3F800001
    lo, hi = bisect(31, jnp.zeros((1, LANES), jnp.int32), jnp.full((1, LANES), one_bits, jnp.int32),
                    lambda mid: count(packed >= as_float(mid)) >= cap)
    thr, nxt = as_float(lo), as_float(hi)
    above = packed >= nxt
    need = cap - count(above)
    tie_tok = jnp.where(packed >= thr, jnp.where(above, n, tok), n)
    cut, _ = bisect(n.bit_length(), jnp.zeros((1, LANES), jnp.int32), jnp.full((1, LANES), n, jnp.int32),
                    lambda mid: count(tie_tok < mid) < need)
    sel_p = jnp.where(above | (tie_tok <= cut), 1.0, 0.0)
    for g in range(ngrp):
        sel_g = sel_p if g == 0 else pltpu.roll(sel_p, LANES - N_EXPERTS * g, 1)
        sel_g = jnp.where(lane_p < N_EXPERTS, sel_g, 0.0)
        sel_sc[rows * g:rows * (g + 1), :] = sel_g
        gsel_ref[0, row0 + rows * g:row0 + rows * (g + 1), :] = sel_g * aff_rows(g)

    blk = min(n, 2 * LANES)
    r_i = lax.broadcasted_iota(jnp.int32, (blk, blk), 0)
    c_i = lax.broadcasted_iota(jnp.int32, (blk, blk), 1)
    tri = (r_i >= c_i).astype(BF16)

    def cum_body(bi, carry):
        r0 = pl.multiple_of(bi * blk, blk)
        s_blk = sel_sc[pl.ds(r0, blk), :]
        c_blk = jnp.dot(tri, s_blk.astype(BF16), preferred_element_type=F32) + carry
        cin_sc[pl.ds(r0, blk), :] = c_blk
        cex_ref[0, pl.ds(pl.multiple_of(row0 + r0, SUBLANES), blk), :] = (c_blk - s_blk).astype(jnp.int32)
        return c_blk[blk - 1:blk, :]

    lax.fori_loop(0, n // blk, cum_body, jnp.zeros((1, LANES), F32))

    if len(idx_ref.shape) == 3:
        slot = lax.broadcasted_iota(jnp.int32, (1, cap), 1).astype(F32)
        for e in range(N_EXPERTS):
            def idx_body(bi, acc):
                r0 = pl.multiple_of(bi * blk, blk)
                col = cin_sc[pl.ds(r0, blk), e:e + 1]
                return acc + jnp.sum(jnp.where(col <= slot, 1.0, 0.0), axis=0, keepdims=True)
            acc = lax.fori_loop(0, n // blk, idx_body, jnp.zeros((1, cap), F32))
            idx_ref[0, e:e + 1, :] = acc.astype(jnp.int32)
        return

    nh = cap // SLOT_DIGIT
    jh = lax.broadcasted_iota(jnp.int32, (nh, blk), 0).astype(F32)
    jl = lax.broadcasted_iota(jnp.int32, (blk, SLOT_DIGIT), 1).astype(F32)
    ones = jnp.ones((blk, SLOT_DIGIT), BF16)
    one_if = lambda m: jnp.where(m, 1.0, 0.0).astype(BF16)

    def table_body(bi, tables):
        r0 = pl.multiple_of(bi * blk, blk)
        c = cin_sc[pl.ds(r0, blk), :]
        hi = jnp.floor(c * (1.0 / SLOT_DIGIT))
        lo = c - SLOT_DIGIT * hi
        hi_t = hi.T
        out = []
        for e in range(N_EXPERTS):
            h_row = hi_t[e:e + 1, :]
            out.append(tables[e]
                       + jnp.dot(one_if(h_row == jh), one_if(lo[:, e:e + 1] <= jl), preferred_element_type=F32)
                       + jnp.dot(one_if(h_row < jh), ones, preferred_element_type=F32))
        return tuple(out)

    tables = lax.fori_loop(0, n // blk, table_body,
                           tuple(jnp.zeros((nh, SLOT_DIGIT), F32) for _ in range(N_EXPERTS)))
    for e in range(N_EXPERTS):
        idx_ref[0, e] = tables[e].astype(jnp.int32)


def _route_kernel(aff_ref, gsel_ref, cex_ref, *rest, groups):
    idx_refs, (sel_sc, cin_sc) = rest[:len(groups)], rest[len(groups):]
    for (row0, n, cap), idx_ref in zip(groups, idx_refs):
        _route_group(aff_ref, row0, n, cap, gsel_ref, cex_ref, idx_ref, sel_sc, cin_sc)


def _route(aff, groups):
    b, tl, _ = aff.shape
    blk = pl.BlockSpec((1, tl, LANES), lambda bb: (bb, 0, 0))
    nmax = max(n for _, n, _ in groups)
    list_shape = lambda cap: ((cap // SLOT_DIGIT, SLOT_DIGIT) if cap >= LIST_TABLE_MIN and cap % SLOT_DIGIT == 0
                              else (cap,))
    outs = pl.pallas_call(
        functools.partial(_route_kernel, groups=groups),
        grid=(b,),
        in_specs=[blk],
        out_specs=[blk, blk] + [pl.BlockSpec((1, N_EXPERTS) + list_shape(cap),
                                             lambda bb, nd=len(list_shape(cap)): (bb,) + (0,) * (nd + 1))
                                for _, _, cap in groups],
        out_shape=[jax.ShapeDtypeStruct((b, tl, LANES), F32), jax.ShapeDtypeStruct((b, tl, LANES), jnp.int32)]
        + [jax.ShapeDtypeStruct((b, N_EXPERTS) + list_shape(cap), jnp.int32) for _, _, cap in groups],
        scratch_shapes=[pltpu.VMEM((nmax, LANES), F32), pltpu.VMEM((nmax, LANES), F32)],
        compiler_params=_params("parallel"),
        name="route",
    )(aff)
    return list(outs[:2]) + [idx.reshape(b, N_EXPERTS, cap) for idx, (_, _, cap) in zip(outs[2:], groups)]


def _expert_ffn_kernel(idx_ref, h_hbm, wg_ref, wu_ref, wd_ref, y_ref, xbuf, xb, sem, *, nrows, nsteps):
    e = pl.program_id(0)
    f = pl.program_id(1)

    ne, nf = pl.num_programs(0), pl.num_programs(1)

    def row_copy(ee, j):
        return pltpu.make_async_copy(h_hbm.at[pl.ds(idx_ref[ee * nrows + j], 1), :],
                                     xbuf.at[pl.ds(j, 1), :], sem.at[0])

    def wait_rows(n):
        pltpu.make_async_copy(h_hbm.at[pl.ds(0, n), :], xbuf.at[pl.ds(0, n), :], sem.at[0]).wait()

    @pl.when(f == 0)
    def _():
        @pl.when(e == 0)
        def _():
            def body(j, c):
                row_copy(0, j).start()
                return c
            lax.fori_loop(0, nrows, body, 0, unroll=GATHER_UNROLL)

        wait_rows(nrows)
        xb[...] = xbuf[...].astype(BF16)

    per_step = nrows // nsteps
    nxt = jnp.minimum(e + 1, ne - 1)
    for jj in range(per_step):
        row_copy(nxt, f * per_step + jj).start()

    wg = wg_ref[0, 0].astype(BF16)
    wu = wu_ref[0, 0].astype(BF16)
    wd = wd_ref[0, 0].astype(BF16)
    rc = nrows // FFN_ROW_CHUNKS
    for ci in range(FFN_ROW_CHUNKS):
        rows = slice(rc * ci, rc * (ci + 1))
        x = xb[rows, :]
        g = jnp.dot(x, wg, preferred_element_type=F32)
        u = jnp.dot(x, wu, preferred_element_type=F32)
        part = jnp.dot((_silu(g) * u).astype(BF16), wd, preferred_element_type=F32)

        @pl.when(f == 0)
        def _():
            y_ref[0, rows, :] = part

        @pl.when(f > 0)
        def _():
            y_ref[0, rows, :] += part

    @pl.when((e == ne - 1) & (f == nf - 1))
    def _():
        wait_rows(nrows)


def _expert_ffn(idx_flat, h_flat, w_gate, w_up, w_down, layer, nrows):
    _, e, d, ff = w_gate.shape
    nf = ff // FF_TILE
    return pl.pallas_call(
        functools.partial(_expert_ffn_kernel, nrows=nrows, nsteps=nf),
        grid_spec=pltpu.PrefetchScalarGridSpec(
            num_scalar_prefetch=1,
            grid=(e, nf),
            in_specs=[pl.BlockSpec(memory_space=pl.ANY),
                      pl.BlockSpec((1, 1, d, FF_TILE), lambda ee, f, idx: (layer, ee, 0, f)),
                      pl.BlockSpec((1, 1, d, FF_TILE), lambda ee, f, idx: (layer, ee, 0, f)),
                      pl.BlockSpec((1, 1, FF_TILE, d), lambda ee, f, idx: (layer, ee, f, 0))],
            out_specs=pl.BlockSpec((1, nrows, d), lambda ee, f, idx: (ee, 0, 0),
                                   pipeline_mode=pl.Buffered(1)),
            scratch_shapes=[pltpu.VMEM((nrows, d), F32), pltpu.VMEM((nrows, d), BF16),
                            pltpu.SemaphoreType.DMA((1,))]),
        out_shape=jax.ShapeDtypeStruct((e, nrows, d), F32),
        compiler_params=_params("arbitrary", "arbitrary"),
        name="expert_ffn",
    )(idx_flat, h_flat, w_gate, w_up, w_down)


def _combine_kernel(wide_ref, start_n_ref, start_w_ref, y_hbm, xn_ref, mod_ref, cex_ref, gsel_ref,
                    delta_ref, o_ref, stage, sem):
    nt = pl.num_programs(1)
    step = pl.program_id(0) * nt + pl.program_id(1)
    nsteps = pl.num_programs(0) * nt
    slot = step % 2

    def copies(step_, slot_, start_ref, win):
        return [pltpu.make_async_copy(
            y_hbm.at[pl.ds(pl.multiple_of(start_ref[step_ * N_EXPERTS + e], SUBLANES), win), :],
            stage.at[slot_, pl.ds(e * win, win), :], sem.at[slot_]) for e in range(N_EXPERTS)]

    def by_width(step_, fn):
        @pl.when(wide_ref[step_] == 0)
        def _():
            fn(start_n_ref, WINDOW_NARROW, 0)

        @pl.when(wide_ref[step_] != 0)
        def _():
            fn(start_w_ref, WINDOW_WIDE, 1)

    def start(step_, slot_):
        by_width(step_, lambda ref, win, _: [cp.start() for cp in copies(step_, slot_, ref, win)])

    @pl.when(step == 0)
    def _():
        start(0, 0)

    @pl.when(step + 1 < nsteps)
    def _():
        start(step + 1, 1 - slot)

    gate = gsel_ref[0]
    cex = cex_ref[0]
    tile = xn_ref.shape[1]

    def reduce(start_ref, win, delta_row):
        for cp in copies(step, slot, start_ref, win):
            cp.wait()
        rpos = cex + delta_ref[0, 0, delta_row:delta_row + 1, :]
        acc = None
        for c0 in range(0, N_EXPERTS * win, LANES):
            lane_r = c0 + lax.broadcasted_iota(jnp.int32, (tile, LANES), 1)
            q = jnp.zeros((tile, LANES), F32)
            for e in range(c0 // win, min(N_EXPERTS - 1, (c0 + LANES - 1) // win) + 1):
                q = jnp.where(rpos[:, e:e + 1] == lane_r, gate[:, e:e + 1], q)
            part = jnp.dot(q.astype(BF16), stage[slot, c0:c0 + LANES, :].astype(BF16),
                           preferred_element_type=F32)
            acc = part if acc is None else acc + part
        o_ref[0] = xn_ref[0] + mod_ref[0, 5:6, :] * acc

    by_width(step, reduce)


def _combine(wide, start_n, start_w, y_flat, xn, mod, cex, gsel, delta, n_ctx_tiles):
    b, tl, d = xn.shape
    nt = tl // MOE_TILE
    tok = lambda w_: pl.BlockSpec((1, MOE_TILE, w_), lambda bb, i, *_: (bb, i, 0))
    return pl.pallas_call(
        _combine_kernel,
        grid_spec=pltpu.PrefetchScalarGridSpec(
            num_scalar_prefetch=3,
            grid=(b, nt),
            in_specs=[pl.BlockSpec(memory_space=pl.ANY), tok(d),
                      pl.BlockSpec((1, 6, d), lambda bb, i, *_: (jnp.where(i < n_ctx_tiles, 0, 1 + bb), 0, 0)),
                      tok(LANES), tok(LANES),
                      pl.BlockSpec((1, 1, SUBLANES, LANES), lambda bb, i, *_: (bb, i, 0, 0))],
            out_specs=tok(d),
            scratch_shapes=[pltpu.VMEM((2, N_EXPERTS * WINDOW_WIDE, d), F32),
                            pltpu.SemaphoreType.DMA((2,))]),
        out_shape=jax.ShapeDtypeStruct((b, tl, d), F32),
        compiler_params=_params("arbitrary", "arbitrary"),
        name="moe_combine",
    )(wide, start_n, start_w, y_flat, xn, mod, cex, gsel, delta)


def _moe(h2, aff, xn, mod, groups, w_gate, w_up, w_down, layer):
    b, tl, d = h2.shape
    ne = w_gate.shape[1]
    caps = [max(1, CAP_FACTOR * n // ne) for _, n in groups]
    nrows = sum(b * cap for cap in caps)
    assert all(n % MOE_TILE == 0 for _, n in groups) and all(cap % SUBLANES == 0 for cap in caps)
    gsel, cex, *idxs = _route(aff, [(row0, n, cap) for (row0, n), cap in zip(groups, caps)])
    idx_parts, src_parts, cnt_parts = [], [], []
    base = 0
    batch = jnp.arange(b, dtype=jnp.int32)
    for (row0, n), cap, idx in zip(groups, caps, idxs):
        rows = idx + (batch * tl + row0)[:, None, None]
        idx_parts.append(jnp.swapaxes(rows, 0, 1).reshape(ne, b * cap))
        s0 = cex[:, row0:row0 + n:MOE_TILE, :ne]
        ends = jnp.concatenate([s0[:, 1:], jnp.full((b, 1, ne), cap, jnp.int32)], axis=1)
        first = (base + batch * cap)[:, None, None] + (jnp.arange(ne, dtype=jnp.int32) * nrows)[None, None, :]
        src_parts.append(jnp.stack([s0 + first, jnp.broadcast_to(first, s0.shape)], axis=0))
        cnt_parts.append(ends - s0)
        base += b * cap
    cat = lambda parts, axis: parts[0] if len(parts) == 1 else jnp.concatenate(parts, axis=axis)
    src, first = cat(src_parts, 2)
    cnt = cat(cnt_parts, 1)
    wide = (cnt > WINDOW_NARROW - (SUBLANES - 1)).any(axis=-1).astype(jnp.int32)
    starts, deltas = [], []
    for win in (WINDOW_NARROW, WINDOW_WIDE):
        st = jnp.minimum(src // SUBLANES * SUBLANES, ne * nrows - win)
        starts.append(st.reshape(-1))
        deltas.append(jnp.arange(ne, dtype=jnp.int32) * win + first - st)
    delta = jnp.pad(jnp.stack(deltas, axis=2), ((0, 0), (0, 0), (0, SUBLANES - 2), (0, LANES - ne)))
    y = _expert_ffn(cat(idx_parts, 1).reshape(-1), h2.reshape(b * tl, d), w_gate, w_up, w_down, layer, nrows)
    return _combine(wide.reshape(-1), starts[0], starts[1], y.reshape(ne * nrows, d), xn, mod, cex, gsel,
                    delta, groups[0][1] // MOE_TILE if len(groups) > 1 else 0)


def _rope_tables(s_len, n_ctx, head_dim):
    quarter = head_dim // 4
    t = jnp.arange(s_len)
    row = (t // GRID_W).astype(F32)
    col = (t % GRID_W).astype(F32)
    inv = ROPE_THETA ** (-jnp.arange(quarter, dtype=F32) / quarter)
    ar, ac = row[:, None] * inv, col[:, None] * inv
    cos = jnp.concatenate([jnp.cos(ar), jnp.cos(ar), jnp.cos(ac), jnp.cos(ac)], axis=1)
    sin = jnp.concatenate([-jnp.sin(ar), jnp.sin(ar), -jnp.sin(ac), jnp.sin(ac)], axis=1)
    reps = LANES // head_dim
    cos, sin = jnp.tile(cos, (1, reps)), jnp.tile(sin, (1, reps))
    cos = jnp.concatenate([jnp.ones((n_ctx, LANES), F32), cos], axis=0)
    sin = jnp.concatenate([jnp.zeros((n_ctx, LANES), F32), sin], axis=0)
    return cos, sin


def _even_weight(w):
    nq, nkv = HQ_A * DH_A, HKV_A * DH_A
    q = w[:, :nq]
    dup = lambda m: jnp.concatenate(
        [m[:, DH_A * (h // 2):DH_A * (h // 2 + 1)] for h in range(2 * HKV_A)], axis=1)
    k = dup(w[:, nq:nq + nkv])
    v = dup(w[:, nq + nkv:nq + 2 * nkv])
    o = nq + 2 * nkv
    nconv, nz = 3 * H_B * DK_B, H_B * DK_B
    conv = w[:, o:o + nconv]
    z = w[:, o + nconv:o + nconv + nz]
    ab = w[:, o + nconv + nz:]
    ab = jnp.pad(ab, ((0, 0), (0, LANES - ab.shape[1])))
    return jnp.concatenate([q, k, v, conv, z, ab], axis=1).astype(BF16)


def _lane_vec(v):
    v = v.reshape(1, -1)
    return jnp.pad(v, ((0, 0), (0, LANES - v.shape[1])))


def kernel(x, c, ctx, c_ctx, w_mod, b_mod, norm_mix, norm_ffn, w_in_ab, w_out_ab, qnorm_a, knorm_a,
           sink_a, conv_b, a_log_b, dt_bias_b, onorm_b, w_in_c, w_out_c, qnorm_c, knorm_c,
           w_router, w_gate, w_up, w_down):
    b, s_len, d = x.shape
    n_ctx = ctx.shape[1]
    depth = w_mod.shape[0]
    assert b + 1 <= SUBLANES and n_ctx % ROW_TILE == 0 and s_len % ROW_TILE == 0
    t_all = n_ctx + s_len

    cvec = jnp.concatenate([c_ctx[None], c, jnp.zeros((SUBLANES - 1 - b, d), F32)], axis=0)
    mod = _modulation(cvec, w_mod, b_mod).reshape(depth, SUBLANES, 6, d)
    cos_a, sin_a = _rope_tables(s_len, n_ctx, DH_A)
    cos_c, sin_c = _rope_tables(s_len, n_ctx, DH_C)
    nq_a = HQ_A * DH_A
    seg = jnp.arange(nq_a) // DH_A
    ones_bd = (seg[:, None] == seg[None, :]).astype(BF16)

    xs = jnp.concatenate([ctx, x], axis=1)
    for i in range(depth):
        last = i == depth - 1
        j = i // 2
        gain1 = norm_mix[i].reshape(1, d)
        gain2 = norm_ffn[i].reshape(1, d)
        w_r = jnp.pad(w_router[i], ((0, 0), (0, LANES - N_EXPERTS)))
        w_r_hi = w_r.astype(BF16)
        w_r = jnp.stack([w_r_hi, (w_r - w_r_hi.astype(F32)).astype(BF16)])
        if i % 2 == 0:
            q, k, v, pc, z, ab = _inproj_even(
                xs, mod[i], gain1, _even_weight(w_in_ab[j]), cos_a, sin_a,
                jnp.tile(qnorm_a[j], HQ_A).reshape(1, -1), jnp.tile(knorm_a[j], 2 * HKV_A).reshape(1, -1),
                ones_bd, n_ctx)
            oa = _attn_a(sink_a[j], q, k, v, n_ctx)
            qb, kb, vb, gb = _gdn_prep(pc, conv_b[j], ab, _lane_vec(a_log_b[j]), _lane_vec(dt_bias_b[j]),
                                       n_ctx)
            u, w, qg, a, kdt, ge = _gdn_chunk(qb, kb, vb, gb)
            o_fwd, o_bwd = _gdn_scan(u, w, qg, a, kdt, ge, n_ctx)
            tile0 = n_ctx // ROW_TILE if last else 0
            xn, h2, aff = _outproj_call(
                _outproj_even_kernel, "outproj_even",
                [(oa, None, tile0), (o_fwd, None, tile0), (o_bwd, None, tile0), (z, None, tile0)], xs, mod[i],
                [w_out_ab[j].astype(BF16), onorm_b[j].reshape(1, -1), gain2, w_r], n_ctx, tile0)
        else:
            q, k, v = _inproj_odd(xs, mod[i], gain1, w_in_c[j].astype(BF16), cos_c, sin_c,
                                  qnorm_c[j].reshape(1, -1), knorm_c[j].reshape(1, -1), n_ctx)
            if last:
                o = _attn_c(q, k, v, n_ctx)
                tile0 = n_ctx // ROW_TILE
            else:
                raise NotImplementedError("context queries of a non-final odd layer")
            xn, h2, aff = _outproj_call(
                _outproj_odd_kernel, "outproj_odd", [(o, None, 0)], xs, mod[i],
                [w_out_c[j].astype(BF16), gain2, w_r], n_ctx, tile0)
        groups = [(0, s_len)] if last else [(0, n_ctx), (n_ctx, s_len)]
        xs = _moe(h2, aff, xn, mod[i], groups, w_gate, w_up, w_down, i)
    return xs
```

```python
import functools

import jax
import jax.numpy as jnp
from jax import lax
from jax.experimental import pallas as pl
from jax.experimental.pallas import tpu as pltpu

F32 = jnp.float32
BF16 = jnp.bfloat16
HIGHEST = lax.Precision.HIGHEST

GRID_W = 64
NORM_EPS = 1e-6
ROPE_THETA = 10000.0
NEG_INF = -1e30
HQ_A, HKV_A, DH_A, WINDOW = 8, 2, 64, 128
H_B, DK_B, CONV_K, CHUNK = 4, 128, 5, 64
INV_BASE = 8
HQ_C, HKV_C, DH_C = 8, 2, 128
N_EXPERTS, CAP_FACTOR = 16, 2

LANES = 128
SUBLANES = 8
VMEM_LIMIT = 56 * 2 ** 20

ROW_TILE = 256
Q_TILE = 128
GLOBAL_Q_TILE = 256
KV_TILE_MAX = 1408
SOFTMAX_ROWS = 64
LOG2_E = 1.4426950408889634
FF_TILE = 512
FFN_ROW_CHUNKS = 2
MOE_TILE = 128
WINDOW_NARROW = 40
WINDOW_WIDE = MOE_TILE + SUBLANES
GATHER_UNROLL = 8
SLOT_DIGIT = 32
LIST_TABLE_MIN = 256


def _params(*sem):
    return pltpu.CompilerParams(dimension_semantics=sem, vmem_limit_bytes=VMEM_LIMIT)


def _silu(x):
    return x * (1.0 / (1.0 + jnp.exp(-x)))


def _sigmoid(x):
    return 1.0 / (1.0 + jnp.exp(-x))


def _norm_mod(x, gain, shift, scale):
    ms = jnp.mean(x * x, axis=-1, keepdims=True)
    y = x * lax.rsqrt(ms + NORM_EPS) * gain
    return y * (1.0 + scale) + shift


def _segment_mean_square(p, ones_bd, seg):
    sq = p * p
    hi = sq.astype(BF16)
    lo = (sq - hi.astype(F32)).astype(BF16)
    s = (jnp.dot(hi, ones_bd, preferred_element_type=F32)
         + jnp.dot(lo, ones_bd, preferred_element_type=F32))
    return s * (1.0 / seg)


def _rope(x, cos, sin_signed, dist):
    n = x.shape[-1]
    lane = lax.broadcasted_iota(jnp.int32, x.shape, 1)
    up = pltpu.roll(x, n - dist, 1)
    dn = pltpu.roll(x, dist, 1)
    partner = jnp.where((lane & dist) == 0, up, dn)
    return x * cos + partner * sin_signed


def _mod_kernel(c_ref, w_ref, b_ref, o_ref):
    s = _silu(c_ref[...])
    o_ref[0] = jnp.dot(s, w_ref[0], precision=HIGHEST, preferred_element_type=F32) + b_ref[0]


def _modulation(cvec, w_mod, b_mod):
    depth, d, n6 = w_mod.shape
    tn = 1536
    return pl.pallas_call(
        _mod_kernel,
        grid=(depth, n6 // tn),
        in_specs=[pl.BlockSpec((SUBLANES, d), lambda l, j: (0, 0)),
                  pl.BlockSpec((1, d, tn), lambda l, j: (l, 0, j)),
                  pl.BlockSpec((1, 1, tn), lambda l, j: (l, 0, j))],
        out_specs=pl.BlockSpec((1, SUBLANES, tn), lambda l, j: (l, 0, j)),
        out_shape=jax.ShapeDtypeStruct((depth, SUBLANES, n6), F32),
        compiler_params=_params("parallel", "parallel"),
        name="modulation",
    )(cvec, w_mod, b_mod.reshape(depth, 1, n6))


def _mod_spec(n_ctx_tiles, tile0=0):
    return lambda b, i: (jnp.where(i + tile0 < n_ctx_tiles, 0, 1 + b), 0, 0)


def _inproj_even_kernel(x_ref, mod_ref, gain_ref, w_ref, cos_ref, sin_ref, qn_ref, kn_ref, ones_ref,
                        q_ref, k_ref, v_ref, pc_ref, z_ref, ab_ref):
    h = _norm_mod(x_ref[0], gain_ref[...], mod_ref[0, 0:1, :], mod_ref[0, 1:2, :]).astype(BF16)

    def proj(lo, hi):
        return jnp.dot(h, w_ref[:, lo:hi], preferred_element_type=F32)

    c = cos_ref[...]
    s = sin_ref[...]
    nq, nk = HQ_A * DH_A, 2 * HKV_A * DH_A
    q = proj(0, nq)
    q = q * lax.rsqrt(_segment_mean_square(q, ones_ref[...], DH_A) + NORM_EPS) * qn_ref[...]
    q = _rope(q, jnp.concatenate([c] * (nq // LANES), axis=1),
              jnp.concatenate([s] * (nq // LANES), axis=1), DH_A // 4)
    q_ref[0] = (q * (DH_A ** -0.5 * LOG2_E)).astype(BF16)
    k = proj(nq, nq + nk)
    k = k * lax.rsqrt(_segment_mean_square(k, ones_ref[0:nk, 0:nk], DH_A) + NORM_EPS) * kn_ref[...]
    k = _rope(k, jnp.concatenate([c] * (nk // LANES), axis=1),
              jnp.concatenate([s] * (nk // LANES), axis=1), DH_A // 4)
    k_ref[0] = k.astype(BF16)
    o = nq + nk
    v_ref[0] = proj(o, o + nk).astype(BF16)
    o += nk
    nconv = pc_ref.shape[2]
    pc_ref[0] = proj(o, o + nconv)
    o += nconv
    nz = z_ref.shape[2]
    z_ref[0] = proj(o, o + nz)
    o += nz
    ab_ref[0] = proj(o, o + LANES)


def _inproj_even(xs, mod, gain, w, cos, sin, qn, kn, ones_bd, n_ctx):
    b, t, d = xs.shape
    tm = ROW_TILE
    nq, nk = HQ_A * DH_A, 2 * HKV_A * DH_A
    nconv, nz = 3 * H_B * DK_B, H_B * DK_B
    row = lambda w_: pl.BlockSpec((1, tm, w_), lambda bb, i: (bb, i, 0))
    const = lambda a: pl.BlockSpec(a.shape, lambda bb, i: (0,) * a.ndim)
    return pl.pallas_call(
        _inproj_even_kernel,
        grid=(b, t // tm),
        in_specs=[row(d), pl.BlockSpec((1, 6, d), _mod_spec(n_ctx // tm)), const(gain), const(w),
                  pl.BlockSpec((tm, LANES), lambda bb, i: (i, 0)),
                  pl.BlockSpec((tm, LANES), lambda bb, i: (i, 0)),
                  const(qn), const(kn), const(ones_bd)],
        out_specs=[row(nq), row(nk), row(nk), row(nconv), row(nz), row(LANES)],
        out_shape=[jax.ShapeDtypeStruct((b, t, nq), BF16), jax.ShapeDtypeStruct((b, t, nk), BF16),
                   jax.ShapeDtypeStruct((b, t, nk), BF16), jax.ShapeDtypeStruct((b, t, nconv), F32),
                   jax.ShapeDtypeStruct((b, t, nz), F32), jax.ShapeDtypeStruct((b, t, LANES), F32)],
        compiler_params=_params("parallel", "parallel"),
        name="inproj_even",
    )(xs, mod, gain, w, cos, sin, qn, kn, ones_bd)


def _attn_a_kernel(sink_ref, q_ref, k_ref, v_ref, o_ref, s_sc, p_sc, r_sc, *, n_ctx, t_all):
    i = pl.program_id(1)
    g_heads = HQ_A // HKV_A
    band = 3 * Q_TILE
    n = i - n_ctx // Q_TILE
    start = jnp.clip(n_ctx + (n - 1) * Q_TILE, 0, t_all - band)
    start = pl.multiple_of(start, Q_TILE)
    q = q_ref[0]
    rows = g_heads * Q_TILE
    lane = lax.broadcasted_iota(jnp.int32, (Q_TILE, LANES), 1)
    kj = lax.broadcasted_iota(jnp.int32, (rows, band), 1)
    rel = kj - (lax.broadcasted_iota(jnp.int32, (rows, band), 0) & (Q_TILE - 1))
    first = start - n_ctx
    off = first - n * Q_TILE
    valid = (n >= 0) & (rel >= -WINDOW - off) & (rel <= WINDOW - off) & (kj >= -first)
    dims = (((1,), (1,)), ((), ()))
    nkeys = n_ctx + band
    for h in range(HKV_A):
        cols = slice(LANES * h, LANES * (h + 1))
        parts = []
        for g in range(g_heads):
            j = g_heads * h + g
            tile = q[:, LANES * (j // 2):LANES * (j // 2 + 1)]
            keep = (lane >= DH_A * (j % 2)) & (lane < DH_A * (j % 2 + 1))
            parts.append(jnp.where(keep, tile, jnp.zeros_like(tile)))
        qs = jnp.concatenate(parts, axis=0)
        s_sc[h, :, 0:n_ctx] = lax.dot_general(qs, k_ref[0, 0:n_ctx, cols], dims,
                                              preferred_element_type=F32)
        s_b = lax.dot_general(qs, k_ref[0, pl.ds(start, band), cols], dims, preferred_element_type=F32)
        s_sc[h, :, n_ctx:nkeys] = jnp.where(valid, s_b, NEG_INF)
    for h in range(HKV_A):
        for rb in range(rows // SOFTMAX_ROWS):
            blk = slice(SOFTMAX_ROWS * rb, SOFTMAX_ROWS * (rb + 1))
            sink = sink_ref[g_heads * h + SOFTMAX_ROWS * rb // Q_TILE] * LOG2_E
            tiles = [slice(LANES * t, LANES * (t + 1)) for t in range(nkeys // LANES)]
            mx = s_sc[h, blk, tiles[0]]
            for t in tiles[1:]:
                mx = jnp.maximum(mx, s_sc[h, blk, t])
            m = jnp.maximum(mx.max(axis=1, keepdims=True), sink)
            m_b = jnp.broadcast_to(m, (SOFTMAX_ROWS, LANES))
            l_run = jnp.zeros((SOFTMAX_ROWS, LANES), F32)
            for t in tiles:
                p = jnp.exp2(s_sc[h, blk, t] - m_b)
                l_run = l_run + p
                p_sc[h, blk, t] = p.astype(BF16)
            den = l_run.sum(axis=1, keepdims=True) + jnp.exp2(sink - m)
            r_sc[h, blk, :] = jnp.broadcast_to(1.0 / den, (SOFTMAX_ROWS, LANES))
    for h in range(HKV_A):
        cols = slice(LANES * h, LANES * (h + 1))
        o = (jnp.dot(p_sc[h, :, 0:n_ctx], v_ref[0, 0:n_ctx, cols], preferred_element_type=F32)
             + jnp.dot(p_sc[h, :, n_ctx:nkeys], v_ref[0, pl.ds(start, band), cols],
                       preferred_element_type=F32)) * r_sc[h]
        for pair in range(g_heads // 2):
            lo = o[(2 * pair) * Q_TILE:(2 * pair + 1) * Q_TILE]
            hi = o[(2 * pair + 1) * Q_TILE:(2 * pair + 2) * Q_TILE]
            c0 = LANES * (g_heads // 2 * h + pair)
            o_ref[0, :, c0:c0 + LANES] = jnp.where(lane < DH_A, lo, hi).astype(BF16)


def _attn_a(sink, q, k, v, n_ctx):
    b, t, nq = q.shape
    nk = k.shape[2]
    rows, nkeys = HQ_A // HKV_A * Q_TILE, n_ctx + 3 * Q_TILE
    return pl.pallas_call(
        functools.partial(_attn_a_kernel, n_ctx=n_ctx, t_all=t),
        grid=(b, t // Q_TILE),
        in_specs=[pl.BlockSpec(memory_space=pltpu.SMEM),
                  pl.BlockSpec((1, Q_TILE, nq), lambda bb, i: (bb, i, 0)),
                  pl.BlockSpec((1, t, nk), lambda bb, i: (bb, 0, 0)),
                  pl.BlockSpec((1, t, nk), lambda bb, i: (bb, 0, 0))],
        out_specs=pl.BlockSpec((1, Q_TILE, nq), lambda bb, i: (bb, i, 0)),
        out_shape=jax.ShapeDtypeStruct((b, t, nq), BF16),
        scratch_shapes=[pltpu.VMEM((HKV_A, rows, nkeys), F32), pltpu.VMEM((HKV_A, rows, nkeys), BF16),
                        pltpu.VMEM((HKV_A, rows, LANES), F32)],
        compiler_params=_params("parallel", "arbitrary"),
        name="attn_window",
    )(sink, q, k, v)


def _gdn_prep_kernel(pc_ref, prev_ref, next_ref, cw_ref, ab_ref, alog_ref, dtb_ref,
                     q_ref, k_ref, v_ref, gb_ref, ext_sc, *, n_ctx, t_all):
    tm = pc_ref.shape[1]
    r0 = pl.program_id(1) * tm
    halo = SUBLANES
    prev_on = jnp.where((r0 == 0) | (r0 == n_ctx), 0.0, 1.0)
    next_on = jnp.where((r0 + tm == n_ctx) | (r0 + tm == t_all), 0.0, 1.0)
    ext_sc[0:halo, :] = prev_ref[0] * prev_on
    ext_sc[halo:halo + tm, :] = pc_ref[0]
    ext_sc[halo + tm:2 * halo + tm, :] = next_ref[0] * next_on
    nh = H_B * DK_B
    for grp, out_ref in enumerate((q_ref, k_ref, v_ref)):
        c0 = nh * grp
        acc = None
        for tap in range(CONV_K):
            off = halo - CONV_K // 2 + tap
            term = cw_ref[tap:tap + 1, c0:c0 + nh] * ext_sc[off:off + tm, c0:c0 + nh]
            acc = term if acc is None else acc + term
        y = _silu(acc)
        if grp == 2:
            out_ref[0] = y
            continue
        scale = DK_B ** -0.5 if grp == 0 else 1.0
        for h in range(H_B):
            yh = y[:, DK_B * h:DK_B * (h + 1)]
            inv = lax.rsqrt(jnp.sum(yh * yh, axis=-1, keepdims=True) + NORM_EPS)
            out_ref[0, :, DK_B * h:DK_B * (h + 1)] = yh * (inv * scale)
    ab = ab_ref[0]
    lane = lax.broadcasted_iota(jnp.int32, ab.shape, 1)
    xg = ab + dtb_ref[...]
    softplus = jnp.maximum(xg, 0.0) + jnp.log(1.0 + jnp.exp(-jnp.abs(xg)))
    g = -jnp.exp(alog_ref[...]) * softplus
    gb_ref[0] = jnp.where(lane < 2 * H_B, g, jnp.where(lane < 4 * H_B, _sigmoid(ab), 0.0))


def _gdn_prep(pc, conv_w, ab, alog, dtb, n_ctx):
    b, t, nconv = pc.shape
    tm = ROW_TILE
    nh = H_B * DK_B
    hb = tm // SUBLANES
    nblk = t // SUBLANES
    row = lambda w_: pl.BlockSpec((1, tm, w_), lambda bb, i: (bb, i, 0))
    const = lambda a: pl.BlockSpec(a.shape, lambda bb, i: (0,) * a.ndim)
    return pl.pallas_call(
        functools.partial(_gdn_prep_kernel, n_ctx=n_ctx, t_all=t),
        grid=(b, t // tm),
        in_specs=[row(nconv),
                  pl.BlockSpec((1, SUBLANES, nconv), lambda bb, i: (bb, jnp.maximum(i * hb - 1, 0), 0)),
                  pl.BlockSpec((1, SUBLANES, nconv),
                               lambda bb, i: (bb, jnp.minimum((i + 1) * hb, nblk - 1), 0)),
                  const(conv_w), row(LANES), const(alog), const(dtb)],
        out_specs=[row(nh), row(nh), row(nh), row(LANES)],
        out_shape=[jax.ShapeDtypeStruct((b, t, nh), F32)] * 3 + [jax.ShapeDtypeStruct((b, t, LANES), F32)],
        scratch_shapes=[pltpu.VMEM((tm + 2 * SUBLANES, nconv), F32)],
        compiler_params=_params("parallel", "parallel"),
        name="gdn_prep",
    )(pc, pc, pc, conv_w, ab, alog, dtb)


def _gdn_chunk_kernel(q_ref, k_ref, v_ref, gb_ref, u_ref, w_ref, qg_ref, a_ref, kdt_ref, ge_ref):
    c = CHUNK
    nchunks = q_ref.shape[1] // c
    r_i = lax.broadcasted_iota(jnp.int32, (c, c), 0)
    c_i = lax.broadcasted_iota(jnp.int32, (c, c), 1)
    tri_lu = jnp.concatenate([(r_i >= c_i).astype(BF16), (r_i <= c_i).astype(BF16)], axis=0)

    def cumsums(g):
        total = None
        for _ in range(3):
            piece = g.astype(BF16)
            part = jnp.dot(tri_lu, piece, preferred_element_type=F32)
            total = part if total is None else total + part
            g = g - piece.astype(F32)
        return total[0:c], total[c:2 * c]
    row2 = lax.broadcasted_iota(jnp.int32, (c, LANES), 0)
    col2 = lax.broadcasted_iota(jnp.int32, (c, LANES), 1)
    colm = col2 & (c - 1)
    lane8 = lax.broadcasted_iota(jnp.int32, (1, LANES), 1)
    half_of = [(col2 // c) == (h % 2) for h in range(H_B)]
    eye_stack = jnp.concatenate(
        [jnp.where(half_of[h] & (row2 == colm), 1.0, 0.0) for h in range(H_B)], axis=0)
    stack = lambda m: jnp.concatenate([jnp.where(m, 1.0, 0.0)] * H_B, axis=0)
    base_mask = stack((row2 // INV_BASE) == (colm // INV_BASE))
    level_masks = [[], []]
    size = INV_BASE
    while size < c:
        same = (row2 // (2 * size)) == (colm // (2 * size))
        r_hi, c_hi = (row2 // size) % 2 == 1, (colm // size) % 2 == 1
        level_masks[0].append(stack(same & r_hi & ~c_hi))
        level_masks[1].append(stack(same & ~r_hi & c_hi))
        size *= 2

    def bmm(ls, rs):
        rcat = jnp.concatenate([rs[0:2 * c], rs[2 * c:4 * c]], axis=1).astype(BF16)
        full = jnp.dot(ls.astype(BF16), rcat, preferred_element_type=F32)
        return jnp.concatenate([full[0:2 * c, 0:LANES], full[2 * c:4 * c, LANES:2 * LANES]], axis=0)

    def wide(t_stack, mats):
        rv = jnp.concatenate([jnp.concatenate(mats[0:2], axis=0),
                              jnp.concatenate(mats[2:4], axis=0)], axis=1).astype(BF16)
        full = jnp.dot(t_stack.astype(BF16), rv, preferred_element_type=F32)
        return [full[c * h:c * (h + 1), LANES * (h // 2):LANES * (h // 2 + 1)] for h in range(H_B)]

    dims = (((1,), (1,)), ((), ()))
    head = lambda ref, rows, h: ref[0, rows, DK_B * h:DK_B * (h + 1)]
    combos = []
    for ci in range(nchunks):
        rows = slice(c * ci, c * (ci + 1))
        gb = gb_ref[0, rows, :]
        prefix, suffix = cumsums(gb)
        gc = jnp.where(lane8 < H_B, prefix, suffix)
        gc_t = gc.T
        eg = jnp.exp(gc)
        g_last = jnp.where(lane8 < H_B, gc[c - 1:c, :], gc[0:1, :])
        ge_ref[0, ci] = jnp.broadcast_to(jnp.exp(g_last), (SUBLANES, LANES))
        ek = jnp.exp(g_last - gc)
        raw = []
        for h in range(H_B):
            k_h = head(k_ref, rows, h)
            kq = jnp.concatenate([k_h, head(q_ref, rows, h)], axis=0).astype(BF16)
            kk = jnp.concatenate([k_h, k_h], axis=0).astype(BF16)
            raw.append(lax.dot_general(kq, kk, dims, preferred_element_type=F32))
        for d in range(2):
            keep = (row2 >= colm) if d == 0 else (row2 <= colm)
            strict = (row2 > colm) if d == 0 else (row2 < colm)
            a_blocks, aqk, betas, egs, eks = [], [], [], [], []
            for h in range(H_B):
                idx = H_B * d + h
                g_col = gc[:, idx:idx + 1]
                g_row = jnp.concatenate([gc_t[idx:idx + 1, :]] * 2, axis=1)
                decay = jnp.where(keep, jnp.exp(jnp.where(keep, g_col - g_row, 0.0)), 0.0)
                beta = gb[:, 2 * H_B + idx:2 * H_B + idx + 1]
                betas.append(beta)
                egs.append(eg[:, idx:idx + 1])
                eks.append(ek[:, idx:idx + 1])
                a_blocks.append(jnp.where(strict & half_of[h], beta * raw[h][0:c] * decay, 0.0))
                aqk.append(raw[h][c:2 * c] * decay)
            combos.append(dict(ci=ci, d=d, rows=rows, a=jnp.concatenate(a_blocks, axis=0),
                               aqk=aqk, betas=betas, egs=egs, eks=eks))

    xs = [-cb["a"] * base_mask for cb in combos]
    ts = [eye_stack + x for x in xs]
    ps = [bmm(x, x) for x in xs]
    ts = [t + bmm(t, p) for t, p in zip(ts, ps)]
    ps = [bmm(p, p) for p in ps]
    ts = [t + bmm(t, p) for t, p in zip(ts, ps)]
    for lvl in range(len(level_masks[0])):
        mids = [bmm(t, cb["a"] * level_masks[cb["d"]][lvl]) for t, cb in zip(ts, combos)]
        ts = [t - bmm(m, t) for t, m in zip(ts, mids)]

    for t_inv, cb in zip(ts, combos):
        ci, d, rows = cb["ci"], cb["d"], cb["rows"]
        qs = [head(q_ref, rows, h) for h in range(H_B)]
        ks = [head(k_ref, rows, h) for h in range(H_B)]
        us = wide(t_inv, [head(v_ref, rows, h) * cb["betas"][h] for h in range(H_B)])
        ws = wide(t_inv, [ks[h] * (cb["betas"][h] * cb["egs"][h]) for h in range(H_B)])
        for h in range(H_B):
            cols = slice(DK_B * h, DK_B * (h + 1))
            u_ref[d, 0, rows, cols] = us[h]
            w_ref[d, 0, rows, cols] = ws[h].astype(BF16)
            qg_ref[d, 0, rows, cols] = (qs[h] * cb["egs"][h]).astype(BF16)
        for pair in range(H_B // 2):
            h0, h1 = 2 * pair, 2 * pair + 1
            a_ref[d, 0, rows, LANES * pair:LANES * (pair + 1)] = jnp.where(
                col2 < c, cb["aqk"][h0], cb["aqk"][h1]).astype(BF16)
            kd0 = (ks[h0] * cb["eks"][h0]).T
            kd1 = (ks[h1] * cb["eks"][h1]).T
            kdt_ref[d, 0, ci, :, LANES * pair:LANES * (pair + 1)] = jnp.concatenate(
                [kd0, kd1], axis=1).astype(BF16)


def _gdn_chunk(qb, kb, vb, gb):
    b, t, nh = qb.shape
    tm = ROW_TILE
    cps = tm // CHUNK
    nck = t // CHUNK
    row = lambda w_: pl.BlockSpec((1, tm, w_), lambda bb, i: (bb, i, 0))
    drow = lambda w_: pl.BlockSpec((2, 1, tm, w_), lambda bb, i: (0, bb, i, 0))
    return pl.pallas_call(
        _gdn_chunk_kernel,
        grid=(b, t // tm),
        in_specs=[row(nh), row(nh), row(nh), row(LANES)],
        out_specs=[drow(nh), drow(nh), drow(nh), drow(nh // 2),
                   pl.BlockSpec((2, 1, cps, DK_B, nh // 2), lambda bb, i: (0, bb, i, 0, 0)),
                   pl.BlockSpec((1, cps, SUBLANES, LANES), lambda bb, i: (bb, i, 0, 0))],
        out_shape=[jax.ShapeDtypeStruct((2, b, t, nh), F32), jax.ShapeDtypeStruct((2, b, t, nh), BF16),
                   jax.ShapeDtypeStruct((2, b, t, nh), BF16),
                   jax.ShapeDtypeStruct((2, b, t, nh // 2), BF16),
                   jax.ShapeDtypeStruct((2, b, nck, DK_B, nh // 2), BF16),
                   jax.ShapeDtypeStruct((b, nck, SUBLANES, LANES), F32)],
        compiler_params=_params("parallel", "parallel"),
        name="gdn_chunk",
    )(qb, kb, vb, gb)


def _gdn_scan_kernel(*refs):
    ins = (refs[0:6], refs[6:12])
    outs, s_sc = refs[12:14], refs[14]
    nb = outs[0].shape[0]

    @pl.when(pl.program_id(0) == 0)
    def _():
        s_sc[...] = jnp.zeros_like(s_sc)

    zero = jnp.zeros((CHUNK, DK_B), BF16)
    chains = [(d, b, h) for d in range(2) for b in range(nb) for h in range(H_B)]
    cols = lambda h: slice(DK_B * h, DK_B * (h + 1))
    pair = lambda h: slice(LANES * (h // 2), LANES * (h // 2 + 1))
    states = [s_sc[d, b, h] for d, b, h in chains]
    rs = [jnp.dot(jnp.concatenate([ins[d][1][0, b, :, cols(h)], ins[d][2][0, b, :, cols(h)]], axis=0),
                  s.astype(BF16), preferred_element_type=F32) for (d, b, h), s in zip(chains, states)]
    v_pads = []
    for (d, b, h), r in zip(chains, rs):
        v_new = (ins[d][0][0, b, :, cols(h)] - r[0:CHUNK]).astype(BF16)
        v_pads.append(jnp.concatenate([v_new, zero] if h % 2 == 0 else [zero, v_new], axis=0))
    for (d, b, h), s, r, v_pad in zip(chains, states, rs, v_pads):
        g_end = ins[d][5][b, 0, 0:1, H_B * d + h:H_B * d + h + 1]
        s_sc[d, b, h] = s * g_end + jnp.dot(ins[d][4][0, b, 0, :, pair(h)], v_pad,
                                            preferred_element_type=F32)
    for (d, b, h), r, v_pad in zip(chains, rs, v_pads):
        outs[d][b, :, cols(h)] = r[CHUNK:2 * CHUNK] + jnp.dot(
            ins[d][3][0, b, :, pair(h)], v_pad, preferred_element_type=F32)


def _gdn_scan(u, w, qg, a, kdt, ge, n_ctx):
    _, b, t, nh = u.shape
    nck = t // CHUNK
    ncc = n_ctx // CHUNK
    chunk = (lambda s: s,
             lambda s: jnp.where(s < ncc, ncc - 1 - s, nck - 1 - (s - ncc)))

    def specs(d):
        drow = lambda w_: pl.BlockSpec((1, b, CHUNK, w_), lambda s: (d, 0, chunk[d](s), 0))
        return [drow(nh), drow(nh), drow(nh), drow(nh // 2),
                pl.BlockSpec((1, b, 1, DK_B, nh // 2), lambda s: (d, 0, chunk[d](s), 0, 0)),
                pl.BlockSpec((b, 1, SUBLANES, LANES), lambda s: (0, chunk[d](s), 0, 0))]

    return pl.pallas_call(
        _gdn_scan_kernel,
        grid=(nck,),
        in_specs=specs(0) + specs(1),
        out_specs=[pl.BlockSpec((b, CHUNK, nh), lambda s, d=d: (0, chunk[d](s), 0)) for d in range(2)],
        out_shape=[jax.ShapeDtypeStruct((b, t, nh), F32)] * 2,
        scratch_shapes=[pltpu.VMEM((2, b, H_B, DK_B, DK_B), F32)],
        compiler_params=_params("arbitrary"),
        name="gdn_scan",
    )(u, w, qg, a, kdt, ge, u, w, qg, a, kdt, ge)


def _residual_router(y, x_ref, mod_ref, g2_ref, wr_ref, xn_ref, h2_ref, aff_ref):
    xn = x_ref[0] + mod_ref[0, 2:3, :] * y
    xn_ref[0] = xn
    h2 = _norm_mod(xn, g2_ref[...], mod_ref[0, 3:4, :], mod_ref[0, 4:5, :])
    h2_ref[0] = h2
    h_hi = h2.astype(BF16)
    h_lo = (h2 - h_hi.astype(F32)).astype(BF16)
    logits = (jnp.dot(h_hi, wr_ref[0], preferred_element_type=F32)
              + jnp.dot(h_lo, wr_ref[0], preferred_element_type=F32)
              + jnp.dot(h_hi, wr_ref[1], preferred_element_type=F32))
    lane = lax.broadcasted_iota(jnp.int32, logits.shape, 1)
    logits = jnp.where(lane < N_EXPERTS, logits, NEG_INF)
    e = jnp.exp(logits - logits.max(axis=-1, keepdims=True))
    aff_ref[0] = e / e.sum(axis=-1, keepdims=True)


def _outproj_even_kernel(oa_ref, of_ref, ob_ref, z_ref, x_ref, mod_ref, w_ref, onorm_ref, g2_ref, wr_ref,
                         xn_ref, h2_ref, aff_ref):
    na = oa_ref.shape[2]
    y = jnp.dot(oa_ref[0], w_ref[0:na, :], preferred_element_type=F32)
    o = of_ref[0] + ob_ref[0]
    z = z_ref[0]
    for h in range(H_B):
        cols = slice(DK_B * h, DK_B * (h + 1))
        oh = o[:, cols]
        ms = jnp.mean(oh * oh, axis=-1, keepdims=True)
        yh = oh * lax.rsqrt(ms + NORM_EPS) * onorm_ref[...] * _silu(z[:, cols])
        y = y + jnp.dot(yh.astype(BF16), w_ref[na + DK_B * h:na + DK_B * (h + 1), :],
                        preferred_element_type=F32)
    _residual_router(y, x_ref, mod_ref, g2_ref, wr_ref, xn_ref, h2_ref, aff_ref)


def _outproj_odd_kernel(o_ref, x_ref, mod_ref, w_ref, g2_ref, wr_ref, xn_ref, h2_ref, aff_ref):
    y = jnp.dot(o_ref[0], w_ref[...], preferred_element_type=F32)
    _residual_router(y, x_ref, mod_ref, g2_ref, wr_ref, xn_ref, h2_ref, aff_ref)


def _outproj_call(kernel, name, acts, xs, mod, consts, n_ctx, tile0):
    b, t, d = xs.shape
    tm = ROW_TILE
    nt = t // tm - tile0
    specs = []
    for a, lead, off in acts:
        if lead is None:
            specs.append(pl.BlockSpec((1, tm, a.shape[-1]), lambda bb, i, off=off: (bb, i + off, 0)))
        else:
            specs.append(pl.BlockSpec((1, 1, tm, a.shape[-1]),
                                      lambda bb, i, lead=lead, off=off: (lead, bb, i + off, 0)))
    row = pl.BlockSpec((1, tm, d), lambda bb, i: (bb, i + tile0, 0))
    const = lambda a: pl.BlockSpec(a.shape, lambda bb, i: (0,) * a.ndim)
    mod_spec = pl.BlockSpec((1, 6, d), _mod_spec(n_ctx // tm, tile0))
    wout, rest = consts[0], consts[1:]
    return pl.pallas_call(
        kernel,
        grid=(b, nt),
        in_specs=specs + [row, mod_spec, const(wout)] + [const(a) for a in rest],
        out_specs=[pl.BlockSpec((1, tm, d), lambda bb, i: (bb, i, 0)),
                   pl.BlockSpec((1, tm, d), lambda bb, i: (bb, i, 0)),
                   pl.BlockSpec((1, tm, LANES), lambda bb, i: (bb, i, 0))],
        out_shape=[jax.ShapeDtypeStruct((b, nt * tm, d), F32), jax.ShapeDtypeStruct((b, nt * tm, d), F32),
                   jax.ShapeDtypeStruct((b, nt * tm, LANES), F32)],
        compiler_params=_params("parallel", "parallel"),
        name=name,
    )(*[a for a, _, _ in acts], xs, mod, wout, *rest)


def _inproj_odd_kernel(x_ref, mod_ref, gain_ref, w_ref, cos_ref, sin_ref, qn_ref, kn_ref,
                       q_ref, k_ref, v_ref):
    h = _norm_mod(x_ref[0], gain_ref[...], mod_ref[0, 0:1, :], mod_ref[0, 1:2, :]).astype(BF16)
    c = cos_ref[...]
    s = sin_ref[...]
    nq, nk = HQ_C * DH_C, HKV_C * DH_C

    def normed_heads(lo, nheads, gain_ref_, out_ref, scale):
        p = jnp.dot(h, w_ref[:, lo:lo + nheads * DH_C], preferred_element_type=F32)
        for hh in range(nheads):
            ph = p[:, DH_C * hh:DH_C * (hh + 1)]
            ms = jnp.mean(ph * ph, axis=-1, keepdims=True)
            ph = _rope(ph * lax.rsqrt(ms + NORM_EPS) * gain_ref_[...], c, s, DH_C // 4)
            out_ref[0, :, DH_C * hh:DH_C * (hh + 1)] = (ph * scale).astype(BF16)

    normed_heads(0, HQ_C, qn_ref, q_ref, DH_C ** -0.5 * LOG2_E)
    normed_heads(nq, HKV_C, kn_ref, k_ref, 1.0)
    v_ref[0] = jnp.dot(h, w_ref[:, nq + nk:nq + 2 * nk], preferred_element_type=F32).astype(BF16)


def _inproj_odd(xs, mod, gain, w, cos, sin, qn, kn, n_ctx):
    b, t, d = xs.shape
    tm = ROW_TILE
    nq, nk = HQ_C * DH_C, HKV_C * DH_C
    row = lambda w_: pl.BlockSpec((1, tm, w_), lambda bb, i: (bb, i, 0))
    const = lambda a: pl.BlockSpec(a.shape, lambda bb, i: (0,) * a.ndim)
    return pl.pallas_call(
        _inproj_odd_kernel,
        grid=(b, t // tm),
        in_specs=[row(d), pl.BlockSpec((1, 6, d), _mod_spec(n_ctx // tm)), const(gain), const(w),
                  pl.BlockSpec((tm, LANES), lambda bb, i: (i, 0)),
                  pl.BlockSpec((tm, LANES), lambda bb, i: (i, 0)),
                  const(qn), const(kn)],
        out_specs=[row(nq), row(nk), row(nk)],
        out_shape=[jax.ShapeDtypeStruct((b, t, nq), BF16), jax.ShapeDtypeStruct((b, t, nk), BF16),
                   jax.ShapeDtypeStruct((b, t, nk), BF16)],
        compiler_params=_params("parallel", "parallel"),
        name="inproj_odd",
    )(xs, mod, gain, w, cos, sin, qn, kn)


def _attn_c_kernel(q_ref, k_ref, v_ref, o_ref, s_sc, p_sc, m_sc, l_sc, a_sc, acc_sc):
    kv_tile = s_sc.shape[2]
    nchunk = k_ref.shape[1] // kv_tile
    g_heads = HQ_C // HKV_C
    ntile = kv_tile // LANES
    q_tile = q_ref.shape[1]
    nrows = g_heads * q_tile
    dims = (((1,), (1,)), ((), ()))
    qs = [jnp.concatenate([q_ref[0, :, DH_C * (g_heads * h + g):DH_C * (g_heads * h + g + 1)]
                           for g in range(g_heads)], axis=0) for h in range(HKV_C)]
    m_sc[...] = jnp.full(m_sc.shape, NEG_INF, F32)
    l_sc[...] = jnp.zeros(l_sc.shape, F32)
    acc_sc[...] = jnp.zeros(acc_sc.shape, F32)

    def body(ci, carry):
        r0 = pl.multiple_of(ci * kv_tile, kv_tile)
        for h in range(HKV_C):
            s_sc[h] = lax.dot_general(qs[h], k_ref[0, pl.ds(r0, kv_tile), DH_C * h:DH_C * (h + 1)],
                                      dims, preferred_element_type=F32)
        for h in range(HKV_C):
            for rb in range(nrows // SOFTMAX_ROWS):
                rows = slice(SOFTMAX_ROWS * rb, SOFTMAX_ROWS * (rb + 1))
                mx = s_sc[h, rows, 0:LANES]
                for t in range(1, ntile):
                    mx = jnp.maximum(mx, s_sc[h, rows, LANES * t:LANES * (t + 1)])
                m_old = m_sc[h, rows, :]
                m_new = jnp.maximum(m_old, jnp.broadcast_to(mx.max(axis=1, keepdims=True),
                                                            (SOFTMAX_ROWS, LANES)))
                alpha = jnp.exp2(m_old - m_new)
                l_new = alpha * l_sc[h, rows, :]
                for t in range(ntile):
                    p = jnp.exp2(s_sc[h, rows, LANES * t:LANES * (t + 1)] - m_new)
                    l_new = l_new + p
                    p_sc[h, rows, LANES * t:LANES * (t + 1)] = p.astype(BF16)
                l_sc[h, rows, :] = l_new
                m_sc[h, rows, :] = m_new
                a_sc[h, rows, :] = alpha
        for h in range(HKV_C):
            pv = jnp.dot(p_sc[h], v_ref[0, pl.ds(r0, kv_tile), DH_C * h:DH_C * (h + 1)],
                         preferred_element_type=F32)
            acc_sc[h] = a_sc[h] * acc_sc[h] + pv
        return carry

    lax.fori_loop(0, nchunk, body, 0)
    for j in range(HQ_C):
        h, rows = j // g_heads, slice(q_tile * (j % g_heads), q_tile * (j % g_heads + 1))
        o = acc_sc[h, rows, :] / l_sc[h, rows, :].sum(axis=1, keepdims=True)
        o_ref[0, :, DH_C * j:DH_C * (j + 1)] = o.astype(BF16)


def _attn_c(q, k, v, n_ctx):
    b, t, nq = q.shape
    nk = k.shape[2]
    qt = GLOBAL_Q_TILE
    t0 = n_ctx // qt
    rows = HQ_C // HKV_C * qt
    kv_tile = max(w for w in range(LANES, KV_TILE_MAX + 1, LANES) if t % w == 0)
    return pl.pallas_call(
        _attn_c_kernel,
        grid=(b, t // qt - t0),
        in_specs=[pl.BlockSpec((1, qt, nq), lambda bb, i: (bb, i + t0, 0)),
                  pl.BlockSpec((1, t, nk), lambda bb, i: (bb, 0, 0)),
                  pl.BlockSpec((1, t, nk), lambda bb, i: (bb, 0, 0))],
        out_specs=pl.BlockSpec((1, qt, nq), lambda bb, i: (bb, i, 0)),
        out_shape=jax.ShapeDtypeStruct((b, t - n_ctx, nq), BF16),
        scratch_shapes=[pltpu.VMEM((HKV_C, rows, kv_tile), F32), pltpu.VMEM((HKV_C, rows, kv_tile), BF16),
                        pltpu.VMEM((HKV_C, rows, LANES), F32), pltpu.VMEM((HKV_C, rows, LANES), F32),
                        pltpu.VMEM((HKV_C, rows, LANES), F32), pltpu.VMEM((HKV_C, rows, DH_C), F32)],
        compiler_params=_params("parallel", "arbitrary"),
        name="attn_global",
    )(q, k, v)


def _route_group(aff_ref, row0, n, cap, gsel_ref, cex_ref, idx_ref, sel_sc, cin_sc):
    ngrp = LANES // N_EXPERTS
    rows = n // ngrp
    aff_rows = lambda g: aff_ref[0, row0 + rows * g:row0 + rows * (g + 1), :]
    packed = aff_rows(0)
    for g in range(1, ngrp):
        packed = packed + pltpu.roll(aff_rows(g), N_EXPERTS * g, 1)
    lane_p = lax.broadcasted_iota(jnp.int32, (rows, LANES), 1)
    tok = lax.broadcasted_iota(jnp.int32, (rows, LANES), 0) + rows * (lane_p // N_EXPERTS)

    def count(mask):
        c = jnp.broadcast_to(jnp.sum(jnp.where(mask, 1.0, 0.0), axis=0, keepdims=True), (SUBLANES, LANES))
        shift = N_EXPERTS
        while shift < LANES:
            c = c + pltpu.roll(c, shift, 1)
            shift *= 2
        return c[0:1, :]

    def bisect(steps, lo, hi, enough):
        def body(_, c):
            lo_, hi_ = c
            mid = lo_ + ((hi_ - lo_) >> 1)
            ok = enough(mid)
            return jnp.where(ok, mid, lo_), jnp.where(ok, hi_, mid)
        return lax.fori_loop(0, steps, body, (lo, hi))

    as_float = lambda bits: pltpu.bitcast(bits, F32)
    one_bits = 0x3F800001
    lo, hi = bisect(31, jnp.zeros((1, LANES), jnp.int32), jnp.full((1, LANES), one_bits, jnp.int32),
                    lambda mid: count(packed >= as_float(mid)) >= cap)
    thr, nxt = as_float(lo), as_float(hi)
    above = packed >= nxt
    need = cap - count(above)
    tie_tok = jnp.where(packed >= thr, jnp.where(above, n, tok), n)
    cut, _ = bisect(n.bit_length(), jnp.zeros((1, LANES), jnp.int32), jnp.full((1, LANES), n, jnp.int32),
                    lambda mid: count(tie_tok < mid) < need)
    sel_p = jnp.where(above | (tie_tok <= cut), 1.0, 0.0)
    for g in range(ngrp):
        sel_g = sel_p if g == 0 else pltpu.roll(sel_p, LANES - N_EXPERTS * g, 1)
        sel_g = jnp.where(lane_p < N_EXPERTS, sel_g, 0.0)
        sel_sc[rows * g:rows * (g + 1), :] = sel_g
        gsel_ref[0, row0 + rows * g:row0 + rows * (g + 1), :] = sel_g * aff_rows(g)

    blk = min(n, 2 * LANES)
    r_i = lax.broadcasted_iota(jnp.int32, (blk, blk), 0)
    c_i = lax.broadcasted_iota(jnp.int32, (blk, blk), 1)
    tri = (r_i >= c_i).astype(BF16)

    def cum_body(bi, carry):
        r0 = pl.multiple_of(bi * blk, blk)
        s_blk = sel_sc[pl.ds(r0, blk), :]
        c_blk = jnp.dot(tri, s_blk.astype(BF16), preferred_element_type=F32) + carry
        cin_sc[pl.ds(r0, blk), :] = c_blk
        cex_ref[0, pl.ds(pl.multiple_of(row0 + r0, SUBLANES), blk), :] = (c_blk - s_blk).astype(jnp.int32)
        return c_blk[blk - 1:blk, :]

    lax.fori_loop(0, n // blk, cum_body, jnp.zeros((1, LANES), F32))

    if len(idx_ref.shape) == 3:
        slot = lax.broadcasted_iota(jnp.int32, (1, cap), 1).astype(F32)
        for e in range(N_EXPERTS):
            def idx_body(bi, acc):
                r0 = pl.multiple_of(bi * blk, blk)
                col = cin_sc[pl.ds(r0, blk), e:e + 1]
                return acc + jnp.sum(jnp.where(col <= slot, 1.0, 0.0), axis=0, keepdims=True)
            acc = lax.fori_loop(0, n // blk, idx_body, jnp.zeros((1, cap), F32))
            idx_ref[0, e:e + 1, :] = acc.astype(jnp.int32)
        return

    nh = cap // SLOT_DIGIT
    jh = lax.broadcasted_iota(jnp.int32, (nh, blk), 0).astype(F32)
    jl = lax.broadcasted_iota(jnp.int32, (blk, SLOT_DIGIT), 1).astype(F32)
    ones = jnp.ones((blk, SLOT_DIGIT), BF16)
    one_if = lambda m: jnp.where(m, 1.0, 0.0).astype(BF16)

    def table_body(bi, tables):
        r0 = pl.multiple_of(bi * blk, blk)
        c = cin_sc[pl.ds(r0, blk), :]
        hi = jnp.floor(c * (1.0 / SLOT_DIGIT))
        lo = c - SLOT_DIGIT * hi
        hi_t = hi.T
        out = []
        for e in range(N_EXPERTS):
            h_row = hi_t[e:e + 1, :]
            out.append(tables[e]
                       + jnp.dot(one_if(h_row == jh), one_if(lo[:, e:e + 1] <= jl), preferred_element_type=F32)
                       + jnp.dot(one_if(h_row < jh), ones, preferred_element_type=F32))
        return tuple(out)

    tables = lax.fori_loop(0, n // blk, table_body,
                           tuple(jnp.zeros((nh, SLOT_DIGIT), F32) for _ in range(N_EXPERTS)))
    for e in range(N_EXPERTS):
        idx_ref[0, e] = tables[e].astype(jnp.int32)


def _route_kernel(aff_ref, gsel_ref, cex_ref, *rest, groups):
    idx_refs, (sel_sc, cin_sc) = rest[:len(groups)], rest[len(groups):]
    for (row0, n, cap), idx_ref in zip(groups, idx_refs):
        _route_group(aff_ref, row0, n, cap, gsel_ref, cex_ref, idx_ref, sel_sc, cin_sc)


def _route(aff, groups):
    b, tl, _ = aff.shape
    blk = pl.BlockSpec((1, tl, LANES), lambda bb: (bb, 0, 0))
    nmax = max(n for _, n, _ in groups)
    list_shape = lambda cap: ((cap // SLOT_DIGIT, SLOT_DIGIT) if cap >= LIST_TABLE_MIN and cap % SLOT_DIGIT == 0
                              else (cap,))
    outs = pl.pallas_call(
        functools.partial(_route_kernel, groups=groups),
        grid=(b,),
        in_specs=[blk],
        out_specs=[blk, blk] + [pl.BlockSpec((1, N_EXPERTS) + list_shape(cap),
                                             lambda bb, nd=len(list_shape(cap)): (bb,) + (0,) * (nd + 1))
                                for _, _, cap in groups],
        out_shape=[jax.ShapeDtypeStruct((b, tl, LANES), F32), jax.ShapeDtypeStruct((b, tl, LANES), jnp.int32)]
        + [jax.ShapeDtypeStruct((b, N_EXPERTS) + list_shape(cap), jnp.int32) for _, _, cap in groups],
        scratch_shapes=[pltpu.VMEM((nmax, LANES), F32), pltpu.VMEM((nmax, LANES), F32)],
        compiler_params=_params("parallel"),
        name="route",
    )(aff)
    return list(outs[:2]) + [idx.reshape(b, N_EXPERTS, cap) for idx, (_, _, cap) in zip(outs[2:], groups)]


def _expert_ffn_kernel(idx_ref, h_hbm, wg_ref, wu_ref, wd_ref, y_ref, xbuf, xb, sem, *, nrows, nsteps):
    e = pl.program_id(0)
    f = pl.program_id(1)

    ne, nf = pl.num_programs(0), pl.num_programs(1)

    def row_copy(ee, j):
        return pltpu.make_async_copy(h_hbm.at[pl.ds(idx_ref[ee * nrows + j], 1), :],
                                     xbuf.at[pl.ds(j, 1), :], sem.at[0])

    def wait_rows(n):
        pltpu.make_async_copy(h_hbm.at[pl.ds(0, n), :], xbuf.at[pl.ds(0, n), :], sem.at[0]).wait()

    @pl.when(f == 0)
    def _():
        @pl.when(e == 0)
        def _():
            def body(j, c):
                row_copy(0, j).start()
                return c
            lax.fori_loop(0, nrows, body, 0, unroll=GATHER_UNROLL)

        wait_rows(nrows)
        xb[...] = xbuf[...].astype(BF16)
        y_ref[...] = jnp.zeros(y_ref.shape, F32)

    wg = wg_ref[0, 0].astype(BF16)
    wu = wu_ref[0, 0].astype(BF16)
    wd = wd_ref[0, 0].astype(BF16)
    rc = nrows // FFN_ROW_CHUNKS
    for ci in range(FFN_ROW_CHUNKS):
        rows = slice(rc * ci, rc * (ci + 1))
        x = xb[rows, :]
        g = jnp.dot(x, wg, preferred_element_type=F32)
        u = jnp.dot(x, wu, preferred_element_type=F32)
        y_ref[0, rows, :] += jnp.dot((_silu(g) * u).astype(BF16), wd, preferred_element_type=F32)

    per_step = nrows // nsteps
    nxt = jnp.minimum(e + 1, ne - 1)
    for jj in range(per_step):
        row_copy(nxt, f * per_step + jj).start(priority=jj % 2)

    @pl.when((e == ne - 1) & (f == nf - 1))
    def _():
        wait_rows(nrows)


def _expert_ffn(idx_flat, h_flat, w_gate, w_up, w_down, layer, nrows):
    _, e, d, ff = w_gate.shape
    nf = ff // FF_TILE
    return pl.pallas_call(
        functools.partial(_expert_ffn_kernel, nrows=nrows, nsteps=nf),
        grid_spec=pltpu.PrefetchScalarGridSpec(
            num_scalar_prefetch=1,
            grid=(e, nf),
            in_specs=[pl.BlockSpec(memory_space=pl.ANY),
                      pl.BlockSpec((1, 1, d, FF_TILE), lambda ee, f, idx: (layer, ee, 0, f)),
                      pl.BlockSpec((1, 1, d, FF_TILE), lambda ee, f, idx: (layer, ee, 0, f)),
                      pl.BlockSpec((1, 1, FF_TILE, d), lambda ee, f, idx: (layer, ee, f, 0))],
            out_specs=pl.BlockSpec((1, nrows, d), lambda ee, f, idx: (ee, 0, 0),
                                   pipeline_mode=pl.Buffered(1)),
            scratch_shapes=[pltpu.VMEM((nrows, d), F32), pltpu.VMEM((nrows, d), BF16),
                            pltpu.SemaphoreType.DMA((1,))]),
        out_shape=jax.ShapeDtypeStruct((e, nrows, d), F32),
        compiler_params=_params("arbitrary", "arbitrary"),
        name="expert_ffn",
    )(idx_flat, h_flat, w_gate, w_up, w_down)


def _combine_kernel(wide_ref, start_n_ref, start_w_ref, y_hbm, xn_ref, mod_ref, cex_ref, gsel_ref,
                    delta_ref, o_ref, stage, sem):
    nt = pl.num_programs(1)
    step = pl.program_id(0) * nt + pl.program_id(1)
    nsteps = pl.num_programs(0) * nt
    slot = step % 2

    def copies(step_, slot_, start_ref, win):
        return [pltpu.make_async_copy(
            y_hbm.at[pl.ds(pl.multiple_of(start_ref[step_ * N_EXPERTS + e], SUBLANES), win), :],
            stage.at[slot_, pl.ds(e * win, win), :], sem.at[slot_]) for e in range(N_EXPERTS)]

    def by_width(step_, fn):
        @pl.when(wide_ref[step_] == 0)
        def _():
            fn(start_n_ref, WINDOW_NARROW, 0)

        @pl.when(wide_ref[step_] != 0)
        def _():
            fn(start_w_ref, WINDOW_WIDE, 1)

    def start(step_, slot_):
        by_width(step_, lambda ref, win, _: [cp.start() for cp in copies(step_, slot_, ref, win)])

    @pl.when(step == 0)
    def _():
        start(0, 0)

    @pl.when(step + 1 < nsteps)
    def _():
        start(step + 1, 1 - slot)

    gate = gsel_ref[0]
    cex = cex_ref[0]
    tile = xn_ref.shape[1]

    def reduce(start_ref, win, delta_row):
        for cp in copies(step, slot, start_ref, win):
            cp.wait()
        rpos = cex + delta_ref[0, 0, delta_row:delta_row + 1, :]
        acc = None
        for c0 in range(0, N_EXPERTS * win, LANES):
            lane_r = c0 + lax.broadcasted_iota(jnp.int32, (tile, LANES), 1)
            q = jnp.zeros((tile, LANES), F32)
            for e in range(c0 // win, min(N_EXPERTS - 1, (c0 + LANES - 1) // win) + 1):
                q = jnp.where(rpos[:, e:e + 1] == lane_r, gate[:, e:e + 1], q)
            part = jnp.dot(q.astype(BF16), stage[slot, c0:c0 + LANES, :].astype(BF16),
                           preferred_element_type=F32)
            acc = part if acc is None else acc + part
        o_ref[0] = xn_ref[0] + mod_ref[0, 5:6, :] * acc

    by_width(step, reduce)


def _combine(wide, start_n, start_w, y_flat, xn, mod, cex, gsel, delta, n_ctx_tiles):
    b, tl, d = xn.shape
    nt = tl // MOE_TILE
    tok = lambda w_: pl.BlockSpec((1, MOE_TILE, w_), lambda bb, i, *_: (bb, i, 0))
    return pl.pallas_call(
        _combine_kernel,
        grid_spec=pltpu.PrefetchScalarGridSpec(
            num_scalar_prefetch=3,
            grid=(b, nt),
            in_specs=[pl.BlockSpec(memory_space=pl.ANY), tok(d),
                      pl.BlockSpec((1, 6, d), lambda bb, i, *_: (jnp.where(i < n_ctx_tiles, 0, 1 + bb), 0, 0)),
                      tok(LANES), tok(LANES),
                      pl.BlockSpec((1, 1, SUBLANES, LANES), lambda bb, i, *_: (bb, i, 0, 0))],
            out_specs=tok(d),
            scratch_shapes=[pltpu.VMEM((2, N_EXPERTS * WINDOW_WIDE, d), F32),
                            pltpu.SemaphoreType.DMA((2,))]),
        out_shape=jax.ShapeDtypeStruct((b, tl, d), F32),
        compiler_params=_params("arbitrary", "arbitrary"),
        name="moe_combine",
    )(wide, start_n, start_w, y_flat, xn, mod, cex, gsel, delta)


def _moe(h2, aff, xn, mod, groups, w_gate, w_up, w_down, layer):
    b, tl, d = h2.shape
    ne = w_gate.shape[1]
    caps = [max(1, CAP_FACTOR * n // ne) for _, n in groups]
    nrows = sum(b * cap for cap in caps)
    assert all(n % MOE_TILE == 0 for _, n in groups) and all(cap % SUBLANES == 0 for cap in caps)
    gsel, cex, *idxs = _route(aff, [(row0, n, cap) for (row0, n), cap in zip(groups, caps)])
    idx_parts, src_parts, cnt_parts = [], [], []
    base = 0
    batch = jnp.arange(b, dtype=jnp.int32)
    for (row0, n), cap, idx in zip(groups, caps, idxs):
        rows = idx + (batch * tl + row0)[:, None, None]
        idx_parts.append(jnp.swapaxes(rows, 0, 1).reshape(ne, b * cap))
        s0 = cex[:, row0:row0 + n:MOE_TILE, :ne]
        ends = jnp.concatenate([s0[:, 1:], jnp.full((b, 1, ne), cap, jnp.int32)], axis=1)
        first = (base + batch * cap)[:, None, None] + (jnp.arange(ne, dtype=jnp.int32) * nrows)[None, None, :]
        src_parts.append(jnp.stack([s0 + first, jnp.broadcast_to(first, s0.shape)], axis=0))
        cnt_parts.append(ends - s0)
        base += b * cap
    cat = lambda parts, axis: parts[0] if len(parts) == 1 else jnp.concatenate(parts, axis=axis)
    src, first = cat(src_parts, 2)
    cnt = cat(cnt_parts, 1)
    wide = (cnt > WINDOW_NARROW - (SUBLANES - 1)).any(axis=-1).astype(jnp.int32)
    starts, deltas = [], []
    for win in (WINDOW_NARROW, WINDOW_WIDE):
        st = jnp.minimum(src // SUBLANES * SUBLANES, ne * nrows - win)
        starts.append(st.reshape(-1))
        deltas.append(jnp.arange(ne, dtype=jnp.int32) * win + first - st)
    delta = jnp.pad(jnp.stack(deltas, axis=2), ((0, 0), (0, 0), (0, SUBLANES - 2), (0, LANES - ne)))
    y = _expert_ffn(cat(idx_parts, 1).reshape(-1), h2.reshape(b * tl, d), w_gate, w_up, w_down, layer, nrows)
    return _combine(wide.reshape(-1), starts[0], starts[1], y.reshape(ne * nrows, d), xn, mod, cex, gsel,
                    delta, groups[0][1] // MOE_TILE if len(groups) > 1 else 0)


def _rope_tables(s_len, n_ctx, head_dim):
    quarter = head_dim // 4
    t = jnp.arange(s_len)
    row = (t // GRID_W).astype(F32)
    col = (t % GRID_W).astype(F32)
    inv = ROPE_THETA ** (-jnp.arange(quarter, dtype=F32) / quarter)
    ar, ac = row[:, None] * inv, col[:, None] * inv
    cos = jnp.concatenate([jnp.cos(ar), jnp.cos(ar), jnp.cos(ac), jnp.cos(ac)], axis=1)
    sin = jnp.concatenate([-jnp.sin(ar), jnp.sin(ar), -jnp.sin(ac), jnp.sin(ac)], axis=1)
    reps = LANES // head_dim
    cos, sin = jnp.tile(cos, (1, reps)), jnp.tile(sin, (1, reps))
    cos = jnp.concatenate([jnp.ones((n_ctx, LANES), F32), cos], axis=0)
    sin = jnp.concatenate([jnp.zeros((n_ctx, LANES), F32), sin], axis=0)
    return cos, sin


def _even_weight(w):
    nq, nkv = HQ_A * DH_A, HKV_A * DH_A
    q = w[:, :nq]
    dup = lambda m: jnp.concatenate(
        [m[:, DH_A * (h // 2):DH_A * (h // 2 + 1)] for h in range(2 * HKV_A)], axis=1)
    k = dup(w[:, nq:nq + nkv])
    v = dup(w[:, nq + nkv:nq + 2 * nkv])
    o = nq + 2 * nkv
    nconv, nz = 3 * H_B * DK_B, H_B * DK_B
    conv = w[:, o:o + nconv]
    z = w[:, o + nconv:o + nconv + nz]
    ab = w[:, o + nconv + nz:]
    ab = jnp.pad(ab, ((0, 0), (0, LANES - ab.shape[1])))
    return jnp.concatenate([q, k, v, conv, z, ab], axis=1).astype(BF16)


def _lane_vec(v):
    v = v.reshape(1, -1)
    return jnp.pad(v, ((0, 0), (0, LANES - v.shape[1])))


def kernel(x, c, ctx, c_ctx, w_mod, b_mod, norm_mix, norm_ffn, w_in_ab, w_out_ab, qnorm_a, knorm_a,
           sink_a, conv_b, a_log_b, dt_bias_b, onorm_b, w_in_c, w_out_c, qnorm_c, knorm_c,
           w_router, w_gate, w_up, w_down):
    b, s_len, d = x.shape
    n_ctx = ctx.shape[1]
    depth = w_mod.shape[0]
    assert b + 1 <= SUBLANES and n_ctx % ROW_TILE == 0 and s_len % ROW_TILE == 0
    t_all = n_ctx + s_len

    cvec = jnp.concatenate([c_ctx[None], c, jnp.zeros((SUBLANES - 1 - b, d), F32)], axis=0)
    mod = _modulation(cvec, w_mod, b_mod).reshape(depth, SUBLANES, 6, d)
    cos_a, sin_a = _rope_tables(s_len, n_ctx, DH_A)
    cos_c, sin_c = _rope_tables(s_len, n_ctx, DH_C)
    nq_a = HQ_A * DH_A
    seg = jnp.arange(nq_a) // DH_A
    ones_bd = (seg[:, None] == seg[None, :]).astype(BF16)

    xs = jnp.concatenate([ctx, x], axis=1)
    for i in range(depth):
        last = i == depth - 1
        j = i // 2
        gain1 = norm_mix[i].reshape(1, d)
        gain2 = norm_ffn[i].reshape(1, d)
        w_r = jnp.pad(w_router[i], ((0, 0), (0, LANES - N_EXPERTS)))
        w_r_hi = w_r.astype(BF16)
        w_r = jnp.stack([w_r_hi, (w_r - w_r_hi.astype(F32)).astype(BF16)])
        if i % 2 == 0:
            q, k, v, pc, z, ab = _inproj_even(
                xs, mod[i], gain1, _even_weight(w_in_ab[j]), cos_a, sin_a,
                jnp.tile(qnorm_a[j], HQ_A).reshape(1, -1), jnp.tile(knorm_a[j], 2 * HKV_A).reshape(1, -1),
                ones_bd, n_ctx)
            oa = _attn_a(sink_a[j], q, k, v, n_ctx)
            qb, kb, vb, gb = _gdn_prep(pc, conv_b[j], ab, _lane_vec(a_log_b[j]), _lane_vec(dt_bias_b[j]),
                                       n_ctx)
            u, w, qg, a, kdt, ge = _gdn_chunk(qb, kb, vb, gb)
            o_fwd, o_bwd = _gdn_scan(u, w, qg, a, kdt, ge, n_ctx)
            tile0 = n_ctx // ROW_TILE if last else 0
            xn, h2, aff = _outproj_call(
                _outproj_even_kernel, "outproj_even",
                [(oa, None, tile0), (o_fwd, None, tile0), (o_bwd, None, tile0), (z, None, tile0)], xs, mod[i],
                [w_out_ab[j].astype(BF16), onorm_b[j].reshape(1, -1), gain2, w_r], n_ctx, tile0)
        else:
            q, k, v = _inproj_odd(xs, mod[i], gain1, w_in_c[j].astype(BF16), cos_c, sin_c,
                                  qnorm_c[j].reshape(1, -1), knorm_c[j].reshape(1, -1), n_ctx)
            if last:
                o = _attn_c(q, k, v, n_ctx)
                tile0 = n_ctx // ROW_TILE
            else:
                raise NotImplementedError("context queries of a non-final odd layer")
            xn, h2, aff = _outproj_call(
                _outproj_odd_kernel, "outproj_odd", [(o, None, 0)], xs, mod[i],
                [w_out_c[j].astype(BF16), gain2, w_r], n_ctx, tile0)
        groups = [(0, s_len)] if last else [(0, n_ctx), (n_ctx, s_len)]
        xs = _moe(h2, aff, xn, mod[i], groups, w_gate, w_up, w_down, i)
    return xs
```
